```python
import math
import jax, jax.numpy as jnp
from jax import lax
import numpy as np

D_MODEL = 2048
BATCH = 4
SEQ = 2048
DEPTH = 4
DEC_BATCH = 128
DEC_SEQ = 4
PAST_LEN = 16384
PAGE_SIZE = 128

BRANCH_W = D_MODEL // 2
N_BRANCH = 5
S5_GROUP = 16
S5_GROUPS = BRANCH_W // S5_GROUP
S5_STATE = 64
HG_HEADS = 8
HG_DK = BRANCH_W // HG_HEADS
HG_DV = BRANCH_W // HG_HEADS
GLA_HEADS = 4
GLA_KEY_W = BRANCH_W // 2
GLA_DK = GLA_KEY_W // GLA_HEADS
GLA_DV = BRANCH_W // GLA_HEADS
GLA_RANK = 16
GLA_GATE_TEMP = 16.0
LRU_BLOCKS = 8
LRU_BW = BRANCH_W // LRU_BLOCKS
CONV_W = 4
LRU_C = 8.0
N_MEM = 256
MEM_HEADS = 4
MEM_HD = BRANCH_W // MEM_HEADS

CHUNK = 64
EPS = 1e-6

IN_SPLITS = (
    ("a_x", BRANCH_W), ("a_g", BRANCH_W),
    ("b_q", BRANCH_W), ("b_f", BRANCH_W), ("b_i", BRANCH_W), ("b_g", BRANCH_W),
    ("c_q", GLA_KEY_W), ("c_k", GLA_KEY_W), ("c_v", BRANCH_W), ("c_r", GLA_RANK), ("c_g", BRANCH_W),
    ("d_x", BRANCH_W), ("d_g", BRANCH_W),
    ("e_q", BRANCH_W), ("e_g", BRANCH_W),
)
IN_WIDTH = sum(w for _, w in IN_SPLITS)

kernel_name = "hybrid_s5_hgrn2_gla_rglru_memxattn_step"

F32 = jnp.float32


def rmsnorm(x, g):
    xf = x.astype(F32)
    y = xf * lax.rsqrt(jnp.mean(xf * xf, axis=-1, keepdims=True) + EPS)
    return (y * g.astype(F32)).astype(x.dtype)


def split_cols(proj):
    out = {}
    off = 0
    for name, w in IN_SPLITS:
        out[name] = proj[..., off:off + w]
        off += w
    return out


def s5_branch(u, h0_re, h0_im, lam_re, lam_im, log_dt, b_re, b_im, c_re, c_im, d_skip, w_glu):
    n, t, _ = u.shape
    uf = u.astype(F32)
    ug = uf.reshape(n, t, S5_GROUPS, S5_GROUP)
    dt = jnp.exp(log_dt.astype(F32))[:, None]
    lr, li = lam_re.astype(F32), lam_im.astype(F32)
    mag = jnp.exp(lr * dt)
    ar, ai = mag * jnp.cos(li * dt), mag * jnp.sin(li * dt)
    den = lr * lr + li * li
    zr = ((ar - 1.0) * lr + ai * li) / den
    zi = (ai * lr - (ar - 1.0) * li) / den
    br, bi = b_re.astype(F32), b_im.astype(F32)
    bbr = zr[..., None] * br - zi[..., None] * bi
    bbi = zr[..., None] * bi + zi[..., None] * br
    xr = jnp.einsum("ntgh,gph->ntgp", ug, bbr)
    xi = jnp.einsum("ntgh,gph->ntgp", ug, bbi)
    h0r, h0i = h0_re.astype(F32), h0_im.astype(F32)
    xr = xr.at[:, 0].add(ar * h0r - ai * h0i)
    xi = xi.at[:, 0].add(ar * h0i + ai * h0r)
    are = jnp.broadcast_to(ar, xr.shape)
    aim = jnp.broadcast_to(ai, xi.shape)

    def combine(e1, e2):
        a1r, a1i, b1r, b1i = e1
        a2r, a2i, b2r, b2i = e2
        return (a2r * a1r - a2i * a1i, a2r * a1i + a2i * a1r,
                a2r * b1r - a2i * b1i + b2r, a2r * b1i + a2i * b1r + b2i)

    _, _, hr, hi = lax.associative_scan(combine, (are, aim, xr, xi), axis=1)
    y = jnp.einsum("gop,ntgp->ntgo", c_re.astype(F32), hr) - jnp.einsum("gop,ntgp->ntgo", c_im.astype(F32), hi)
    y = y.reshape(n, t, BRANCH_W) + d_skip.astype(F32) * uf
    z = jax.nn.gelu(y)
    out = z * jax.nn.sigmoid(z @ w_glu.astype(F32))
    return out, hr[:, -1], hi[:, -1]


def chunk_gla(q, k, v, log_f, s0):
    n, t, h, dk = q.shape
    dv = v.shape[-1]
    L = min(CHUNK, t)
    nc = -(-t // L)
    pad = nc * L - t
    if pad:
        pw = ((0, 0), (0, pad), (0, 0), (0, 0))
        q, k, v, log_f = [jnp.pad(a, pw) for a in (q, k, v, log_f)]

    def to_chunks(a):
        return a.reshape(n, nc, L, h, a.shape[-1]).transpose(1, 0, 3, 2, 4)

    qc, kc, vc, fc = [to_chunks(a) for a in (q, k, v, log_f)]
    mask = jnp.tril(jnp.ones((L, L), dtype=bool))[:, :, None]

    def step(S, inp):
        qi, ki, vi, gi = inp
        b = jnp.cumsum(gi, axis=2)
        inter = jnp.einsum("nhtk,nhkv->nhtv", qi * jnp.exp(b), S)
        diff = b[:, :, :, None, :] - b[:, :, None, :, :]
        decay = jnp.where(mask, jnp.exp(jnp.where(mask, diff, 0.0)), 0.0)
        att = jnp.einsum("nhtk,nhsk,nhtsk->nhts", qi, ki, decay)
        intra = jnp.einsum("nhts,nhsv->nhtv", att, vi)
        b_last = b[:, :, -1]
        S_new = jnp.exp(b_last)[..., None] * S + jnp.einsum(
            "nhsk,nhsv->nhkv", ki * jnp.exp(b_last[:, :, None] - b), vi)
        return S_new, inter + intra

    S, o = lax.scan(step, s0, (qc, kc, vc, fc))
    o = o.transpose(1, 0, 3, 2, 4).reshape(n, nc * L, h, dv)[:, :t]
    return o, S


def hgrn2_branch(q, fz, i, s0, lb, g_norm):
    n, t, _ = q.shape
    shp = (n, t, HG_HEADS, HG_DK)
    fz = fz.astype(F32)
    lb = lb.astype(F32)
    log_f = jnp.logaddexp(jnp.log(lb), jnp.log1p(-lb) + jax.nn.log_sigmoid(fz))
    k = (1.0 - lb) * jax.nn.sigmoid(-fz)
    o, s = chunk_gla(q.astype(F32).reshape(shp), k.reshape(shp),
                     i.astype(F32).reshape(n, t, HG_HEADS, HG_DV), log_f.reshape(shp), s0.astype(F32))
    o = rmsnorm(o, g_norm).reshape(n, t, BRANCH_W)
    return o, s


def gla_branch(q, k, v, r, s0, w_up, b_up, g_norm):
    n, t, _ = q.shape
    shp = (n, t, GLA_HEADS, GLA_DK)
    log_a = jax.nn.log_sigmoid(r.astype(F32) @ w_up.astype(F32) + b_up.astype(F32)) / GLA_GATE_TEMP
    qh = q.astype(F32).reshape(shp) * (GLA_DK ** -0.5)
    o, s = chunk_gla(qh, k.astype(F32).reshape(shp), v.astype(F32).reshape(n, t, GLA_HEADS, GLA_DV),
                     log_a.reshape(shp), s0.astype(F32))
    o = rmsnorm(o, g_norm).reshape(n, t, BRANCH_W)
    return o, s


def rglru_branch(xd, conv_buf, h0, conv_w, conv_b, w_r, b_r, w_i, b_i, lam):
    n, t, _ = xd.shape
    xcat = jnp.concatenate([conv_buf.astype(F32), xd.astype(F32)], axis=1)
    cw = conv_w.astype(F32)
    xc = conv_b.astype(F32) + sum(cw[j] * xcat[:, j:j + t] for j in range(CONV_W))
    new_buf = xcat[:, t:]
    xb = xc.reshape(n, t, LRU_BLOCKS, LRU_BW)
    r = jax.nn.sigmoid(jnp.einsum("ntbi,bio->ntbo", xb, w_r.astype(F32)).reshape(n, t, BRANCH_W) + b_r.astype(F32))
    ig = jax.nn.sigmoid(jnp.einsum("ntbi,bio->ntbo", xb, w_i.astype(F32)).reshape(n, t, BRANCH_W) + b_i.astype(F32))
    log_a = -LRU_C * r * jax.nn.softplus(-lam.astype(F32))
    a = jnp.exp(log_a)
    u = jnp.sqrt(-jnp.expm1(2.0 * log_a)) * (ig * xc)
    u = u.at[:, 0].add(a[:, 0] * h0.astype(F32))

    def combine(e1, e2):
        a1, b1 = e1
        a2, b2 = e2
        return a1 * a2, a2 * b1 + b2

    _, hseq = lax.associative_scan(combine, (a, u), axis=1)
    return hseq, hseq[:, -1], new_buf


def mem_attend(q, mk, mv):
    n, t, _ = q.shape
    qh = q.astype(F32).reshape(n, t, MEM_HEADS, MEM_HD) * (MEM_HD ** -0.5)
    s = jnp.einsum("nthd,nmhd->nhtm", qh, mk.astype(F32))
    pr = jax.nn.softmax(s, axis=-1)
    return jnp.einsum("nhtm,nmhd->nthd", pr, mv.astype(F32)).reshape(n, t, BRANCH_W)


def trunk(x, mem_k, mem_v, s5_re, s5_im, hg_s, gla_s, lru_h, conv_buf, hg_lb, weights):
    (norm_pre, norm_post, w_in, s5_lam_re, s5_lam_im, s5_log_dt, s5_b_re, s5_b_im, s5_c_re, s5_c_im,
     s5_d, s5_w_glu, hg_norm, gla_w_up, gla_b_up, gla_norm, conv_w, conv_b, lru_w_r, lru_b_r,
     lru_w_i, lru_b_i, lru_lam, w_branch, w_merge, w_out) = weights
    n, t, _ = x.shape
    outs = ([], [], [], [], [], [])
    for l in range(DEPTH):
        h = rmsnorm(x, norm_pre[l])
        p = split_cols(h @ w_in[l])
        oa, a_re, a_im = s5_branch(p["a_x"], s5_re[l], s5_im[l], s5_lam_re[l], s5_lam_im[l], s5_log_dt[l],
                                   s5_b_re[l], s5_b_im[l], s5_c_re[l], s5_c_im[l], s5_d[l], s5_w_glu[l])
        ob, hs = hgrn2_branch(p["b_q"], p["b_f"], p["b_i"], hg_s[l], hg_lb[l], hg_norm[l])
        oc, gs = gla_branch(p["c_q"], p["c_k"], p["c_v"], p["c_r"], gla_s[l], gla_w_up[l], gla_b_up[l], gla_norm[l])
        od, lh, cb = rglru_branch(p["d_x"], conv_buf[l], lru_h[l], conv_w[l], conv_b[l],
                                  lru_w_r[l], lru_b_r[l], lru_w_i[l], lru_b_i[l], lru_lam[l])
        oe = mem_attend(p["e_q"], mem_k[l], mem_v[l])
        silu = lambda g: jax.nn.silu(g.astype(F32))
        branches = jnp.stack([oa * silu(p["a_g"]), ob * silu(p["b_g"]), oc * silu(p["c_g"]),
                              od * silu(p["d_g"]), oe * silu(p["e_g"])], axis=2).astype(x.dtype)
        pb = jnp.einsum("ntcw,cwd->ntcd", branches, w_branch[l])
        gate = jax.nn.sigmoid(h @ w_merge[l]).reshape(n, t, N_BRANCH, D_MODEL)
        merged = jnp.einsum("ntcd,ntcd->ntd", gate, pb)
        x = x + rmsnorm(merged @ w_out[l], norm_post[l])
        for lst, val in zip(outs, (a_re, a_im, hs, gs, lh, cb)):
            lst.append(val.astype(x.dtype))
    return x, [jnp.stack(lst) for lst in outs]


def setup_inputs(seed: int = 0) -> dict:
    key = jax.random.key(seed)
    counter = [0]

    def nxt():
        counter[0] += 1
        return jax.random.fold_in(key, counter[0])

    def nrm(shape, scale):
        return scale * jax.random.normal(nxt(), shape, F32)

    def gain(shape):
        return 1.0 + nrm(shape, 0.05)

    W = BRANCH_W
    d = {}
    d["x_prompt"] = nrm((BATCH, SEQ, D_MODEL), 1.0)
    d["x_sample"] = nrm((DEC_BATCH, DEC_SEQ, D_MODEL), 1.0)
    d["mem_prompt"] = nrm((BATCH, N_MEM, D_MODEL), 1.0)
    d["state_s5_re"] = nrm((DEPTH, DEC_BATCH, S5_GROUPS, S5_STATE), 0.5)
    d["state_s5_im"] = nrm((DEPTH, DEC_BATCH, S5_GROUPS, S5_STATE), 0.5)
    d["state_hgrn"] = nrm((DEPTH, DEC_BATCH, HG_HEADS, HG_DK, HG_DV), 0.3)
    d["state_gla"] = nrm((DEPTH, DEC_BATCH, GLA_HEADS, GLA_DK, GLA_DV), 0.3)
    d["state_rglru"] = nrm((DEPTH, DEC_BATCH, W), 0.5)
    d["state_conv"] = nrm((DEPTH, DEC_BATCH, CONV_W - 1, W), 1.0)
    d["cache_mem_k"] = nrm((DEPTH, DEC_BATCH, N_MEM, MEM_HEADS, MEM_HD), 1.0)
    d["cache_mem_v"] = nrm((DEPTH, DEC_BATCH, N_MEM, MEM_HEADS, MEM_HD), 1.0)
    d["norm_pre"] = gain((DEPTH, D_MODEL))
    d["norm_post"] = gain((DEPTH, D_MODEL))
    d["w_in"] = nrm((DEPTH, D_MODEL, IN_WIDTH), D_MODEL ** -0.5)
    d["s5_lam_re"] = -0.5 * jnp.exp(nrm((DEPTH, S5_GROUPS, S5_STATE), 0.05))
    d["s5_lam_im"] = jnp.pi * jnp.arange(S5_STATE, dtype=F32) + nrm((DEPTH, S5_GROUPS, S5_STATE), 0.01)
    d["s5_log_dt"] = jax.random.uniform(nxt(), (DEPTH, S5_GROUPS), F32, math.log(1e-3), math.log(1e-1))
    d["s5_b_re"] = nrm((DEPTH, S5_GROUPS, S5_STATE, S5_GROUP), (2.0 * S5_GROUP) ** -0.5)
    d["s5_b_im"] = nrm((DEPTH, S5_GROUPS, S5_STATE, S5_GROUP), (2.0 * S5_GROUP) ** -0.5)
    d["s5_c_re"] = nrm((DEPTH, S5_GROUPS, S5_GROUP, S5_STATE), (2.0 * S5_STATE) ** -0.5)
    d["s5_c_im"] = nrm((DEPTH, S5_GROUPS, S5_GROUP, S5_STATE), (2.0 * S5_STATE) ** -0.5)
    d["s5_d"] = nrm((DEPTH, W), 1.0)
    d["s5_w_glu"] = nrm((DEPTH, W, W), W ** -0.5)
    d["hg_lb_logits"] = nrm((DEPTH, W), 0.1)
    d["hg_norm"] = gain((DEPTH, HG_DV))
    d["gla_w_up"] = nrm((DEPTH, GLA_RANK, GLA_KEY_W), GLA_RANK ** -0.5)
    d["gla_b_up"] = nrm((DEPTH, GLA_KEY_W), 0.1)
    d["gla_norm"] = gain((DEPTH, GLA_DV))
    d["conv_w"] = nrm((DEPTH, CONV_W, W), CONV_W ** -0.5)
    d["conv_b"] = nrm((DEPTH, W), 0.01)
    d["lru_w_r"] = nrm((DEPTH, LRU_BLOCKS, LRU_BW, LRU_BW), LRU_BW ** -0.5)
    d["lru_b_r"] = nrm((DEPTH, W), 0.01)
    d["lru_w_i"] = nrm((DEPTH, LRU_BLOCKS, LRU_BW, LRU_BW), LRU_BW ** -0.5)
    d["lru_b_i"] = nrm((DEPTH, W), 0.01)
    a0 = jax.random.uniform(nxt(), (DEPTH, W), F32, 0.9, 0.999)
    s = a0 ** (1.0 / LRU_C)
    d["lru_lam"] = jnp.log(s) - jnp.log1p(-s)
    d["mem_norm"] = gain((DEPTH, D_MODEL))
    d["w_mem_k"] = nrm((DEPTH, D_MODEL, W), D_MODEL ** -0.5)
    d["w_mem_v"] = nrm((DEPTH, D_MODEL, W), D_MODEL ** -0.5)
    d["w_branch"] = nrm((DEPTH, N_BRANCH, W, D_MODEL), W ** -0.5)
    d["w_merge"] = nrm((DEPTH, D_MODEL, N_BRANCH * D_MODEL), D_MODEL ** -0.5)
    d["w_out"] = nrm((DEPTH, D_MODEL, D_MODEL), D_MODEL ** -0.5)
    return d


def reference(x_prompt, x_sample, mem_prompt, state_s5_re, state_s5_im, state_hgrn, state_gla, state_rglru,
              state_conv, cache_mem_k, cache_mem_v, norm_pre, norm_post, w_in, s5_lam_re, s5_lam_im, s5_log_dt,
              s5_b_re, s5_b_im, s5_c_re, s5_c_im, s5_d, s5_w_glu, hg_lb_logits, hg_norm, gla_w_up, gla_b_up,
              gla_norm, conv_w, conv_b, lru_w_r, lru_b_r, lru_w_i, lru_b_i, lru_lam, mem_norm, w_mem_k, w_mem_v,
              w_branch, w_merge, w_out):
    weights = (norm_pre, norm_post, w_in, s5_lam_re, s5_lam_im, s5_log_dt, s5_b_re, s5_b_im, s5_c_re, s5_c_im,
               s5_d, s5_w_glu, hg_norm, gla_w_up, gla_b_up, gla_norm, conv_w, conv_b, lru_w_r, lru_b_r,
               lru_w_i, lru_b_i, lru_lam, w_branch, w_merge, w_out)
    lb = jnp.cumsum(jax.nn.softmax(hg_lb_logits.astype(F32), axis=0), axis=0)
    hg_lb = lb - lb[:1]

    nb = x_prompt.shape[0]
    mk, mv = [], []
    for l in range(DEPTH):
        m = rmsnorm(mem_prompt, mem_norm[l])
        mk.append((m @ w_mem_k[l]).reshape(nb, N_MEM, MEM_HEADS, MEM_HD))
        mv.append((m @ w_mem_v[l]).reshape(nb, N_MEM, MEM_HEADS, MEM_HD))
    p_mem_k = jnp.stack(mk)
    p_mem_v = jnp.stack(mv)
    dt = x_prompt.dtype
    z_s5 = jnp.zeros((DEPTH, nb, S5_GROUPS, S5_STATE), dt)
    z_hg = jnp.zeros((DEPTH, nb, HG_HEADS, HG_DK, HG_DV), dt)
    z_gla = jnp.zeros((DEPTH, nb, GLA_HEADS, GLA_DK, GLA_DV), dt)
    z_lru = jnp.zeros((DEPTH, nb, BRANCH_W), dt)
    z_conv = jnp.zeros((DEPTH, nb, CONV_W - 1, BRANCH_W), dt)
    y_prompt, (p_s5_re, p_s5_im, p_hgrn, p_gla, p_rglru, p_conv) = trunk(
        x_prompt, p_mem_k, p_mem_v, z_s5, z_s5, z_hg, z_gla, z_lru, z_conv, hg_lb, weights)

    y_sample, (s_s5_re, s_s5_im, s_hgrn, s_gla, s_rglru, s_conv) = trunk(
        x_sample, cache_mem_k, cache_mem_v, state_s5_re, state_s5_im, state_hgrn, state_gla,
        state_rglru, state_conv, hg_lb, weights)

    return (y_prompt, y_sample, p_s5_re, p_s5_im, p_hgrn, p_gla, p_rglru, p_conv, p_mem_k, p_mem_v,
            s_s5_re, s_s5_im, s_hgrn, s_gla, s_rglru, s_conv)
```

```python
import functools
import math

import jax
import jax.numpy as jnp
from jax import lax
from jax.experimental import pallas as pl
from jax.experimental.pallas import tpu as pltpu

F32 = jnp.float32
BF16 = jnp.bfloat16
HIGHEST = lax.Precision.HIGHEST

V7X_LANES = 128
V7X_SUBLANES = 8
V7X_VMEM_LIMIT_BYTES = 56 * 1024 * 1024

EPS = 1e-6
D_MODEL = 2048
DEPTH = 4
BATCH = 4
SEQ = 2048
DEC_BATCH = 128
DEC_SEQ = 4
BRANCH_W = 1024
N_BRANCH = 5
S5_GROUP = 16
S5_GROUPS = 64
S5_STATE = 64
S5_CHUNK = 4
S5_GB = 8
HG_HEADS = 8
HG_DK = 128
HG_DV = 128
GLA_HEADS = 4
GLA_KEY_W = 512
GLA_DK = 128
GLA_DV = 256
GLA_RANK = 16
GLA_GATE_TEMP = 16.0
LRU_BLOCKS = 8
LRU_BW = 128
CONV_W = 4
LRU_C = 8.0
N_MEM = 256
MEM_HEADS = 4
MEM_HD = 256


def _params(*sem):
    return pltpu.CompilerParams(dimension_semantics=sem, vmem_limit_bytes=V7X_VMEM_LIMIT_BYTES)


def _dot(a, b):
    return jnp.dot(a, b, preferred_element_type=F32)


def _dot_nt(a, b):
    return lax.dot_general(a, b, (((1,), (1,)), ((), ())), preferred_element_type=F32)


def _dot_tn(a, b):
    return lax.dot_general(a, b, (((0,), (0,)), ((), ())), preferred_element_type=F32)


def _sigmoid(x):
    return 1.0 / (1.0 + jnp.exp(-x))


def _silu(x):
    return x * _sigmoid(x)


def _log_sigmoid(x):
    return jnp.minimum(x, 0.0) - jnp.log1p(jnp.exp(-jnp.abs(x)))


def _softplus(x):
    return jnp.maximum(x, 0.0) + jnp.log1p(jnp.exp(-jnp.abs(x)))


def _gelu_tanh(x):
    return 0.5 * x * (1.0 + jnp.tanh(math.sqrt(2.0 / math.pi) * (x + 0.044715 * (x * x * x))))


def _row_iota(shape):
    return lax.broadcasted_iota(jnp.int32, shape, 0)


def _col_iota(shape):
    return lax.broadcasted_iota(jnp.int32, shape, 1)


def _shift_rows(x, k):
    return pltpu.roll(x, k, 0)


def _shift_rows_up(x, k):
    return pltpu.roll(x, x.shape[0] - k, 0)


def _rmsnorm_kernel(x_ref, g_ref, o_ref):
    x = x_ref[...]
    y = x * lax.rsqrt(jnp.mean(x * x, axis=-1, keepdims=True) + EPS)
    o_ref[...] = (y * g_ref[...]).astype(o_ref.dtype)


def rmsnorm_rows(x, g, tm):
    m, d = x.shape
    return pl.pallas_call(
        _rmsnorm_kernel,
        out_shape=jax.ShapeDtypeStruct((m, d), BF16),
        grid=(m // tm,),
        in_specs=[pl.BlockSpec((tm, d), lambda i: (i, 0)), pl.BlockSpec((1, d), lambda i: (0, 0))],
        out_specs=pl.BlockSpec((tm, d), lambda i: (i, 0)),
        compiler_params=_params("parallel"),
        name="rmsnorm_rows",
    )(x, g)


def _matmul_kernel(a_ref, w_ref, o_ref):
    o_ref[...] = _dot(a_ref[...], w_ref[...]).astype(o_ref.dtype)


def matmul(a, w, tm, tn, out_dtype=F32):
    m, k = a.shape
    n = w.shape[1]
    return pl.pallas_call(
        _matmul_kernel,
        out_shape=jax.ShapeDtypeStruct((m, n), out_dtype),
        grid=(m // tm, n // tn),
        in_specs=[pl.BlockSpec((tm, k), lambda i, j: (i, 0)), pl.BlockSpec((k, tn), lambda i, j: (0, j))],
        out_specs=pl.BlockSpec((tm, tn), lambda i, j: (i, j)),
        compiler_params=_params("parallel", "parallel"),
        name="matmul",
    )(a, w)


def _merge_kernel(h_ref, b0, b1, b2, b3, b4, wm_ref, wb_ref, o_ref, acc_ref):
    c = pl.program_id(2)
    gate = _sigmoid(_dot(h_ref[...], wm_ref[...]))
    for cc, b_ref in enumerate((b0, b1, b2, b3, b4)):
        @pl.when(c == cc)
        def _(b_ref=b_ref, cc=cc):
            contrib = gate * _dot(b_ref[...], wb_ref[...])
            if cc == 0:
                acc_ref[...] = contrib
            else:
                acc_ref[...] += contrib

    @pl.when(c == N_BRANCH - 1)
    def _():
        o_ref[...] = acc_ref[...].astype(o_ref.dtype)


def merge_branches(h, branches, w_merge, w_branch, tm, tn):
    m, d = h.shape
    w = branches[0].shape[1]
    nj = d // tn
    return pl.pallas_call(
        _merge_kernel,
        out_shape=jax.ShapeDtypeStruct((m, d), BF16),
        grid=(m // tm, nj, N_BRANCH),
        in_specs=[pl.BlockSpec((tm, d), lambda i, j, c: (i, 0))]
        + [pl.BlockSpec((tm, w), lambda i, j, c: (i, 0)) for _ in range(N_BRANCH)]
        + [pl.BlockSpec((d, tn), lambda i, j, c: (0, c * nj + j)),
           pl.BlockSpec((None, w, tn), lambda i, j, c: (c, 0, j))],
        out_specs=pl.BlockSpec((tm, tn), lambda i, j, c: (i, j)),
        scratch_shapes=[pltpu.VMEM((tm, tn), F32)],
        compiler_params=_params("parallel", "parallel", "arbitrary"),
        name="merge_branches",
    )(h, *branches, w_merge, w_branch)


def _out_kernel(m_ref, w_ref, g_ref, x_ref, o_ref):
    y = _dot(m_ref[...], w_ref[...])
    y = y * lax.rsqrt(jnp.mean(y * y, axis=-1, keepdims=True) + EPS)
    o_ref[...] = x_ref[...] + y * g_ref[...]


def out_proj_residual(merged, w_out, g, x, tm):
    m, d = x.shape
    return pl.pallas_call(
        _out_kernel,
        out_shape=jax.ShapeDtypeStruct((m, d), F32),
        grid=(m // tm,),
        in_specs=[pl.BlockSpec((tm, d), lambda i: (i, 0)), pl.BlockSpec((d, d), lambda i: (0, 0)),
                  pl.BlockSpec((1, d), lambda i: (0, 0)), pl.BlockSpec((tm, d), lambda i: (i, 0))],
        out_specs=pl.BlockSpec((tm, d), lambda i: (i, 0)),
        compiler_params=_params("parallel"),
        name="out_proj_residual",
    )(merged, w_out, g, x)


def _hg_lb_kernel(logit_ref, lb_ref):
    x = logit_ref[...]
    e = jnp.exp(x - jnp.max(x, axis=0, keepdims=True))
    sm = e / jnp.sum(e, axis=0, keepdims=True)
    acc = jnp.zeros_like(sm[0:1])
    lb_ref[0:1, :] = acc
    for l in range(1, DEPTH):
        acc = acc + sm[l:l + 1]
        lb_ref[l:l + 1, :] = acc


def hg_lower_bounds(logits):
    return pl.pallas_call(
        _hg_lb_kernel,
        out_shape=jax.ShapeDtypeStruct(logits.shape, F32),
        name="hg_lower_bounds",
    )(logits)


def _gla_kernel(*refs, mode, seq_len, nb, heads, dk, dv, has_init):
    L = seq_len
    R = nb * L
    C = heads * dk
    if mode == "hgrn":
        q_ref, f_ref, v_ref, g_ref, lb_ref, gn_ref = refs[:6]
        rest = refs[6:]
    else:
        q_ref, k_ref, v_ref, r_ref, g_ref, wup_ref, bup_ref, gn_ref = refs[:8]
        rest = refs[8:]
    if has_init:
        s0_ref, rest = rest[0], rest[1:]
    o_ref, sout_ref, s_scr, att_scr, qe_scr, kd_scr, b_scr = rest

    c = pl.program_id(1)

    @pl.when(c == 0)
    def _():
        if has_init:
            s_scr[...] = s0_ref[...]
        else:
            s_scr[...] = jnp.zeros_like(s_scr)

    t = _row_iota((R, C)) % L
    if mode == "hgrn":
        lb = lb_ref[...]
        fz = f_ref[...]
        la = jnp.log(lb)
        lc = jnp.log1p(-lb) + _log_sigmoid(fz)
        g = jnp.maximum(la, lc) + jnp.log1p(jnp.exp(-jnp.abs(la - lc)))
        k = (1.0 - lb) * _sigmoid(-fz)
        q = q_ref[...]
    else:
        pre = jnp.dot(r_ref[...], wup_ref[...], preferred_element_type=F32, precision=HIGHEST)
        g = _log_sigmoid(pre + bup_ref[...]) * (1.0 / GLA_GATE_TEMP)
        k = k_ref[...]
        q = q_ref[...] * (dk ** -0.5)

    b = g
    rb = jnp.where(t + 1 < L, _shift_rows_up(g, 1), 0.0)
    s = 1
    while s < L:
        b = b + jnp.where(t >= s, _shift_rows(b, s), 0.0)
        rb = rb + jnp.where(t + s < L, _shift_rows_up(rb, s), 0.0)
        s *= 2
    b_scr[...] = b
    qe_scr[...] = (q * jnp.exp(b)).astype(BF16)
    kd_scr[...] = (k * jnp.exp(rb)).astype(BF16)

    rr = _row_iota((R, R))
    cc = _col_iota((R, R))

    def add_level(level, qt, kt, mask):
        qt = qt.astype(BF16)
        kt = kt.astype(BF16)
        for h in range(heads):
            hs = slice(h * dk, (h + 1) * dk)
            p = jnp.where(mask, _dot_nt(qt[:, hs], kt[:, hs]), 0.0)
            if level == 0:
                att_scr[h] = p
            else:
                att_scr[h] += p

    add_level(0, q, k, rr == cc)
    level = 1
    m = 1
    while m < L:
        pos = t % (2 * m)
        upper = pos >= m
        if m == 1:
            d = jnp.where(upper, g, 0.0)
        elif m == 2:
            d = jnp.where(pos == 2, g,
                          jnp.where(pos == 3, g + _shift_rows(g, 1),
                                    jnp.where(pos == 0, _shift_rows_up(g, 1), 0.0)))
        else:
            b3 = b.reshape(R // (2 * m), 2 * m, C)
            d3 = b3 - b3[:, m - 1:m, :]
            d = d3.reshape(R, C)
            d = jnp.where(upper, d, -d)
        e = jnp.exp(d)
        add_level(level, jnp.where(upper, q * e, 0.0), jnp.where(upper, 0.0, k * e),
                  (rr // (2 * m)) == (cc // (2 * m)))
        level += 1
        m *= 2

    gn = gn_ref[...]
    seq_o = _row_iota((R, dv)) // L
    seq_k = _row_iota((R, dk)) // L
    pad_rows = (-R) % V7X_LANES
    for h in range(heads):
        hs = slice(h * dk, (h + 1) * dk)
        vs = slice(h * dv, (h + 1) * dv)
        vh = v_ref[:, vs].astype(BF16)
        vh_p = jnp.concatenate([vh, jnp.zeros((pad_rows, dv), BF16)], axis=0) if pad_rows else vh
        o = _dot(att_scr[h].astype(BF16), vh)
        qe = qe_scr[:, hs]
        kd = kd_scr[:, hs]
        for j in range(nb):
            st = s_scr[j, h]
            inter = _dot(qe, st.astype(BF16))
            kd_j = kd
            if nb > 1:
                inter = jnp.where(seq_o == j, inter, 0.0)
                kd_j = jnp.where(seq_k == j, kd, jnp.zeros_like(kd))
            o = o + inter
            if pad_rows:
                kd_j = jnp.concatenate([kd_j, jnp.zeros((pad_rows, dk), BF16)], axis=0)
            upd = _dot_tn(kd_j, vh_p)
            dl = jnp.exp(b_scr[j * L + L - 1:j * L + L, hs])
            colb = jnp.broadcast_to(dl, (dk, dk)).T
            if dv != dk:
                colb = jnp.concatenate([colb] * (dv // dk), axis=1)
            s_scr[j, h] = colb * st + upd
        on = o * lax.rsqrt(jnp.mean(o * o, axis=-1, keepdims=True) + EPS) * gn
        o_ref[:, vs] = (on * _silu(g_ref[:, vs])).astype(o_ref.dtype)

    @pl.when(c == pl.num_programs(1) - 1)
    def _():
        sout_ref[...] = s_scr[...]


def gla_branch(mode, p, col_blocks, consts, s0, out_prev, sout_prev, *, layer, row_base, n_seq, seq_total,
               chunk, nb, heads, dk, dv, out_rows):
    L = chunk
    R = nb * L
    n_chunks = seq_total // L
    base_blk = row_base // R
    width = heads * dv

    def rows(i, c):
        return base_blk + i * n_chunks + c

    in_specs = [pl.BlockSpec((R, w), functools.partial(lambda i, c, bi: (rows(i, c), bi), bi=bi))
                for (w, bi) in col_blocks]
    args = [p] * len(col_blocks)
    for a in consts:
        in_specs.append(pl.BlockSpec(a.shape, lambda i, c: (0, 0)))
        args.append(a)
    has_init = s0 is not None
    state_spec = pl.BlockSpec((None, nb, heads, dk, dv), lambda i, c: (layer, i, 0, 0, 0))
    if has_init:
        in_specs.append(state_spec)
        args.append(s0)
    n_real = len(args)
    aliases = {}
    for out_idx, prev in enumerate((out_prev, sout_prev)):
        if prev is not None:
            aliases[len(args)] = out_idx
            in_specs.append(pl.BlockSpec(memory_space=pl.ANY))
            args.append(prev)
    n_in = len(args)

    def kern(*refs):
        _gla_kernel(*refs[:n_real], *refs[n_in:], mode=mode, seq_len=L, nb=nb, heads=heads, dk=dk, dv=dv,
                    has_init=has_init)

    return pl.pallas_call(
        kern,
        out_shape=(jax.ShapeDtypeStruct((out_rows, width), BF16),
                   jax.ShapeDtypeStruct((DEPTH, n_seq, heads, dk, dv), F32)),
        grid=(n_seq // nb, n_chunks),
        in_specs=in_specs,
        out_specs=(pl.BlockSpec((R, width), lambda i, c: (rows(i, c), 0)), state_spec),
        scratch_shapes=[pltpu.VMEM((nb, heads, dk, dv), F32), pltpu.VMEM((heads, R, R), F32),
                        pltpu.VMEM((R, heads * dk), BF16), pltpu.VMEM((R, heads * dk), BF16),
                        pltpu.VMEM((R, heads * dk), F32)],
        input_output_aliases=aliases,
        compiler_params=_params("parallel", "arbitrary"),
        name=mode + "_branch",
    )(*args)


def _lru_gates(xc, wr_ref, br_ref, wi_ref, bi_ref, lam_ref):
    xcb = xc.astype(BF16)
    r_parts, i_parts = [], []
    for blk in range(LRU_BLOCKS):
        bs = slice(blk * LRU_BW, (blk + 1) * LRU_BW)
        r_parts.append(_dot(xcb[:, bs], wr_ref[blk]))
        i_parts.append(_dot(xcb[:, bs], wi_ref[blk]))
    r = _sigmoid(jnp.concatenate(r_parts, axis=1) + br_ref[...])
    ig = _sigmoid(jnp.concatenate(i_parts, axis=1) + bi_ref[...])
    log_a = (-LRU_C) * r * _softplus(-lam_ref[...])
    a = jnp.exp(log_a)
    u = jnp.sqrt(1.0 - jnp.exp(2.0 * log_a)) * (ig * xc)
    return a, u


def _lru_prompt_kernel(x_ref, g_ref, cw_ref, cb_ref, wr_ref, br_ref, wi_ref, bi_ref, lam_ref,
                       o_ref, hfin_ref, cfin_ref, tail_scr, hc_scr, a_scr, u_scr):
    tc, w = x_ref.shape
    sub = V7X_SUBLANES
    c = pl.program_id(1)

    @pl.when(c == 0)
    def _():
        tail_scr[...] = jnp.zeros_like(tail_scr)
        hc_scr[...] = jnp.zeros_like(hc_scr)

    x = x_ref[...]
    tail = tail_scr[...]
    row8 = _row_iota((sub, w))
    cw = cw_ref[...]
    xc = cb_ref[...] + cw[CONV_W - 1:CONV_W] * x
    for k in range(1, CONV_W):
        xs = _shift_rows(x, k)
        head = jnp.where(row8 < k, _shift_rows(tail, k), xs[0:sub])
        xs = jnp.concatenate([head, xs[sub:]], axis=0)
        xc = xc + cw[CONV_W - 1 - k:CONV_W - k] * xs
    last8 = x[tc - sub:tc]
    tail_scr[...] = last8

    a, u = _lru_gates(xc, wr_ref, br_ref, wi_ref, bi_ref, lam_ref)
    a_scr[...] = a
    u_scr[...] = u

    def body(i, carry):
        r0 = pl.multiple_of(i * sub, sub)
        at = a_scr[pl.ds(r0, sub), :]
        ut = u_scr[pl.ds(r0, sub), :]
        for k in (1, 2, 4):
            keep = row8 >= k
            ut = ut + at * jnp.where(keep, _shift_rows(ut, k), 0.0)
            at = at * jnp.where(keep, _shift_rows(at, k), 1.0)
        ht = ut + at * carry
        u_scr[pl.ds(r0, sub), :] = ht
        return jnp.broadcast_to(ht[sub - 1:sub], (sub, w))

    carry = lax.fori_loop(0, tc // sub, body, hc_scr[...])
    hc_scr[...] = carry
    o_ref[...] = (u_scr[...] * _silu(g_ref[...])).astype(o_ref.dtype)

    @pl.when(c == pl.num_programs(1) - 1)
    def _():
        hfin_ref[...] = carry[0:1]
        cfin_ref[...] = _shift_rows(last8, CONV_W - 1)[0:CONV_W - 1]


def lru_prompt(p_d, consts, *, n_seq, seq_total, chunk, out_rows):
    w = BRANCH_W
    n_chunks = seq_total // chunk
    const_specs = [pl.BlockSpec(a.shape, functools.partial(lambda i, c, nd: (0,) * nd, nd=a.ndim)) for a in consts]
    return pl.pallas_call(
        _lru_prompt_kernel,
        out_shape=(jax.ShapeDtypeStruct((out_rows, w), BF16),
                   jax.ShapeDtypeStruct((n_seq, 1, w), F32),
                   jax.ShapeDtypeStruct((n_seq, CONV_W - 1, w), F32)),
        grid=(n_seq, n_chunks),
        in_specs=[pl.BlockSpec((chunk, w), lambda i, c: (i * n_chunks + c, 0)),
                  pl.BlockSpec((chunk, w), lambda i, c: (i * n_chunks + c, 1))] + const_specs,
        out_specs=(pl.BlockSpec((chunk, w), lambda i, c: (i * n_chunks + c, 0)),
                   pl.BlockSpec((None, 1, w), lambda i, c: (i, 0, 0)),
                   pl.BlockSpec((None, CONV_W - 1, w), lambda i, c: (i, 0, 0))),
        scratch_shapes=[pltpu.VMEM((V7X_SUBLANES, w), F32), pltpu.VMEM((V7X_SUBLANES, w), F32),
                        pltpu.VMEM((chunk, w), F32), pltpu.VMEM((chunk, w), F32)],
        compiler_params=_params("parallel", "arbitrary"),
        name="lru_prompt",
    )(p_d, p_d, *consts)


def _lru_sample_kernel(x0, x1, x2, x3, g0, g1, g2, g3, buf_ref, h0_ref,
                       cw_ref, cb_ref, wr_ref, br_ref, wi_ref, bi_ref, lam_ref, prev_ref,
                       o_ref, hfin_ref, cfin_ref):
    del prev_ref
    w = BRANCH_W
    n = x0.shape[0]
    steps = len((x0, x1, x2, x3))
    xcat = [buf_ref[:, j * w:(j + 1) * w] for j in range(CONV_W - 1)] + [r[...] for r in (x0, x1, x2, x3)]
    cw = cw_ref[...]
    xcs = []
    for t in range(steps):
        xc = cb_ref[...]
        for j in range(CONV_W):
            xc = xc + cw[j:j + 1] * xcat[t + j]
        xcs.append(xc)
    a, u = _lru_gates(jnp.concatenate(xcs, axis=0), wr_ref, br_ref, wi_ref, bi_ref, lam_ref)
    h = h0_ref[...]
    for t, g_ref in enumerate((g0, g1, g2, g3)):
        h = a[t * n:(t + 1) * n] * h + u[t * n:(t + 1) * n]
        o_ref[:, t * w:(t + 1) * w] = (h * _silu(g_ref[...])).astype(o_ref.dtype)
    hfin_ref[...] = h
    for j in range(CONV_W - 1):
        cfin_ref[:, j * w:(j + 1) * w] = xcat[steps + j]


def lru_sample(p_d4, conv_buf, h0, consts, out_prev4, *, row_blk):
    w = BRANCH_W
    n = h0.shape[0]
    steps = DEC_SEQ
    const_specs = [pl.BlockSpec(a.shape, functools.partial(lambda i, nd: (0,) * nd, nd=a.ndim)) for a in consts]
    x_specs = [pl.BlockSpec((n, w), functools.partial(lambda i, t: (row_blk, 2 * t), t=t)) for t in range(steps)]
    g_specs = [pl.BlockSpec((n, w), functools.partial(lambda i, t: (row_blk, 2 * t + 1), t=t)) for t in range(steps)]
    return pl.pallas_call(
        _lru_sample_kernel,
        out_shape=(jax.ShapeDtypeStruct(out_prev4.shape, BF16),
                   jax.ShapeDtypeStruct((n, w), F32),
                   jax.ShapeDtypeStruct((n, (CONV_W - 1) * w), F32)),
        grid=(1,),
        in_specs=x_specs + g_specs
        + [pl.BlockSpec((n, (CONV_W - 1) * w), lambda i: (0, 0)), pl.BlockSpec((n, w), lambda i: (0, 0))]
        + const_specs + [pl.BlockSpec(memory_space=pl.ANY)],
        out_specs=(pl.BlockSpec((n, steps * w), lambda i: (row_blk, 0)),
                   pl.BlockSpec((n, w), lambda i: (0, 0)),
                   pl.BlockSpec((n, (CONV_W - 1) * w), lambda i: (0, 0))),
        input_output_aliases={2 * steps + 2 + len(consts): 0},
        compiler_params=_params("arbitrary"),
        name="lru_sample",
    )(*([p_d4] * (2 * steps)), conv_buf, h0, *consts, out_prev4)


def _mem_attn_kernel(*refs, nbm, seq_rows, aliased):
    if aliased:
        q_ref, g_ref, k_ref, v_ref, _, o_ref = refs
    else:
        q_ref, g_ref, k_ref, v_ref, o_ref = refs
    rows = q_ref.shape[0]
    q = (q_ref[...] * (MEM_HD ** -0.5)).astype(BF16)
    seq = _row_iota((rows, MEM_HD)) // seq_rows
    for h in range(MEM_HEADS):
        hs = slice(h * MEM_HD, (h + 1) * MEM_HD)
        acc = None
        for j in range(nbm):
            kh = k_ref[j, :, hs].astype(BF16)
            vh = v_ref[j, :, hs].astype(BF16)
            s = _dot_nt(q[:, hs], kh)
            e = jnp.exp(s - jnp.max(s, axis=-1, keepdims=True))
            pr = e / jnp.sum(e, axis=-1, keepdims=True)
            oh = _dot(pr.astype(BF16), vh)
            acc = oh if acc is None else jnp.where(seq == j, oh, acc)
        o_ref[:, hs] = (acc * _silu(g_ref[:, hs])).astype(o_ref.dtype)


def mem_attention(p_e, mem_k, mem_v, out_prev, *, layer, row_base, n_seq, seq_total, rows_blk, nbm, out_rows):
    w = BRANCH_W
    base_blk = row_base // rows_blk
    if nbm == 1:
        t_blocks = seq_total // rows_blk
        grid = (n_seq, t_blocks)
        rmap = lambda i, t: base_blk + i * t_blocks + t
    else:
        grid = (n_seq // nbm, 1)
        rmap = lambda i, t: base_blk + i
    in_specs = [pl.BlockSpec((rows_blk, w), lambda i, t: (rmap(i, t), 0)),
                pl.BlockSpec((rows_blk, w), lambda i, t: (rmap(i, t), 1)),
                pl.BlockSpec((None, nbm, N_MEM, w), lambda i, t: (layer, i, 0, 0)),
                pl.BlockSpec((None, nbm, N_MEM, w), lambda i, t: (layer, i, 0, 0))]
    args = [p_e, p_e, mem_k, mem_v]
    aliases = {}
    if out_prev is not None:
        in_specs.append(pl.BlockSpec(memory_space=pl.ANY))
        args.append(out_prev)
        aliases = {4: 0}
    return pl.pallas_call(
        functools.partial(_mem_attn_kernel, nbm=nbm, seq_rows=seq_total, aliased=out_prev is not None),
        out_shape=jax.ShapeDtypeStruct((out_rows, w), BF16),
        grid=grid,
        in_specs=in_specs,
        out_specs=pl.BlockSpec((rows_blk, w), lambda i, t: (rmap(i, t), 0)),
        input_output_aliases=aliases,
        compiler_params=_params("parallel", "arbitrary"),
        name="mem_attention",
    )(*args)


def _cmul(ar, ai, br, bi):
    return ar * br - ai * bi, ar * bi + ai * br


def _s5_abar(lr, li, log_dt):
    dt = jnp.exp(log_dt)
    mag = jnp.exp(lr * dt)
    return mag * jnp.cos(li * dt), mag * jnp.sin(li * dt)


def _s5_prep_kernel(lra, lia, dta, bra, bia, cfr, cfi, lrc, lic, dtc, ctr, cti,
                    m_ref, wre_ref, wim_ref, vre_ref, vim_ref):
    gi = S5_GB * S5_GROUP
    gp = S5_GB * S5_STATE
    lr, li = lra[...], lia[...]
    ar, ai = _s5_abar(lr, li, dta[...])
    den = lr * lr + li * li
    zr = ((ar - 1.0) * lr + ai * li) / den
    zi = (ai * lr - (ar - 1.0) * li) / den
    bbr, bbi = _cmul(zr, zi, bra[...], bia[...])
    wmask = (_row_iota((gi, gp)) // S5_GROUP) == (_col_iota((gi, gp)) // S5_STATE)
    mmask = (_row_iota((gi, gi)) // S5_GROUP) == (_col_iota((gi, gi)) // S5_GROUP)
    cr, ci = cfr[...], cfi[...]
    pr, pi = jnp.ones_like(ar), jnp.zeros_like(ar)
    for tau in range(S5_CHUNK):
        lrr, lii = _cmul(pr, pi, bbr, bbi)
        wre_ref[S5_CHUNK - 1 - tau] = jnp.where(wmask, lrr, 0.0).astype(BF16)
        wim_ref[S5_CHUNK - 1 - tau] = jnp.where(wmask, lii, 0.0).astype(BF16)
        x = (lax.dot_general(lrr[:, :S5_STATE], cr, (((1,), (1,)), ((), ())), precision=HIGHEST,
                             preferred_element_type=F32)
             - lax.dot_general(lii[:, :S5_STATE], ci, (((1,), (1,)), ((), ())), precision=HIGHEST,
                               preferred_element_type=F32))
        m_ref[tau] = jnp.where(mmask, x, 0.0).astype(BF16)
        pr, pi = _cmul(pr, pi, ar, ai)

    arc, aic = _s5_abar(lrc[...], lic[...], dtc[...])
    vmask = (_row_iota((gp, gi)) // S5_STATE) == (_col_iota((gp, gi)) // S5_GROUP)
    pr, pi = arc, aic
    for t in range(S5_CHUNK):
        vr, vi = _cmul(ctr[...], cti[...], pr, pi)
        vre_ref[t] = jnp.where(vmask, vr, 0.0).astype(BF16)
        vim_ref[t] = jnp.where(vmask, -vi, 0.0).astype(BF16)
        pr, pi = _cmul(pr, pi, arc, aic)


def s5_prepare(lam_re, lam_im, log_dt, b_re, b_im, c_re, c_im):
    dp, g, p = lam_re.shape
    i = S5_GROUP
    gi, gp = S5_GB * i, S5_GB * p
    nb = g // S5_GB

    def a_layout(x_gp):
        return jnp.broadcast_to(x_gp[:, :, None, None, :], (dp, g, i, S5_GB, p)).reshape(dp, g * i, gp)

    def c_layout(x_gp):
        return jnp.broadcast_to(x_gp[:, :, :, None], (dp, g, p, gi)).reshape(dp, g * p, gi)

    ldt = jnp.broadcast_to(log_dt[:, :, None], (dp, g, p))
    bt = lambda b: jnp.broadcast_to(jnp.transpose(b, (0, 1, 3, 2))[:, :, :, None, :],
                                    (dp, g, i, S5_GB, p)).reshape(dp, g * i, gp)
    ct = lambda c: jnp.broadcast_to(jnp.transpose(c, (0, 1, 3, 2))[:, :, :, None, :],
                                    (dp, g, p, S5_GB, i)).reshape(dp, g * p, gi)
    args = [a_layout(lam_re), a_layout(lam_im), a_layout(ldt), bt(b_re), bt(b_im),
            c_re.reshape(dp, g * i, p), c_im.reshape(dp, g * i, p),
            c_layout(lam_re), c_layout(lam_im), c_layout(ldt), ct(c_re), ct(c_im)]
    spec_a = pl.BlockSpec((None, gi, gp), lambda l, b: (l, b, 0))
    spec_f = pl.BlockSpec((None, gi, p), lambda l, b: (l, b, 0))
    spec_c = pl.BlockSpec((None, gp, gi), lambda l, b: (l, b, 0))
    out5 = lambda r, c: pl.BlockSpec((None, S5_CHUNK, None, r, c), lambda l, b: (l, 0, b, 0, 0))
    shp = lambda r, c: jax.ShapeDtypeStruct((dp, S5_CHUNK, nb, r, c), BF16)
    return pl.pallas_call(
        _s5_prep_kernel,
        out_shape=(shp(gi, gi), shp(gi, gp), shp(gi, gp), shp(gp, gi), shp(gp, gi)),
        grid=(dp, nb),
        in_specs=[spec_a] * 5 + [spec_f] * 2 + [spec_c] * 5,
        out_specs=(out5(gi, gi), out5(gi, gp), out5(gi, gp), out5(gp, gi), out5(gp, gi)),
        compiler_params=_params("parallel", "parallel"),
        name="s5_prepare",
    )(*args)


S5_GB_STEP = 2


def _s5_e_kernel(u0, u1, u2, u3, wre_ref, wim_ref, ere_ref, eim_ref):
    gi, gp = S5_GB * S5_GROUP, S5_GB * S5_STATE
    for half in range(S5_GB_STEP):
        accr = acci = None
        for s, u in enumerate((u0, u1, u2, u3)):
            ub = u[:, half * gi:(half + 1) * gi].astype(BF16)
            r = _dot(ub, wre_ref[s, half])
            i = _dot(ub, wim_ref[s, half])
            accr = r if accr is None else accr + r
            acci = i if acci is None else acci + i
        ere_ref[:, half * gp:(half + 1) * gp] = accr
        eim_ref[:, half * gp:(half + 1) * gp] = acci


def _u_specs(tr, n_idx):
    blocks_per_step = 2 * BRANCH_W // (S5_GB_STEP * S5_GB * S5_GROUP)
    if n_idx == 2:
        return [pl.BlockSpec((tr, S5_GB_STEP * 128), functools.partial(
            lambda i, b, s: (i, s * blocks_per_step + b), s=s)) for s in range(S5_CHUNK)]
    return [pl.BlockSpec((tr, S5_GB_STEP * 128), functools.partial(
        lambda i, b, t, s: (i, s * blocks_per_step + b), s=s)) for s in range(S5_CHUNK)]


def s5_chunk_inputs(p_a4, wre, wim, layer, tr):
    rows = p_a4.shape[0]
    gp = S5_GB * S5_STATE
    n_b = S5_GROUPS // (S5_GB * S5_GB_STEP)
    wspec = pl.BlockSpec((None, S5_CHUNK, S5_GB_STEP, S5_GB * S5_GROUP, gp), lambda i, b: (layer, 0, b, 0, 0))
    ospec = pl.BlockSpec((tr, S5_GB_STEP * gp), lambda i, b: (i, b))
    oshape = jax.ShapeDtypeStruct((rows, S5_GROUPS * S5_STATE), F32)
    return pl.pallas_call(
        _s5_e_kernel,
        out_shape=(oshape, oshape),
        grid=(rows // tr, n_b),
        in_specs=_u_specs(tr, 2) + [wspec, wspec],
        out_specs=(ospec, ospec),
        compiler_params=_params("parallel", "parallel"),
        name="s5_chunk_inputs",
    )(p_a4, p_a4, p_a4, p_a4, wre, wim)


def _s5_scan_kernel(ere, eim, h0r, h0i, lr_ref, li_ref, ldt_ref, hre, him, fpr, fpi, fsr, fsi,
                    *, n_seq, rows_per_seq):
    sub = V7X_SUBLANES
    cw = ere.shape[1]
    ar, ai = _s5_abar(lr_ref[...], li_ref[...], ldt_ref[...])
    a2 = _cmul(ar, ai, ar, ai)
    p1 = _cmul(*a2, *a2)
    p2 = _cmul(*p1, *p1)
    p4 = _cmul(*p2, *p2)
    p8 = _cmul(*p4, *p4)
    row8 = _row_iota((sub, cw))
    tr_, ti_ = jnp.ones((sub, cw), F32), jnp.zeros((sub, cw), F32)
    for bit, pw in ((1, p1), (2, p2), (4, p4)):
        nr, ni = _cmul(tr_, ti_, *pw)
        sel = (row8 & bit) != 0
        tr_, ti_ = jnp.where(sel, nr, tr_), jnp.where(sel, ni, ti_)

    base = n_seq * rows_per_seq
    n_s = h0r.shape[0]
    h0r_v, h0i_v = h0r[...], h0i[...]
    hre[base:base + n_s, :] = h0r_v
    him[base:base + n_s, :] = h0i_v
    dr, di = _cmul(p1[0], p1[1], h0r_v, h0i_v)
    fsr[...] = dr + ere[base:base + n_s, :]
    fsi[...] = di + eim[base:base + n_s, :]

    for n in range(n_seq):
        def body(j, carry, n=n):
            cr, ci = carry
            r0 = pl.multiple_of(n * rows_per_seq + j * sub, sub)
            xr = ere[pl.ds(r0, sub), :]
            xi = eim[pl.ds(r0, sub), :]
            for k, pw in ((1, p1), (2, p2), (4, p4)):
                keep = row8 >= k
                sr = jnp.where(keep, _shift_rows(xr, k), 0.0)
                si = jnp.where(keep, _shift_rows(xi, k), 0.0)
                mr, mi = _cmul(pw[0], pw[1], sr, si)
                xr, xi = xr + mr, xi + mi
            er = jnp.where(row8 >= 1, _shift_rows(xr, 1), 0.0)
            ei = jnp.where(row8 >= 1, _shift_rows(xi, 1), 0.0)
            qr, qi = _cmul(tr_, ti_, cr, ci)
            hre[pl.ds(r0, sub), :] = er + qr
            him[pl.ds(r0, sub), :] = ei + qi
            nr, ni = _cmul(p8[0], p8[1], cr, ci)
            return xr[sub - 1:sub] + nr, xi[sub - 1:sub] + ni

        zero = jnp.zeros((1, cw), F32)
        cr, ci = lax.fori_loop(0, rows_per_seq // sub, body, (zero, zero))
        fpr[n:n + 1, :] = cr
        fpi[n:n + 1, :] = ci


def s5_scan(ere, eim, h0r, h0i, lam_re_row, lam_im_row, log_dt_row, layer, *, n_seq, rows_per_seq, cw):
    rows, width = ere.shape
    n_s = h0r.shape[0]
    col = lambda r: pl.BlockSpec((r, cw), lambda j: (0, j))
    prow = pl.BlockSpec((None, 1, cw), lambda j: (layer, 0, j))
    shp = lambda r: jax.ShapeDtypeStruct((r, width), F32)
    return pl.pallas_call(
        functools.partial(_s5_scan_kernel, n_seq=n_seq, rows_per_seq=rows_per_seq),
        out_shape=(shp(rows), shp(rows), shp(n_seq), shp(n_seq), shp(n_s), shp(n_s)),
        grid=(width // cw,),
        in_specs=[col(rows), col(rows), col(n_s), col(n_s), prow, prow, prow],
        out_specs=(col(rows), col(rows), col(n_seq), col(n_seq), col(n_s), col(n_s)),
        compiler_params=_params("parallel"),
        name="s5_scan",
    )(ere, eim, h0r, h0i, lam_re_row, lam_im_row, log_dt_row)


def _s5_y_kernel(u0, u1, u2, u3, m_ref, hre_ref, him_ref, vre_ref, vim_ref, y_ref):
    gi, gp = S5_GB * S5_GROUP, S5_GB * S5_STATE
    us = (u0, u1, u2, u3)
    t = pl.program_id(2)
    for tt in range(S5_CHUNK):
        @pl.when(t == tt)
        def _(tt=tt):
            for half in range(S5_GB_STEP):
                hs = slice(half * gp, (half + 1) * gp)
                ls = slice(half * gi, (half + 1) * gi)
                acc = (_dot(hre_ref[:, hs].astype(BF16), vre_ref[tt, half])
                       + _dot(him_ref[:, hs].astype(BF16), vim_ref[tt, half]))
                for s in range(tt + 1):
                    acc = acc + _dot(us[s][:, ls].astype(BF16), m_ref[tt - s, half])
                y_ref[:, ls] = acc


def s5_outputs(p_a4, m, hre, him, vre, vim, layer, tr):
    rows = p_a4.shape[0]
    gi, gp = S5_GB * S5_GROUP, S5_GB * S5_STATE
    n_b = S5_GROUPS // (S5_GB * S5_GB_STEP)
    mspec = pl.BlockSpec((None, S5_CHUNK, S5_GB_STEP, gi, gi), lambda i, b, t: (layer, 0, b, 0, 0))
    vspec = pl.BlockSpec((None, S5_CHUNK, S5_GB_STEP, gp, gi), lambda i, b, t: (layer, 0, b, 0, 0))
    hspec = pl.BlockSpec((tr, S5_GB_STEP * gp), lambda i, b, t: (i, b))
    return pl.pallas_call(
        _s5_y_kernel,
        out_shape=jax.ShapeDtypeStruct((rows, S5_CHUNK * BRANCH_W), F32),
        grid=(rows // tr, n_b, S5_CHUNK),
        in_specs=_u_specs(tr, 3) + [mspec, hspec, hspec, vspec, vspec],
        out_specs=pl.BlockSpec((tr, S5_GB_STEP * gi), lambda i, b, t: (i, t * n_b + b)),
        compiler_params=_params("parallel", "parallel", "arbitrary"),
        name="s5_outputs",
    )(p_a4, p_a4, p_a4, p_a4, m, hre, him, vre, vim)


def _s5_epilogue_kernel(y_ref, u_ref, g_ref, d_ref, w_ref, o_ref):
    z = _gelu_tanh(y_ref[...] + d_ref[...] * u_ref[...])
    o = z * _sigmoid(_dot(z.astype(BF16), w_ref[...]))
    o_ref[...] = (o * _silu(g_ref[...])).astype(o_ref.dtype)


def s5_epilogue(y4, p_a4, d_skip, w_glu, tr):
    rows = y4.shape[0]
    w = BRANCH_W
    return pl.pallas_call(
        _s5_epilogue_kernel,
        out_shape=jax.ShapeDtypeStruct((rows, S5_CHUNK * w), BF16),
        grid=(rows // tr, S5_CHUNK),
        in_specs=[pl.BlockSpec((tr, w), lambda i, t: (i, t)),
                  pl.BlockSpec((tr, w), lambda i, t: (i, 2 * t)),
                  pl.BlockSpec((tr, w), lambda i, t: (i, 2 * t + 1)),
                  pl.BlockSpec((1, w), lambda i, t: (0, 0)),
                  pl.BlockSpec((w, w), lambda i, t: (0, 0))],
        out_specs=pl.BlockSpec((tr, w), lambda i, t: (i, t)),
        compiler_params=_params("parallel", "parallel"),
        name="s5_epilogue",
    )(y4, p_a4, p_a4, d_skip, w_glu)


_IN_WIDTHS = (("a_x", 1024), ("a_g", 1024), ("b_q", 1024), ("b_f", 1024), ("b_i", 1024), ("b_g", 1024),
              ("c_q", 512), ("c_k", 512), ("c_v", 1024), ("c_r", 16), ("c_g", 1024),
              ("d_x", 1024), ("d_g", 1024), ("e_q", 1024), ("e_g", 1024))

TM_NORM = 512
TM_MM = 1088
TM_MERGE = 544
TN_MERGE = 512
TR_S5 = 1088
TR_S5_EPI = 544
CW_S5_SCAN = 256
CHUNK_GLA = 128
NB_GLA_SAMPLE = 8
CHUNK_LRU = 128
ROWS_MEM_PROMPT = 512
NB_MEM_SAMPLE = 8


def _in_cols(w_in_l):
    cols, off = {}, 0
    for name, width in _IN_WIDTHS:
        cols[name] = w_in_l[:, off:off + width]
        off += width
    return cols


def kernel(x_prompt, x_sample, mem_prompt, state_s5_re, state_s5_im, state_hgrn, state_gla, state_rglru, state_conv, cache_mem_k, cache_mem_v, norm_pre, norm_post, w_in, s5_lam_re, s5_lam_im, s5_log_dt, s5_b_re, s5_b_im, s5_c_re, s5_c_im, s5_d, s5_w_glu, hg_lb_logits, hg_norm, gla_w_up, gla_b_up, gla_norm, conv_w, conv_b, lru_w_r, lru_b_r, lru_w_i, lru_b_i, lru_lam, mem_norm, w_mem_k, w_mem_v, w_branch, w_merge, w_out):
    d, w = D_MODEL, BRANCH_W
    m_p, m_s = BATCH * SEQ, DEC_BATCH * DEC_SEQ
    m_all = m_p + m_s
    bf = lambda a: a.astype(BF16)
    row = lambda v: v[None, :]

    x = jnp.concatenate([x_prompt.reshape(m_p, d), x_sample.reshape(m_s, d)], axis=0)

    mem2 = mem_prompt.reshape(BATCH * N_MEM, d)
    mk_l, mv_l = [], []
    for l in range(DEPTH):
        mn = rmsnorm_rows(mem2, row(mem_norm[l]), TM_NORM)
        mk_l.append(matmul(mn, bf(w_mem_k[l]), BATCH * N_MEM, TN_MERGE))
        mv_l.append(matmul(mn, bf(w_mem_v[l]), BATCH * N_MEM, TN_MERGE))
    cache_k4 = cache_mem_k.reshape(DEPTH, DEC_BATCH, N_MEM, w)
    cache_v4 = cache_mem_v.reshape(DEPTH, DEC_BATCH, N_MEM, w)

    lb = hg_lower_bounds(hg_lb_logits)
    s5_m, s5_wre, s5_wim, s5_vre, s5_vim = s5_prepare(s5_lam_re, s5_lam_im, s5_log_dt, s5_b_re, s5_b_im,
                                                      s5_c_re, s5_c_im)
    gs = S5_GROUPS * S5_STATE
    lam_re_row = s5_lam_re.reshape(DEPTH, 1, gs)
    lam_im_row = s5_lam_im.reshape(DEPTH, 1, gs)
    log_dt_row = jnp.broadcast_to(s5_log_dt[:, :, None], (DEPTH, S5_GROUPS, S5_STATE)).reshape(DEPTH, 1, gs)
    s5_h0r = state_s5_re.reshape(DEPTH, DEC_BATCH, gs)
    s5_h0i = state_s5_im.reshape(DEPTH, DEC_BATCH, gs)
    w_up_pad = jnp.zeros((DEPTH, V7X_LANES, GLA_KEY_W), F32).at[:, :GLA_RANK].set(gla_w_up)

    hg_p = hg_s = gl_p = gl_s = None
    outs = {k: [] for k in ("p_s5r", "p_s5i", "p_lru", "p_conv", "s_s5r", "s_s5i", "s_lru", "s_conv")}
    hg_cols = [(w, 0), (w, 1), (w, 2), (w, 3)]
    gl_cols = [(GLA_KEY_W, 4), (GLA_KEY_W, 5), (w, 0), (V7X_LANES, 3 * w // V7X_LANES), (w, 1)]
    for l in range(DEPTH):
        cols = _in_cols(w_in[l])
        w_a = bf(jnp.concatenate([cols["a_x"], cols["a_g"]], axis=1))
        w_b = bf(jnp.concatenate([cols["b_q"], cols["b_f"], cols["b_i"], cols["b_g"]], axis=1))
        w_c = bf(jnp.concatenate([cols["c_v"], cols["c_g"], cols["c_q"], cols["c_k"], cols["c_r"],
                                  jnp.zeros((d, V7X_LANES - GLA_RANK), F32)], axis=1))
        w_d = bf(jnp.concatenate([cols["d_x"], cols["d_g"]], axis=1))
        w_e = bf(jnp.concatenate([cols["e_q"], cols["e_g"]], axis=1))

        h = rmsnorm_rows(x, row(norm_pre[l]), TM_NORM)
        p_a = matmul(h, w_a, TM_MM, w)
        p_b = matmul(h, w_b, TM_MM, w)
        p_c = matmul(h, w_c, TM_MM, 5 * V7X_LANES)
        p_d = matmul(h, w_d, TM_MM, w)
        p_e = matmul(h, w_e, TM_MM, w)

        p_a4 = p_a.reshape(m_all // S5_CHUNK, S5_CHUNK * 2 * w)
        ere, eim = s5_chunk_inputs(p_a4, s5_wre, s5_wim, l, TR_S5)
        hre, him, fpr, fpi, fsr, fsi = s5_scan(ere, eim, s5_h0r[l], s5_h0i[l], lam_re_row, lam_im_row, log_dt_row,
                                               l, n_seq=BATCH, rows_per_seq=SEQ // S5_CHUNK, cw=CW_S5_SCAN)
        y4 = s5_outputs(p_a4, s5_m, hre, him, s5_vre, s5_vim, l, TR_S5)
        br_a = s5_epilogue(y4, p_a4, row(s5_d[l]), bf(s5_w_glu[l]), TR_S5_EPI).reshape(m_all, w)
        outs["p_s5r"].append(fpr); outs["p_s5i"].append(fpi)
        outs["s_s5r"].append(fsr); outs["s_s5i"].append(fsi)

        hg_consts = [row(lb[l]), row(hg_norm[l])]
        hg_kw = dict(layer=l, heads=HG_HEADS, dk=HG_DK, dv=HG_DV, out_rows=m_all)
        br_b, hg_p = gla_branch("hgrn", p_b, hg_cols, hg_consts, None, None, hg_p, row_base=0, n_seq=BATCH,
                                seq_total=SEQ, chunk=CHUNK_GLA, nb=1, **hg_kw)
        br_b, hg_s = gla_branch("hgrn", p_b, hg_cols, hg_consts, state_hgrn, br_b, hg_s, row_base=m_p,
                                n_seq=DEC_BATCH, seq_total=DEC_SEQ, chunk=DEC_SEQ, nb=NB_GLA_SAMPLE, **hg_kw)

        gl_consts = [w_up_pad[l], row(gla_b_up[l]), row(gla_norm[l])]
        gl_kw = dict(layer=l, heads=GLA_HEADS, dk=GLA_DK, dv=GLA_DV, out_rows=m_all)
        br_c, gl_p = gla_branch("gla", p_c, gl_cols, gl_consts, None, None, gl_p, row_base=0, n_seq=BATCH,
                                seq_total=SEQ, chunk=CHUNK_GLA, nb=1, **gl_kw)
        br_c, gl_s = gla_branch("gla", p_c, gl_cols, gl_consts, state_gla, br_c, gl_s, row_base=m_p,
                                n_seq=DEC_BATCH, seq_total=DEC_SEQ, chunk=DEC_SEQ, nb=NB_GLA_SAMPLE, **gl_kw)

        lru_consts = [conv_w[l], row(conv_b[l]), bf(lru_w_r[l]), row(lru_b_r[l]), bf(lru_w_i[l]), row(lru_b_i[l]),
                      row(lru_lam[l])]
        br_d, hfin_p, cfin_p = lru_prompt(p_d, lru_consts, n_seq=BATCH, seq_total=SEQ, chunk=CHUNK_LRU,
                                          out_rows=m_all)
        br_d4, hfin_s, cfin_s = lru_sample(p_d.reshape(m_all // DEC_SEQ, DEC_SEQ * 2 * w),
                                           state_conv[l].reshape(DEC_BATCH, (CONV_W - 1) * w), state_rglru[l],
                                           lru_consts, br_d.reshape(m_all // DEC_SEQ, DEC_SEQ * w),
                                           row_blk=m_p // DEC_SEQ // DEC_BATCH)
        br_d = br_d4.reshape(m_all, w)
        outs["p_lru"].append(hfin_p); outs["p_conv"].append(cfin_p)
        outs["s_lru"].append(hfin_s); outs["s_conv"].append(cfin_s)

        br_e = mem_attention(p_e, mk_l[l].reshape(1, BATCH, N_MEM, w), mv_l[l].reshape(1, BATCH, N_MEM, w), None,
                             layer=0, row_base=0, n_seq=BATCH, seq_total=SEQ, rows_blk=ROWS_MEM_PROMPT, nbm=1,
                             out_rows=m_all)
        br_e = mem_attention(p_e, cache_k4, cache_v4, br_e, layer=l, row_base=m_p, n_seq=DEC_BATCH,
                             seq_total=DEC_SEQ, rows_blk=NB_MEM_SAMPLE * DEC_SEQ, nbm=NB_MEM_SAMPLE, out_rows=m_all)

        merged = merge_branches(h, [br_a, br_b, br_c, br_d, br_e], bf(w_merge[l]), bf(w_branch[l]),
                                TM_MERGE, TN_MERGE)
        x = out_proj_residual(merged, bf(w_out[l]), row(norm_post[l]), x, TM_MERGE)

    st = lambda k, shape: jnp.stack(outs[k]).reshape(shape)
    s5_p_shape = (DEPTH, BATCH, S5_GROUPS, S5_STATE)
    s5_s_shape = (DEPTH, DEC_BATCH, S5_GROUPS, S5_STATE)
    mem_shape = (DEPTH, BATCH, N_MEM, MEM_HEADS, MEM_HD)
    return (x[:m_p].reshape(BATCH, SEQ, d), x[m_p:].reshape(DEC_BATCH, DEC_SEQ, d),
            st("p_s5r", s5_p_shape), st("p_s5i", s5_p_shape), hg_p, gl_p,
            st("p_lru", (DEPTH, BATCH, w)), st("p_conv", (DEPTH, BATCH, CONV_W - 1, w)),
            jnp.stack(mk_l).reshape(mem_shape), jnp.stack(mv_l).reshape(mem_shape),
            st("s_s5r", s5_s_shape), st("s_s5i", s5_s_shape), hg_s, gl_s,
            st("s_lru", (DEPTH, DEC_BATCH, w)), st("s_conv", (DEPTH, DEC_BATCH, CONV_W - 1, w)))
```

```python
import functools
import math

import jax
import jax.numpy as jnp
from jax import lax
from jax.experimental import pallas as pl
from jax.experimental.pallas import tpu as pltpu

F32 = jnp.float32
BF16 = jnp.bfloat16
HIGHEST = lax.Precision.HIGHEST

V7X_LANES = 128
V7X_SUBLANES = 8
V7X_VMEM_LIMIT_BYTES = 56 * 1024 * 1024

EPS = 1e-6
D_MODEL = 2048
DEPTH = 4
BATCH = 4
SEQ = 2048
DEC_BATCH = 128
DEC_SEQ = 4
BRANCH_W = 1024
N_BRANCH = 5
S5_GROUP = 16
S5_GROUPS = 64
S5_STATE = 64
S5_CHUNK = 4
S5_GB = 8
HG_HEADS = 8
HG_DK = 128
HG_DV = 128
GLA_HEADS = 4
GLA_KEY_W = 512
GLA_DK = 128
GLA_DV = 256
GLA_RANK = 16
GLA_GATE_TEMP = 16.0
LRU_BLOCKS = 8
LRU_BW = 128
CONV_W = 4
LRU_C = 8.0
N_MEM = 256
MEM_HEADS = 4
MEM_HD = 256


def _params(*sem):
    return pltpu.CompilerParams(dimension_semantics=sem, vmem_limit_bytes=V7X_VMEM_LIMIT_BYTES)


def _dot(a, b):
    return jnp.dot(a, b, preferred_element_type=F32)


def _dot_nt(a, b):
    return lax.dot_general(a, b, (((1,), (1,)), ((), ())), preferred_element_type=F32)


def _dot_tn(a, b):
    return lax.dot_general(a, b, (((0,), (0,)), ((), ())), preferred_element_type=F32)


def _sigmoid(x):
    return 1.0 / (1.0 + jnp.exp(-x))


def _silu(x):
    return x * _sigmoid(x)


def _log_sigmoid(x):
    return jnp.minimum(x, 0.0) - jnp.log1p(jnp.exp(-jnp.abs(x)))


def _softplus(x):
    return jnp.maximum(x, 0.0) + jnp.log1p(jnp.exp(-jnp.abs(x)))


def _gelu_tanh(x):
    return 0.5 * x * (1.0 + jnp.tanh(math.sqrt(2.0 / math.pi) * (x + 0.044715 * (x * x * x))))


def _row_iota(shape):
    return lax.broadcasted_iota(jnp.int32, shape, 0)


def _col_iota(shape):
    return lax.broadcasted_iota(jnp.int32, shape, 1)


def _shift_rows(x, k):
    return pltpu.roll(x, k, 0)


def _shift_rows_up(x, k):
    return pltpu.roll(x, x.shape[0] - k, 0)


def _rmsnorm_kernel(x_ref, g_ref, o_ref):
    x = x_ref[...]
    y = x * lax.rsqrt(jnp.mean(x * x, axis=-1, keepdims=True) + EPS)
    o_ref[...] = (y * g_ref[...]).astype(o_ref.dtype)


def rmsnorm_rows(x, g, tm):
    m, d = x.shape
    return pl.pallas_call(
        _rmsnorm_kernel,
        out_shape=jax.ShapeDtypeStruct((m, d), BF16),
        grid=(m // tm,),
        in_specs=[pl.BlockSpec((tm, d), lambda i: (i, 0)), pl.BlockSpec((1, d), lambda i: (0, 0))],
        out_specs=pl.BlockSpec((tm, d), lambda i: (i, 0)),
        compiler_params=_params("parallel"),
        name="rmsnorm_rows",
    )(x, g)


def _matmul_kernel(a_ref, w_ref, o_ref):
    o_ref[...] = _dot(a_ref[...], w_ref[...]).astype(o_ref.dtype)


def matmul(a, w, tm, tn, out_dtype=F32):
    m, k = a.shape
    n = w.shape[1]
    return pl.pallas_call(
        _matmul_kernel,
        out_shape=jax.ShapeDtypeStruct((m, n), out_dtype),
        grid=(m // tm, n // tn),
        in_specs=[pl.BlockSpec((tm, k), lambda i, j: (i, 0)), pl.BlockSpec((k, tn), lambda i, j: (0, j))],
        out_specs=pl.BlockSpec((tm, tn), lambda i, j: (i, j)),
        compiler_params=_params("parallel", "parallel"),
        name="matmul",
    )(a, w)


def _merge_kernel(h_ref, b_ref, wm_ref, wb_ref, o_ref, acc_ref):
    c = pl.program_id(1)
    j = pl.program_id(2)
    nj, _, tn = acc_ref.shape
    contrib = _sigmoid(_dot(h_ref[...], wm_ref[...])) * _dot(b_ref[...], wb_ref[...])

    @pl.when(c == 0)
    def _():
        acc_ref[j] = contrib

    @pl.when(jnp.logical_and(c > 0, c < N_BRANCH - 1))
    def _():
        acc_ref[j] += contrib

    for jj in range(nj):
        @pl.when(jnp.logical_and(c == N_BRANCH - 1, j == jj))
        def _(jj=jj):
            o_ref[:, jj * tn:(jj + 1) * tn] = (acc_ref[jj] + contrib).astype(o_ref.dtype)


def merge_branches(h, branches, w_merge, w_branch, tm, tn):
    m, d = h.shape
    w = branches.shape[2]
    nj = d // tn
    return pl.pallas_call(
        _merge_kernel,
        out_shape=jax.ShapeDtypeStruct((m, d), BF16),
        grid=(m // tm, N_BRANCH, nj),
        in_specs=[pl.BlockSpec((tm, d), lambda i, c, j: (i, 0)),
                  pl.BlockSpec((None, tm, w), lambda i, c, j: (c, i, 0)),
                  pl.BlockSpec((d, tn), lambda i, c, j: (0, c * nj + j)),
                  pl.BlockSpec((None, w, tn), lambda i, c, j: (c, 0, j))],
        out_specs=pl.BlockSpec((tm, d), lambda i, c, j: (i, 0)),
        scratch_shapes=[pltpu.VMEM((nj, tm, tn), F32)],
        compiler_params=_params("parallel", "arbitrary", "arbitrary"),
        name="merge_branches",
    )(h, branches, w_merge, w_branch)


def _out_kernel(m_ref, w_ref, g_ref, x_ref, o_ref):
    y = _dot(m_ref[...], w_ref[...])
    y = y * lax.rsqrt(jnp.mean(y * y, axis=-1, keepdims=True) + EPS)
    o_ref[...] = x_ref[...] + y * g_ref[...]


def out_proj_residual(merged, w_out, g, x, tm):
    m, d = x.shape
    return pl.pallas_call(
        _out_kernel,
        out_shape=jax.ShapeDtypeStruct((m, d), F32),
        grid=(m // tm,),
        in_specs=[pl.BlockSpec((tm, d), lambda i: (i, 0)), pl.BlockSpec((d, d), lambda i: (0, 0)),
                  pl.BlockSpec((1, d), lambda i: (0, 0)), pl.BlockSpec((tm, d), lambda i: (i, 0))],
        out_specs=pl.BlockSpec((tm, d), lambda i: (i, 0)),
        compiler_params=_params("parallel"),
        name="out_proj_residual",
    )(merged, w_out, g, x)


def _hg_lb_kernel(logit_ref, lb_ref):
    x = logit_ref[...]
    e = jnp.exp(x - jnp.max(x, axis=0, keepdims=True))
    sm = e / jnp.sum(e, axis=0, keepdims=True)
    acc = jnp.zeros_like(sm[0:1])
    lb_ref[0:1, :] = acc
    for l in range(1, DEPTH):
        acc = acc + sm[l:l + 1]
        lb_ref[l:l + 1, :] = acc


def hg_lower_bounds(logits):
    return pl.pallas_call(
        _hg_lb_kernel,
        out_shape=jax.ShapeDtypeStruct(logits.shape, F32),
        name="hg_lower_bounds",
    )(logits)


def _gla_kernel(*refs, mode, seq_len, nb, heads, dk, dv, has_init):
    L = seq_len
    R = nb * L
    C = heads * dk
    if mode == "hgrn":
        q_ref, f_ref, v_ref, g_ref, lb_ref, gn_ref = refs[:6]
        rest = refs[6:]
    else:
        q_ref, k_ref, v_ref, r_ref, g_ref, wup_ref, bup_ref, gn_ref = refs[:8]
        rest = refs[8:]
    if has_init:
        s0_ref, rest = rest[0], rest[1:]
    o_ref, sout_ref, s_scr, att_scr, qe_scr, kd_scr, b_scr = rest

    c = pl.program_id(1)

    @pl.when(c == 0)
    def _():
        if has_init:
            s_scr[...] = s0_ref[...]
        else:
            s_scr[...] = jnp.zeros_like(s_scr)

    t = _row_iota((R, C)) % L
    if mode == "hgrn":
        lb = lb_ref[...]
        fz = f_ref[...]
        la = jnp.log(lb)
        lc = jnp.log1p(-lb) + _log_sigmoid(fz)
        g = jnp.maximum(la, lc) + jnp.log1p(jnp.exp(-jnp.abs(la - lc)))
        k = (1.0 - lb) * _sigmoid(-fz)
        q = q_ref[...]
    else:
        pre = jnp.dot(r_ref[...], wup_ref[...], preferred_element_type=F32, precision=HIGHEST)
        g = _log_sigmoid(pre + bup_ref[...]) * (1.0 / GLA_GATE_TEMP)
        k = k_ref[...]
        q = q_ref[...] * (dk ** -0.5)

    b = g
    rb = jnp.where(t + 1 < L, _shift_rows_up(g, 1), 0.0)
    s = 1
    while s < L:
        b = b + jnp.where(t >= s, _shift_rows(b, s), 0.0)
        rb = rb + jnp.where(t + s < L, _shift_rows_up(rb, s), 0.0)
        s *= 2
    b_scr[...] = b
    qe_scr[...] = (q * jnp.exp(b)).astype(BF16)
    kd_scr[...] = (k * jnp.exp(rb)).astype(BF16)

    rr = _row_iota((R, R))
    cc = _col_iota((R, R))

    def add_level(level, qt, kt, mask):
        qt = qt.astype(BF16)
        kt = kt.astype(BF16)
        for h in range(heads):
            hs = slice(h * dk, (h + 1) * dk)
            p = jnp.where(mask, _dot_nt(qt[:, hs], kt[:, hs]), 0.0)
            if level == 0:
                att_scr[h] = p
            else:
                att_scr[h] += p

    add_level(0, q, k, rr == cc)
    level = 1
    m = 1
    while m < L:
        pos = t % (2 * m)
        upper = pos >= m
        if m == 1:
            d = jnp.where(upper, g, 0.0)
        elif m == 2:
            d = jnp.where(pos == 2, g,
                          jnp.where(pos == 3, g + _shift_rows(g, 1),
                                    jnp.where(pos == 0, _shift_rows_up(g, 1), 0.0)))
        else:
            b3 = b.reshape(R // (2 * m), 2 * m, C)
            d3 = b3 - b3[:, m - 1:m, :]
            d = d3.reshape(R, C)
            d = jnp.where(upper, d, -d)
        e = jnp.exp(d)
        add_level(level, jnp.where(upper, q * e, 0.0), jnp.where(upper, 0.0, k * e),
                  (rr // (2 * m)) == (cc // (2 * m)))
        level += 1
        m *= 2

    gn = gn_ref[...]
    seq_o = _row_iota((R, dv)) // L
    seq_k = _row_iota((R, dk)) // L
    pad_rows = (-R) % V7X_LANES
    for h in range(heads):
        hs = slice(h * dk, (h + 1) * dk)
        vs = slice(h * dv, (h + 1) * dv)
        vh = v_ref[:, vs].astype(BF16)
        vh_p = jnp.concatenate([vh, jnp.zeros((pad_rows, dv), BF16)], axis=0) if pad_rows else vh
        o = _dot(att_scr[h].astype(BF16), vh)
        qe = qe_scr[:, hs]
        kd = kd_scr[:, hs]
        for j in range(nb):
            st = s_scr[j, h]
            inter = _dot(qe, st.astype(BF16))
            kd_j = kd
            if nb > 1:
                inter = jnp.where(seq_o == j, inter, 0.0)
                kd_j = jnp.where(seq_k == j, kd, jnp.zeros_like(kd))
            o = o + inter
            if pad_rows:
                kd_j = jnp.concatenate([kd_j, jnp.zeros((pad_rows, dk), BF16)], axis=0)
            upd = _dot_tn(kd_j, vh_p)
            dl = jnp.exp(b_scr[j * L + L - 1:j * L + L, hs])
            colb = jnp.broadcast_to(dl, (dk, dk)).T
            if dv != dk:
                colb = jnp.concatenate([colb] * (dv // dk), axis=1)
            s_scr[j, h] = colb * st + upd
        on = o * lax.rsqrt(jnp.mean(o * o, axis=-1, keepdims=True) + EPS) * gn
        o_ref[:, vs] = (on * _silu(g_ref[:, vs])).astype(o_ref.dtype)

    @pl.when(c == pl.num_programs(1) - 1)
    def _():
        sout_ref[...] = s_scr[...]


def gla_branch(mode, p, col_blocks, consts, s0, out_prev, sout_prev, *, layer, slab, row_base, n_seq, seq_total,
               chunk, nb, heads, dk, dv):
    L = chunk
    R = nb * L
    n_chunks = seq_total // L
    base_blk = row_base // R
    width = heads * dv

    def rows(i, c):
        return base_blk + i * n_chunks + c

    in_specs = [pl.BlockSpec((R, w), functools.partial(lambda i, c, bi: (rows(i, c), bi), bi=bi))
                for (w, bi) in col_blocks]
    args = [p] * len(col_blocks)
    for a in consts:
        in_specs.append(pl.BlockSpec(a.shape, lambda i, c: (0, 0)))
        args.append(a)
    has_init = s0 is not None
    state_spec = pl.BlockSpec((None, nb, heads, dk, dv), lambda i, c: (layer, i, 0, 0, 0))
    if has_init:
        in_specs.append(state_spec)
        args.append(s0)
    n_real = len(args)
    aliases = {}
    for out_idx, prev in enumerate((out_prev, sout_prev)):
        if prev is not None:
            aliases[len(args)] = out_idx
            in_specs.append(pl.BlockSpec(memory_space=pl.ANY))
            args.append(prev)
    n_in = len(args)

    def kern(*refs):
        _gla_kernel(*refs[:n_real], *refs[n_in:], mode=mode, seq_len=L, nb=nb, heads=heads, dk=dk, dv=dv,
                    has_init=has_init)

    return pl.pallas_call(
        kern,
        out_shape=(jax.ShapeDtypeStruct(out_prev.shape, BF16),
                   jax.ShapeDtypeStruct((DEPTH, n_seq, heads, dk, dv), F32)),
        grid=(n_seq // nb, n_chunks),
        in_specs=in_specs,
        out_specs=(pl.BlockSpec((None, R, width), lambda i, c: (slab, rows(i, c), 0)), state_spec),
        scratch_shapes=[pltpu.VMEM((nb, heads, dk, dv), F32), pltpu.VMEM((heads, R, R), F32),
                        pltpu.VMEM((R, heads * dk), BF16), pltpu.VMEM((R, heads * dk), BF16),
                        pltpu.VMEM((R, heads * dk), F32)],
        input_output_aliases=aliases,
        compiler_params=_params("parallel", "arbitrary"),
        name=mode + "_branch",
    )(*args)


def _lru_gates(xc, wr_ref, br_ref, wi_ref, bi_ref, lam_ref):
    xcb = xc.astype(BF16)
    r_parts, i_parts = [], []
    for blk in range(xc.shape[1] // LRU_BW):
        bs = slice(blk * LRU_BW, (blk + 1) * LRU_BW)
        r_parts.append(_dot(xcb[:, bs], wr_ref[blk]))
        i_parts.append(_dot(xcb[:, bs], wi_ref[blk]))
    r = _sigmoid(jnp.concatenate(r_parts, axis=1) + br_ref[...])
    ig = _sigmoid(jnp.concatenate(i_parts, axis=1) + bi_ref[...])
    log_a = (-LRU_C) * r * _softplus(-lam_ref[...])
    a = jnp.exp(log_a)
    u = jnp.sqrt(1.0 - jnp.exp(2.0 * log_a)) * (ig * xc)
    return a, u


def _lru_prompt_kernel(x_ref, g_ref, cw_ref, cb_ref, wr_ref, br_ref, wi_ref, bi_ref, lam_ref,
                       o_ref, hfin_ref, cfin_ref, tail_scr, hc_scr, a_scr, u_scr):
    tc, w = x_ref.shape
    sub = V7X_SUBLANES
    c = pl.program_id(1)

    @pl.when(c == 0)
    def _():
        tail_scr[...] = jnp.zeros_like(tail_scr)
        hc_scr[...] = jnp.zeros_like(hc_scr)

    x = x_ref[...]
    tail = tail_scr[...]
    row8 = _row_iota((sub, w))
    cw = cw_ref[...]
    xc = cb_ref[...] + cw[CONV_W - 1:CONV_W] * x
    for k in range(1, CONV_W):
        xs = _shift_rows(x, k)
        head = jnp.where(row8 < k, _shift_rows(tail, k), xs[0:sub])
        xs = jnp.concatenate([head, xs[sub:]], axis=0)
        xc = xc + cw[CONV_W - 1 - k:CONV_W - k] * xs
    last8 = x[tc - sub:tc]
    tail_scr[...] = last8

    a, u = _lru_gates(xc, wr_ref, br_ref, wi_ref, bi_ref, lam_ref)
    a_scr[...] = a
    u_scr[...] = u

    def body(i, carry):
        r0 = pl.multiple_of(i * sub, sub)
        at = a_scr[pl.ds(r0, sub), :]
        ut = u_scr[pl.ds(r0, sub), :]
        for k in (1, 2, 4):
            keep = row8 >= k
            ut = ut + at * jnp.where(keep, _shift_rows(ut, k), 0.0)
            at = at * jnp.where(keep, _shift_rows(at, k), 1.0)
        ht = ut + at * carry
        u_scr[pl.ds(r0, sub), :] = ht
        return jnp.broadcast_to(ht[sub - 1:sub], (sub, w))

    carry = lax.fori_loop(0, tc // sub, body, hc_scr[...])
    hc_scr[...] = carry
    o_ref[...] = (u_scr[...] * _silu(g_ref[...])).astype(o_ref.dtype)

    @pl.when(c == pl.num_programs(1) - 1)
    def _():
        hfin_ref[...] = carry[0:1]
        cfin_ref[...] = _shift_rows(last8, CONV_W - 1)[0:CONV_W - 1]


def lru_prompt(p_d, consts, out_prev, *, slab, n_seq, seq_total, chunk):
    w = BRANCH_W
    n_chunks = seq_total // chunk
    const_specs = [pl.BlockSpec(a.shape, functools.partial(lambda i, c, nd: (0,) * nd, nd=a.ndim)) for a in consts]

    def kern(*refs):
        n_in = 2 + len(consts)
        _lru_prompt_kernel(*refs[:n_in], *refs[n_in + 1:])

    return pl.pallas_call(
        kern,
        out_shape=(jax.ShapeDtypeStruct(out_prev.shape, BF16),
                   jax.ShapeDtypeStruct((n_seq, 1, w), F32),
                   jax.ShapeDtypeStruct((n_seq, CONV_W - 1, w), F32)),
        grid=(n_seq, n_chunks),
        in_specs=[pl.BlockSpec((chunk, w), lambda i, c: (i * n_chunks + c, 0)),
                  pl.BlockSpec((chunk, w), lambda i, c: (i * n_chunks + c, 1))] + const_specs
        + [pl.BlockSpec(memory_space=pl.ANY)],
        out_specs=(pl.BlockSpec((None, chunk, w), lambda i, c: (slab, i * n_chunks + c, 0)),
                   pl.BlockSpec((None, 1, w), lambda i, c: (i, 0, 0)),
                   pl.BlockSpec((None, CONV_W - 1, w), lambda i, c: (i, 0, 0))),
        scratch_shapes=[pltpu.VMEM((V7X_SUBLANES, w), F32), pltpu.VMEM((V7X_SUBLANES, w), F32),
                        pltpu.VMEM((chunk, w), F32), pltpu.VMEM((chunk, w), F32)],
        input_output_aliases={2 + len(consts): 0},
        compiler_params=_params("parallel", "arbitrary"),
        name="lru_prompt",
    )(p_d, p_d, *consts, out_prev)


def _lru_sample_kernel(x_ref, g_ref, buf_ref, h0_ref, cw_ref, cb_ref, wr_ref, br_ref, wi_ref, bi_ref, lam_ref,
                       prev_ref, o_ref, hfin_ref, cfin_ref, o_scr):
    del prev_ref
    steps = DEC_SEQ
    n = x_ref.shape[0] // steps
    step_rows = lambda t: pl.ds(t, n, stride=steps)
    xcat = [buf_ref[j] for j in range(CONV_W - 1)] + [x_ref[step_rows(t), :] for t in range(steps)]
    cw = cw_ref[...]
    xcs = []
    for t in range(steps):
        xc = cb_ref[...]
        for j in range(CONV_W):
            xc = xc + cw[j:j + 1] * xcat[t + j]
        xcs.append(xc)
    a, u = _lru_gates(jnp.concatenate(xcs, axis=0), wr_ref, br_ref, wi_ref, bi_ref, lam_ref)
    h = h0_ref[...]
    for t in range(steps):
        h = a[t * n:(t + 1) * n] * h + u[t * n:(t + 1) * n]
        o_scr[step_rows(t), :] = h * _silu(g_ref[step_rows(t), :])
    o_ref[...] = o_scr[...].astype(o_ref.dtype)
    hfin_ref[...] = h
    for j in range(CONV_W - 1):
        cfin_ref[j] = xcat[steps + j]


def lru_sample(p_d, conv_buf, h0, consts, out_prev, *, slab, row_base):
    w = BRANCH_W
    bw = LRU_BW
    n = h0.shape[0]
    rows = n * DEC_SEQ
    blk = row_base // rows
    nbk = w // bw
    conv_w_, conv_b_, w_r, b_r, w_i, b_i, lam = consts
    lane = lambda r: pl.BlockSpec((r, bw), lambda i: (0, i))
    wblk = pl.BlockSpec((1, bw, bw), lambda i: (i, 0, 0))
    tail = pl.BlockSpec((CONV_W - 1, n, bw), lambda i: (0, 0, i))
    return pl.pallas_call(
        _lru_sample_kernel,
        out_shape=(jax.ShapeDtypeStruct(out_prev.shape, BF16),
                   jax.ShapeDtypeStruct((n, w), F32),
                   jax.ShapeDtypeStruct((CONV_W - 1, n, w), F32)),
        grid=(nbk,),
        in_specs=[pl.BlockSpec((rows, bw), lambda i: (blk, i)), pl.BlockSpec((rows, bw), lambda i: (blk, nbk + i)),
                  tail, lane(n), lane(CONV_W), lane(1), wblk, lane(1), wblk, lane(1), lane(1),
                  pl.BlockSpec(memory_space=pl.ANY)],
        out_specs=(pl.BlockSpec((None, rows, bw), lambda i: (slab, blk, i)), lane(n), tail),
        scratch_shapes=[pltpu.VMEM((rows, bw), F32)],
        input_output_aliases={11: 0},
        compiler_params=_params("parallel"),
        name="lru_sample",
    )(p_d, p_d, conv_buf, h0, conv_w_, conv_b_, w_r, b_r, w_i, b_i, lam, out_prev)


def _mem_attn_kernel(*refs, nbm, seq_rows, aliased):
    if aliased:
        q_ref, g_ref, k_ref, v_ref, _, o_ref = refs
    else:
        q_ref, g_ref, k_ref, v_ref, o_ref = refs
    rows = q_ref.shape[0]
    q = (q_ref[...] * (MEM_HD ** -0.5)).astype(BF16)
    seq = _row_iota((rows, MEM_HD)) // seq_rows
    for h in range(MEM_HEADS):
        hs = slice(h * MEM_HD, (h + 1) * MEM_HD)
        acc = None
        for j in range(nbm):
            kh = k_ref[j, :, h, :].astype(BF16)
            vh = v_ref[j, :, h, :].astype(BF16)
            s = _dot_nt(q[:, hs], kh)
            e = jnp.exp(s - jnp.max(s, axis=-1, keepdims=True))
            pr = e / jnp.sum(e, axis=-1, keepdims=True)
            oh = _dot(pr.astype(BF16), vh)
            acc = oh if acc is None else jnp.where(seq == j, oh, acc)
        o_ref[:, hs] = (acc * _silu(g_ref[:, hs])).astype(o_ref.dtype)


def mem_attention(p_e, mem_k, mem_v, out_prev, *, layer, slab, row_base, n_seq, seq_total, rows_blk, nbm):
    w = BRANCH_W
    base_blk = row_base // rows_blk
    if nbm == 1:
        t_blocks = seq_total // rows_blk
        grid = (n_seq, t_blocks)
        rmap = lambda i, t: base_blk + i * t_blocks + t
    else:
        grid = (n_seq // nbm, 1)
        rmap = lambda i, t: base_blk + i
    mem_spec = pl.BlockSpec((None, nbm, N_MEM, MEM_HEADS, MEM_HD), lambda i, t: (layer, i, 0, 0, 0))
    in_specs = [pl.BlockSpec((rows_blk, w), lambda i, t: (rmap(i, t), 0)),
                pl.BlockSpec((rows_blk, w), lambda i, t: (rmap(i, t), 1)),
                mem_spec, mem_spec, pl.BlockSpec(memory_space=pl.ANY)]
    args = [p_e, p_e, mem_k, mem_v, out_prev]
    aliases = {4: 0}
    return pl.pallas_call(
        functools.partial(_mem_attn_kernel, nbm=nbm, seq_rows=seq_total, aliased=True),
        out_shape=jax.ShapeDtypeStruct(out_prev.shape, BF16),
        grid=grid,
        in_specs=in_specs,
        out_specs=pl.BlockSpec((None, rows_blk, w), lambda i, t: (slab, rmap(i, t), 0)),
        input_output_aliases=aliases,
        compiler_params=_params("parallel", "arbitrary"),
        name="mem_attention",
    )(*args)


def _cmul(ar, ai, br, bi):
    return ar * br - ai * bi, ar * bi + ai * br


def _s5_abar(lr, li, log_dt):
    dt = jnp.exp(log_dt)
    mag = jnp.exp(lr * dt)
    return mag * jnp.cos(li * dt), mag * jnp.sin(li * dt)


def _s5_prep_kernel(lra, lia, dta, bra, bia, cfr, cfi, lrc, lic, dtc, ctr, cti,
                    m_ref, wre_ref, wim_ref, vre_ref, vim_ref):
    gi = S5_GB * S5_GROUP
    gp = S5_GB * S5_STATE
    lr, li = lra[...], lia[...]
    ar, ai = _s5_abar(lr, li, dta[...])
    den = lr * lr + li * li
    zr = ((ar - 1.0) * lr + ai * li) / den
    zi = (ai * lr - (ar - 1.0) * li) / den
    bbr, bbi = _cmul(zr, zi, bra[...], bia[...])
    wmask = (_row_iota((gi, gp)) // S5_GROUP) == (_col_iota((gi, gp)) // S5_STATE)
    mmask = (_row_iota((gi, gi)) // S5_GROUP) == (_col_iota((gi, gi)) // S5_GROUP)
    cr, ci = cfr[...], cfi[...]
    pr, pi = jnp.ones_like(ar), jnp.zeros_like(ar)
    for tau in range(S5_CHUNK):
        lrr, lii = _cmul(pr, pi, bbr, bbi)
        wre_ref[S5_CHUNK - 1 - tau] = jnp.where(wmask, lrr, 0.0).astype(BF16)
        wim_ref[S5_CHUNK - 1 - tau] = jnp.where(wmask, lii, 0.0).astype(BF16)
        x = (lax.dot_general(lrr[:, :S5_STATE], cr, (((1,), (1,)), ((), ())), precision=HIGHEST,
                             preferred_element_type=F32)
             - lax.dot_general(lii[:, :S5_STATE], ci, (((1,), (1,)), ((), ())), precision=HIGHEST,
                               preferred_element_type=F32))
        m_ref[tau] = jnp.where(mmask, x, 0.0).astype(BF16)
        pr, pi = _cmul(pr, pi, ar, ai)

    arc, aic = _s5_abar(lrc[...], lic[...], dtc[...])
    vmask = (_row_iota((gp, gi)) // S5_STATE) == (_col_iota((gp, gi)) // S5_GROUP)
    pr, pi = arc, aic
    for t in range(S5_CHUNK):
        vr, vi = _cmul(ctr[...], cti[...], pr, pi)
        vre_ref[t] = jnp.where(vmask, vr, 0.0).astype(BF16)
        vim_ref[t] = jnp.where(vmask, -vi, 0.0).astype(BF16)
        pr, pi = _cmul(pr, pi, arc, aic)


def s5_prepare(lam_re, lam_im, log_dt, b_re, b_im, c_re, c_im):
    dp, g, p = lam_re.shape
    i = S5_GROUP
    gi, gp = S5_GB * i, S5_GB * p
    nb = g // S5_GB

    def a_layout(x_gp):
        return jnp.broadcast_to(x_gp[:, :, None, None, :], (dp, g, i, S5_GB, p)).reshape(dp, g * i, gp)

    def c_layout(x_gp):
        return jnp.broadcast_to(x_gp[:, :, :, None], (dp, g, p, gi)).reshape(dp, g * p, gi)

    ldt = jnp.broadcast_to(log_dt[:, :, None], (dp, g, p))
    bt = lambda b: jnp.broadcast_to(jnp.transpose(b, (0, 1, 3, 2))[:, :, :, None, :],
                                    (dp, g, i, S5_GB, p)).reshape(dp, g * i, gp)
    ct = lambda c: jnp.broadcast_to(jnp.transpose(c, (0, 1, 3, 2))[:, :, :, None, :],
                                    (dp, g, p, S5_GB, i)).reshape(dp, g * p, gi)
    args = [a_layout(lam_re), a_layout(lam_im), a_layout(ldt), bt(b_re), bt(b_im),
            c_re.reshape(dp, g * i, p), c_im.reshape(dp, g * i, p),
            c_layout(lam_re), c_layout(lam_im), c_layout(ldt), ct(c_re), ct(c_im)]
    spec_a = pl.BlockSpec((None, gi, gp), lambda l, b: (l, b, 0))
    spec_f = pl.BlockSpec((None, gi, p), lambda l, b: (l, b, 0))
    spec_c = pl.BlockSpec((None, gp, gi), lambda l, b: (l, b, 0))
    out5 = lambda r, c: pl.BlockSpec((None, S5_CHUNK, None, r, c), lambda l, b: (l, 0, b, 0, 0))
    shp = lambda r, c: jax.ShapeDtypeStruct((dp, S5_CHUNK, nb, r, c), BF16)
    return pl.pallas_call(
        _s5_prep_kernel,
        out_shape=(shp(gi, gi), shp(gi, gp), shp(gi, gp), shp(gp, gi), shp(gp, gi)),
        grid=(dp, nb),
        in_specs=[spec_a] * 5 + [spec_f] * 2 + [spec_c] * 5,
        out_specs=(out5(gi, gi), out5(gi, gp), out5(gi, gp), out5(gp, gi), out5(gp, gi)),
        compiler_params=_params("parallel", "parallel"),
        name="s5_prepare",
    )(*args)


S5_GB_STEP = 1


def _s5_steps(u_ref):
    n = u_ref.shape[0] // S5_CHUNK
    return [u_ref[pl.ds(s, n, stride=S5_CHUNK), :].astype(BF16) for s in range(S5_CHUNK)]


def _s5_e_kernel(u_ref, wre_ref, wim_ref, ere_ref, eim_ref):
    gi, gp = S5_GB * S5_GROUP, S5_GB * S5_STATE
    us = _s5_steps(u_ref)
    for half in range(S5_GB_STEP):
        accr = acci = None
        for s, u in enumerate(us):
            ub = u[:, half * gi:(half + 1) * gi]
            r = _dot(ub, wre_ref[s, half])
            i = _dot(ub, wim_ref[s, half])
            accr = r if accr is None else accr + r
            acci = i if acci is None else acci + i
        ere_ref[:, half * gp:(half + 1) * gp] = accr
        eim_ref[:, half * gp:(half + 1) * gp] = acci


def s5_chunk_inputs(p_a, wre, wim, layer, tr):
    rows = p_a.shape[0] // S5_CHUNK
    gi, gp = S5_GB * S5_GROUP, S5_GB * S5_STATE
    n_b = S5_GROUPS // (S5_GB * S5_GB_STEP)
    wspec = pl.BlockSpec((None, S5_CHUNK, S5_GB_STEP, gi, gp), lambda i, b: (layer, 0, b, 0, 0))
    ospec = pl.BlockSpec((tr, S5_GB_STEP * gp), lambda i, b: (i, b))
    oshape = jax.ShapeDtypeStruct((rows, S5_GROUPS * S5_STATE), F32)
    return pl.pallas_call(
        _s5_e_kernel,
        out_shape=(oshape, oshape),
        grid=(rows // tr, n_b),
        in_specs=[pl.BlockSpec((S5_CHUNK * tr, S5_GB_STEP * gi), lambda i, b: (i, b)), wspec, wspec],
        out_specs=(ospec, ospec),
        compiler_params=_params("parallel", "parallel"),
        name="s5_chunk_inputs",
    )(p_a, wre, wim)


def _s5_scan_kernel(ere, eim, h0r, h0i, lr_ref, li_ref, ldt_ref, hre, him, fpr, fpi, fsr, fsi,
                    *, n_seq, rows_per_seq):
    sub = V7X_SUBLANES
    cw = ere.shape[1]
    ar, ai = _s5_abar(lr_ref[...], li_ref[...], ldt_ref[...])
    a2 = _cmul(ar, ai, ar, ai)
    p1 = _cmul(*a2, *a2)
    p2 = _cmul(*p1, *p1)
    p4 = _cmul(*p2, *p2)
    p8 = _cmul(*p4, *p4)
    row8 = _row_iota((sub, cw))
    tr_, ti_ = jnp.ones((sub, cw), F32), jnp.zeros((sub, cw), F32)
    for bit, pw in ((1, p1), (2, p2), (4, p4)):
        nr, ni = _cmul(tr_, ti_, *pw)
        sel = (row8 & bit) != 0
        tr_, ti_ = jnp.where(sel, nr, tr_), jnp.where(sel, ni, ti_)

    base = n_seq * rows_per_seq
    n_s = h0r.shape[0]
    h0r_v, h0i_v = h0r[...], h0i[...]
    hre[base:base + n_s, :] = h0r_v
    him[base:base + n_s, :] = h0i_v
    dr, di = _cmul(p1[0], p1[1], h0r_v, h0i_v)
    fsr[...] = dr + ere[base:base + n_s, :]
    fsi[...] = di + eim[base:base + n_s, :]

    for n in range(n_seq):
        def body(j, carry, n=n):
            cr, ci = carry
            r0 = pl.multiple_of(n * rows_per_seq + j * sub, sub)
            xr = ere[pl.ds(r0, sub), :]
            xi = eim[pl.ds(r0, sub), :]
            for k, pw in ((1, p1), (2, p2), (4, p4)):
                keep = row8 >= k
                sr = jnp.where(keep, _shift_rows(xr, k), 0.0)
                si = jnp.where(keep, _shift_rows(xi, k), 0.0)
                mr, mi = _cmul(pw[0], pw[1], sr, si)
                xr, xi = xr + mr, xi + mi
            er = jnp.where(row8 >= 1, _shift_rows(xr, 1), 0.0)
            ei = jnp.where(row8 >= 1, _shift_rows(xi, 1), 0.0)
            qr, qi = _cmul(tr_, ti_, cr, ci)
            hre[pl.ds(r0, sub), :] = er + qr
            him[pl.ds(r0, sub), :] = ei + qi
            nr, ni = _cmul(p8[0], p8[1], cr, ci)
            return xr[sub - 1:sub] + nr, xi[sub - 1:sub] + ni

        zero = jnp.zeros((1, cw), F32)
        cr, ci = lax.fori_loop(0, rows_per_seq // sub, body, (zero, zero))
        fpr[n:n + 1, :] = cr
        fpi[n:n + 1, :] = ci


def s5_scan(ere, eim, h0r, h0i, lam_re_row, lam_im_row, log_dt_row, layer, *, n_seq, rows_per_seq, cw):
    rows, width = ere.shape
    n_s = h0r.shape[0]
    col = lambda r: pl.BlockSpec((r, cw), lambda j: (0, j))
    prow = pl.BlockSpec((None, 1, cw), lambda j: (layer, 0, j))
    shp = lambda r: jax.ShapeDtypeStruct((r, width), F32)
    return pl.pallas_call(
        functools.partial(_s5_scan_kernel, n_seq=n_seq, rows_per_seq=rows_per_seq),
        out_shape=(shp(rows), shp(rows), shp(n_seq), shp(n_seq), shp(n_s), shp(n_s)),
        grid=(width // cw,),
        in_specs=[col(rows), col(rows), col(n_s), col(n_s), prow, prow, prow],
        out_specs=(col(rows), col(rows), col(n_seq), col(n_seq), col(n_s), col(n_s)),
        compiler_params=_params("parallel"),
        name="s5_scan",
    )(ere, eim, h0r, h0i, lam_re_row, lam_im_row, log_dt_row)


def _s5_y_kernel(u_ref, m_ref, hre_ref, him_ref, vre_ref, vim_ref, y_ref):
    gi, gp = S5_GB * S5_GROUP, S5_GB * S5_STATE
    us = _s5_steps(u_ref)
    n = hre_ref.shape[0]
    for half in range(S5_GB_STEP):
        hs = slice(half * gp, (half + 1) * gp)
        ls = slice(half * gi, (half + 1) * gi)
        hr = hre_ref[:, hs].astype(BF16)
        hi = him_ref[:, hs].astype(BF16)
        for t in range(S5_CHUNK):
            acc = _dot(hr, vre_ref[t, half]) + _dot(hi, vim_ref[t, half])
            for s in range(t + 1):
                acc = acc + _dot(us[s][:, ls], m_ref[t - s, half])
            y_ref[pl.ds(t, n, stride=S5_CHUNK), ls] = acc


def s5_outputs(p_a, m, hre, him, vre, vim, layer, tr):
    rows = p_a.shape[0] // S5_CHUNK
    gi, gp = S5_GB * S5_GROUP, S5_GB * S5_STATE
    n_b = S5_GROUPS // (S5_GB * S5_GB_STEP)
    mspec = pl.BlockSpec((None, S5_CHUNK, S5_GB_STEP, gi, gi), lambda i, b: (layer, 0, b, 0, 0))
    vspec = pl.BlockSpec((None, S5_CHUNK, S5_GB_STEP, gp, gi), lambda i, b: (layer, 0, b, 0, 0))
    hspec = pl.BlockSpec((tr, S5_GB_STEP * gp), lambda i, b: (i, b))
    uspec = pl.BlockSpec((S5_CHUNK * tr, S5_GB_STEP * gi), lambda i, b: (i, b))
    return pl.pallas_call(
        _s5_y_kernel,
        out_shape=jax.ShapeDtypeStruct((p_a.shape[0], BRANCH_W), F32),
        grid=(rows // tr, n_b),
        in_specs=[uspec, mspec, hspec, hspec, vspec, vspec],
        out_specs=uspec,
        compiler_params=_params("parallel", "parallel"),
        name="s5_outputs",
    )(p_a, m, hre, him, vre, vim)


def _s5_epilogue_kernel(y_ref, u_ref, g_ref, d_ref, w_ref, o_ref):
    z = _gelu_tanh(y_ref[...] + d_ref[...] * u_ref[...])
    o = z * _sigmoid(_dot(z.astype(BF16), w_ref[...]))
    o_ref[...] = (o * _silu(g_ref[...])).astype(o_ref.dtype)


def s5_epilogue(y, p_a, d_skip, w_glu, tr):
    rows, w = y.shape
    return pl.pallas_call(
        _s5_epilogue_kernel,
        out_shape=jax.ShapeDtypeStruct((N_BRANCH, rows, w), BF16),
        grid=(rows // tr,),
        in_specs=[pl.BlockSpec((tr, w), lambda i: (i, 0)),
                  pl.BlockSpec((tr, w), lambda i: (i, 0)),
                  pl.BlockSpec((tr, w), lambda i: (i, 1)),
                  pl.BlockSpec((1, w), lambda i: (0, 0)),
                  pl.BlockSpec((w, w), lambda i: (0, 0))],
        out_specs=pl.BlockSpec((None, tr, w), lambda i: (0, i, 0)),
        compiler_params=_params("parallel"),
        name="s5_epilogue",
    )(y, p_a, p_a, d_skip, w_glu)


_IN_WIDTHS = (("a_x", 1024), ("a_g", 1024), ("b_q", 1024), ("b_f", 1024), ("b_i", 1024), ("b_g", 1024),
              ("c_q", 512), ("c_k", 512), ("c_v", 1024), ("c_r", 16), ("c_g", 1024),
              ("d_x", 1024), ("d_g", 1024), ("e_q", 1024), ("e_g", 1024))

TM_NORM = 512
TM_MM = 1088
TM_MERGE = 1088
TN_MERGE = 1024
TM_OUT = 544
TR_S5 = 544
TR_S5_EPI = 544
CW_S5_SCAN = 256
CHUNK_GLA = 128
NB_GLA_SAMPLE = 8
CHUNK_LRU = 128
ROWS_MEM_PROMPT = 512
NB_MEM_SAMPLE = 4


def _in_cols(w_in_l):
    cols, off = {}, 0
    for name, width in _IN_WIDTHS:
        cols[name] = w_in_l[:, off:off + width]
        off += width
    return cols


def kernel(x_prompt, x_sample, mem_prompt, state_s5_re, state_s5_im, state_hgrn, state_gla, state_rglru, state_conv, cache_mem_k, cache_mem_v, norm_pre, norm_post, w_in, s5_lam_re, s5_lam_im, s5_log_dt, s5_b_re, s5_b_im, s5_c_re, s5_c_im, s5_d, s5_w_glu, hg_lb_logits, hg_norm, gla_w_up, gla_b_up, gla_norm, conv_w, conv_b, lru_w_r, lru_b_r, lru_w_i, lru_b_i, lru_lam, mem_norm, w_mem_k, w_mem_v, w_branch, w_merge, w_out):
    d, w = D_MODEL, BRANCH_W
    m_p, m_s = BATCH * SEQ, DEC_BATCH * DEC_SEQ
    m_all = m_p + m_s
    bf = lambda a: a.astype(BF16)
    row = lambda v: v[None, :]

    x = jnp.concatenate([x_prompt.reshape(m_p, d), x_sample.reshape(m_s, d)], axis=0)

    mem2 = mem_prompt.reshape(BATCH * N_MEM, d)
    mk_l, mv_l = [], []
    for l in range(DEPTH):
        mn = rmsnorm_rows(mem2, row(mem_norm[l]), TM_NORM)
        mem5 = (BATCH, N_MEM, MEM_HEADS, MEM_HD)
        mk_l.append(matmul(mn, bf(w_mem_k[l]), BATCH * N_MEM, w // 2).reshape(mem5))
        mv_l.append(matmul(mn, bf(w_mem_v[l]), BATCH * N_MEM, w // 2).reshape(mem5))

    lb = hg_lower_bounds(hg_lb_logits)
    s5_m, s5_wre, s5_wim, s5_vre, s5_vim = s5_prepare(s5_lam_re, s5_lam_im, s5_log_dt, s5_b_re, s5_b_im,
                                                      s5_c_re, s5_c_im)
    gs = S5_GROUPS * S5_STATE
    lam_re_row = s5_lam_re.reshape(DEPTH, 1, gs)
    lam_im_row = s5_lam_im.reshape(DEPTH, 1, gs)
    log_dt_row = jnp.broadcast_to(s5_log_dt[:, :, None], (DEPTH, S5_GROUPS, S5_STATE)).reshape(DEPTH, 1, gs)
    s5_h0r = state_s5_re.reshape(DEPTH, DEC_BATCH, gs)
    s5_h0i = state_s5_im.reshape(DEPTH, DEC_BATCH, gs)
    w_up_pad = jnp.zeros((DEPTH, V7X_LANES, GLA_KEY_W), F32).at[:, :GLA_RANK].set(gla_w_up)

    hg_p = hg_s = gl_p = gl_s = None
    outs = {k: [] for k in ("p_s5r", "p_s5i", "p_lru", "p_conv", "s_s5r", "s_s5i", "s_lru", "s_conv")}
    hg_cols = [(w, 0), (w, 1), (w, 2), (w, 3)]
    gl_cols = [(GLA_KEY_W, 4), (GLA_KEY_W, 5), (w, 0), (V7X_LANES, 3 * w // V7X_LANES), (w, 1)]
    for l in range(DEPTH):
        cols = _in_cols(w_in[l])
        w_a = bf(jnp.concatenate([cols["a_x"], cols["a_g"]], axis=1))
        w_b = bf(jnp.concatenate([cols["b_q"], cols["b_f"], cols["b_i"], cols["b_g"]], axis=1))
        w_c = bf(jnp.concatenate([cols["c_v"], cols["c_g"], cols["c_q"], cols["c_k"], cols["c_r"],
                                  jnp.zeros((d, V7X_LANES - GLA_RANK), F32)], axis=1))
        w_d = bf(jnp.concatenate([cols["d_x"], cols["d_g"]], axis=1))
        w_e = bf(jnp.concatenate([cols["e_q"], cols["e_g"]], axis=1))

        h = rmsnorm_rows(x, row(norm_pre[l]), TM_NORM)
        p_a = matmul(h, w_a, TM_MM, w)
        p_b = matmul(h, w_b, TM_MM, w)
        p_c = matmul(h, w_c, TM_MM, 5 * V7X_LANES)
        p_d = matmul(h, w_d, TM_MM, w)
        p_e = matmul(h, w_e, TM_MM, w)

        ere, eim = s5_chunk_inputs(p_a, s5_wre, s5_wim, l, TR_S5)
        hre, him, fpr, fpi, fsr, fsi = s5_scan(ere, eim, s5_h0r[l], s5_h0i[l], lam_re_row, lam_im_row, log_dt_row,
                                               l, n_seq=BATCH, rows_per_seq=SEQ // S5_CHUNK, cw=CW_S5_SCAN)
        y = s5_outputs(p_a, s5_m, hre, him, s5_vre, s5_vim, l, TR_S5)
        br = s5_epilogue(y, p_a, row(s5_d[l]), bf(s5_w_glu[l]), TR_S5_EPI)
        outs["p_s5r"].append(fpr); outs["p_s5i"].append(fpi)
        outs["s_s5r"].append(fsr); outs["s_s5i"].append(fsi)

        hg_consts = [row(lb[l]), row(hg_norm[l])]
        hg_kw = dict(layer=l, slab=1, heads=HG_HEADS, dk=HG_DK, dv=HG_DV)
        br, hg_p = gla_branch("hgrn", p_b, hg_cols, hg_consts, None, br, hg_p, row_base=0, n_seq=BATCH,
                              seq_total=SEQ, chunk=CHUNK_GLA, nb=1, **hg_kw)
        br, hg_s = gla_branch("hgrn", p_b, hg_cols, hg_consts, state_hgrn, br, hg_s, row_base=m_p,
                              n_seq=DEC_BATCH, seq_total=DEC_SEQ, chunk=DEC_SEQ, nb=NB_GLA_SAMPLE, **hg_kw)

        gl_consts = [w_up_pad[l], row(gla_b_up[l]), row(gla_norm[l])]
        gl_kw = dict(layer=l, slab=2, heads=GLA_HEADS, dk=GLA_DK, dv=GLA_DV)
        br, gl_p = gla_branch("gla", p_c, gl_cols, gl_consts, None, br, gl_p, row_base=0, n_seq=BATCH,
                              seq_total=SEQ, chunk=CHUNK_GLA, nb=1, **gl_kw)
        br, gl_s = gla_branch("gla", p_c, gl_cols, gl_consts, state_gla, br, gl_s, row_base=m_p,
                              n_seq=DEC_BATCH, seq_total=DEC_SEQ, chunk=DEC_SEQ, nb=NB_GLA_SAMPLE, **gl_kw)

        lru_consts = [conv_w[l], row(conv_b[l]), bf(lru_w_r[l]), row(lru_b_r[l]), bf(lru_w_i[l]), row(lru_b_i[l]),
                      row(lru_lam[l])]
        br, hfin_p, cfin_p = lru_prompt(p_d, lru_consts, br, slab=3, n_seq=BATCH, seq_total=SEQ, chunk=CHUNK_LRU)
        br, hfin_s, cfin_s = lru_sample(p_d, jnp.transpose(state_conv[l], (1, 0, 2)), state_rglru[l],
                                        lru_consts, br, slab=3, row_base=m_p)
        outs["p_lru"].append(hfin_p); outs["p_conv"].append(cfin_p)
        outs["s_lru"].append(hfin_s); outs["s_conv"].append(cfin_s)

        br = mem_attention(p_e, mk_l[l][None], mv_l[l][None], br, layer=0, slab=4, row_base=0, n_seq=BATCH,
                           seq_total=SEQ, rows_blk=ROWS_MEM_PROMPT, nbm=1)
        br = mem_attention(p_e, cache_mem_k, cache_mem_v, br, layer=l, slab=4, row_base=m_p, n_seq=DEC_BATCH,
                           seq_total=DEC_SEQ, rows_blk=NB_MEM_SAMPLE * DEC_SEQ, nbm=NB_MEM_SAMPLE)

        merged = merge_branches(h, br, bf(w_merge[l]), bf(w_branch[l]), TM_MERGE, TN_MERGE)
        x = out_proj_residual(merged, bf(w_out[l]), row(norm_post[l]), x, TM_OUT)

    st = lambda k, shape: jnp.stack(outs[k]).reshape(shape)
    s5_p_shape = (DEPTH, BATCH, S5_GROUPS, S5_STATE)
    s5_s_shape = (DEPTH, DEC_BATCH, S5_GROUPS, S5_STATE)
    mem_shape = (DEPTH, BATCH, N_MEM, MEM_HEADS, MEM_HD)
    return (x[:m_p].reshape(BATCH, SEQ, d), x[m_p:].reshape(DEC_BATCH, DEC_SEQ, d),
            st("p_s5r", s5_p_shape), st("p_s5i", s5_p_shape), hg_p, gl_p,
            st("p_lru", (DEPTH, BATCH, w)), st("p_conv", (DEPTH, BATCH, CONV_W - 1, w)),
            jnp.stack(mk_l), jnp.stack(mv_l),
            st("s_s5r", s5_s_shape), st("s_s5i", s5_s_shape), hg_s, gl_s,
            st("s_lru", (DEPTH, DEC_BATCH, w)), jnp.transpose(jnp.stack(outs["s_conv"]), (0, 2, 1, 3)))
```

```python
import functools
import math

import jax
import jax.numpy as jnp
from jax import lax
from jax.experimental import pallas as pl
from jax.experimental.pallas import tpu as pltpu

F32 = jnp.float32
BF16 = jnp.bfloat16
HIGHEST = lax.Precision.HIGHEST

V7X_LANES = 128
V7X_SUBLANES = 8
V7X_VMEM_LIMIT_BYTES = 56 * 1024 * 1024

EPS = 1e-6
D_MODEL = 2048
DEPTH = 4
BATCH = 4
SEQ = 2048
DEC_BATCH = 128
DEC_SEQ = 4
BRANCH_W = 1024
N_BRANCH = 5
S5_GROUP = 16
S5_GROUPS = 64
S5_STATE = 64
S5_CHUNK = 4
S5_GB = 8
HG_HEADS = 8
HG_DK = 128
HG_DV = 128
GLA_HEADS = 4
GLA_KEY_W = 512
GLA_DK = 128
GLA_DV = 256
GLA_RANK = 16
GLA_GATE_TEMP = 16.0
LRU_BLOCKS = 8
LRU_BW = 128
CONV_W = 4
LRU_C = 8.0
N_MEM = 256
MEM_HEADS = 4
MEM_HD = 256


def _params(*sem):
    return pltpu.CompilerParams(dimension_semantics=sem, vmem_limit_bytes=V7X_VMEM_LIMIT_BYTES)


def _dot(a, b):
    return jnp.dot(a, b, preferred_element_type=F32)


def _dot_nt(a, b):
    return lax.dot_general(a, b, (((1,), (1,)), ((), ())), preferred_element_type=F32)


def _dot_tn(a, b):
    return lax.dot_general(a, b, (((0,), (0,)), ((), ())), preferred_element_type=F32)


def _sigmoid(x):
    return 1.0 / (1.0 + jnp.exp(-x))


def _silu(x):
    return x * _sigmoid(x)


def _log1p_exp_neg_abs(x):
    return jnp.log(1.0 + jnp.exp(-jnp.abs(x)))


def _log_sigmoid(x):
    return jnp.minimum(x, 0.0) - _log1p_exp_neg_abs(x)


def _softplus(x):
    return jnp.maximum(x, 0.0) + _log1p_exp_neg_abs(x)


def _gelu_tanh(x):
    return 0.5 * x * (1.0 + jnp.tanh(math.sqrt(2.0 / math.pi) * (x + 0.044715 * (x * x * x))))


def _row_iota(shape):
    return lax.broadcasted_iota(jnp.int32, shape, 0)


def _col_iota(shape):
    return lax.broadcasted_iota(jnp.int32, shape, 1)


def _shift_rows(x, k):
    return pltpu.roll(x, k, 0)


def _shift_rows_up(x, k):
    return pltpu.roll(x, x.shape[0] - k, 0)


def _rmsnorm_kernel(x_ref, g_ref, o_ref):
    x = x_ref[...]
    y = x * lax.rsqrt(jnp.mean(x * x, axis=-1, keepdims=True) + EPS)
    o_ref[...] = (y * g_ref[...]).astype(o_ref.dtype)


def rmsnorm_rows(x, g, tm):
    m, d = x.shape
    return pl.pallas_call(
        _rmsnorm_kernel,
        out_shape=jax.ShapeDtypeStruct((m, d), BF16),
        grid=(m // tm,),
        in_specs=[pl.BlockSpec((tm, d), lambda i: (i, 0)), pl.BlockSpec((1, d), lambda i: (0, 0))],
        out_specs=pl.BlockSpec((tm, d), lambda i: (i, 0)),
        compiler_params=_params("parallel"),
        name="rmsnorm_rows",
    )(x, g)


def _matmul_kernel(a_ref, w_ref, o_ref):
    o_ref[...] = _dot(a_ref[...], w_ref[...]).astype(o_ref.dtype)


def matmul(a, w, tm, tn, out_dtype=F32):
    m, k = a.shape
    n = w.shape[1]
    return pl.pallas_call(
        _matmul_kernel,
        out_shape=jax.ShapeDtypeStruct((m, n), out_dtype),
        grid=(m // tm, n // tn),
        in_specs=[pl.BlockSpec((tm, k), lambda i, j: (i, 0)), pl.BlockSpec((k, tn), lambda i, j: (0, j))],
        out_specs=pl.BlockSpec((tm, tn), lambda i, j: (i, j)),
        compiler_params=_params("parallel", "parallel"),
        name="matmul",
    )(a, w)


def _merge_kernel(h_ref, b_ref, wm_ref, wb_ref, o_ref, acc_ref):
    c = pl.program_id(1)
    j = pl.program_id(2)
    nj, _, tn = acc_ref.shape
    contrib = _sigmoid(_dot(h_ref[...], wm_ref[...])) * _dot(b_ref[...], wb_ref[...])

    @pl.when(c == 0)
    def _():
        acc_ref[j] = contrib

    @pl.when(jnp.logical_and(c > 0, c < N_BRANCH - 1))
    def _():
        acc_ref[j] += contrib

    for jj in range(nj):
        @pl.when(jnp.logical_and(c == N_BRANCH - 1, j == jj))
        def _(jj=jj):
            o_ref[:, jj * tn:(jj + 1) * tn] = (acc_ref[jj] + contrib).astype(o_ref.dtype)


def merge_branches(h, branches, w_merge, w_branch, tm, tn):
    m, d = h.shape
    w = branches.shape[2]
    nj = d // tn
    return pl.pallas_call(
        _merge_kernel,
        out_shape=jax.ShapeDtypeStruct((m, d), BF16),
        grid=(m // tm, N_BRANCH, nj),
        in_specs=[pl.BlockSpec((tm, d), lambda i, c, j: (i, 0)),
                  pl.BlockSpec((None, tm, w), lambda i, c, j: (c, i, 0)),
                  pl.BlockSpec((d, tn), lambda i, c, j: (0, c * nj + j)),
                  pl.BlockSpec((None, w, tn), lambda i, c, j: (c, 0, j))],
        out_specs=pl.BlockSpec((tm, d), lambda i, c, j: (i, 0)),
        scratch_shapes=[pltpu.VMEM((nj, tm, tn), F32)],
        compiler_params=_params("parallel", "arbitrary", "arbitrary"),
        name="merge_branches",
    )(h, branches, w_merge, w_branch)


def _out_kernel(m_ref, w_ref, g_ref, x_ref, o_ref):
    y = _dot(m_ref[...], w_ref[...])
    y = y * lax.rsqrt(jnp.mean(y * y, axis=-1, keepdims=True) + EPS)
    o_ref[...] = x_ref[...] + y * g_ref[...]


def out_proj_residual(merged, w_out, g, x, tm):
    m, d = x.shape
    return pl.pallas_call(
        _out_kernel,
        out_shape=jax.ShapeDtypeStruct((m, d), F32),
        grid=(m // tm,),
        in_specs=[pl.BlockSpec((tm, d), lambda i: (i, 0)), pl.BlockSpec((d, d), lambda i: (0, 0)),
                  pl.BlockSpec((1, d), lambda i: (0, 0)), pl.BlockSpec((tm, d), lambda i: (i, 0))],
        out_specs=pl.BlockSpec((tm, d), lambda i: (i, 0)),
        compiler_params=_params("parallel"),
        name="out_proj_residual",
    )(merged, w_out, g, x)


def _hg_lb_kernel(logit_ref, lb_ref):
    x = logit_ref[...]
    e = jnp.exp(x - jnp.max(x, axis=0, keepdims=True))
    sm = e / jnp.sum(e, axis=0, keepdims=True)
    acc = jnp.zeros_like(sm[0:1])
    lb_ref[0:1, :] = acc
    for l in range(1, DEPTH):
        acc = acc + sm[l:l + 1]
        lb_ref[l:l + 1, :] = acc


def hg_lower_bounds(logits):
    return pl.pallas_call(
        _hg_lb_kernel,
        out_shape=jax.ShapeDtypeStruct(logits.shape, F32),
        name="hg_lower_bounds",
    )(logits)


def _gla_kernel(*refs, mode, seq_len, nb, heads, dk, dv, has_init):
    L = seq_len
    R = nb * L
    C = heads * dk
    if mode == "hgrn":
        q_ref, f_ref, v_ref, g_ref, lb_ref, gn_ref = refs[:6]
        rest = refs[6:]
    else:
        q_ref, k_ref, v_ref, r_ref, g_ref, wup_ref, bup_ref, gn_ref = refs[:8]
        rest = refs[8:]
    if has_init:
        s0_ref, rest = rest[0], rest[1:]
    o_ref, sout_ref, s_scr, att_scr, qe_scr, kd_scr, b_scr = rest

    c = pl.program_id(1)

    @pl.when(c == 0)
    def _():
        if has_init:
            s_scr[...] = s0_ref[...]
        else:
            s_scr[...] = jnp.zeros_like(s_scr)

    t = _row_iota((R, C)) % L
    if mode == "hgrn":
        lb = lb_ref[...]
        fz = f_ref[...]
        la = jnp.log(lb)
        lc = jnp.log1p(-lb) + _log_sigmoid(fz)
        g = jnp.maximum(la, lc) + _log1p_exp_neg_abs(la - lc)
        k = (1.0 - lb) * _sigmoid(-fz)
        q = q_ref[...]
    else:
        pre = jnp.dot(r_ref[...], wup_ref[...], preferred_element_type=F32, precision=HIGHEST)
        g = _log_sigmoid(pre + bup_ref[...]) * (1.0 / GLA_GATE_TEMP)
        k = k_ref[...]
        q = q_ref[...] * (dk ** -0.5)

    rr = _row_iota((R, R))
    cc = _col_iota((R, R))
    same_seq = (rr // L) == (cc // L)
    tri_incl = jnp.where(same_seq, jnp.where(cc <= rr, 1.0, 0.0), 0.0)
    tri_after = jnp.where(same_seq, jnp.where(cc > rr, 1.0, 0.0), 0.0)
    b = jnp.dot(tri_incl, g, preferred_element_type=F32, precision=HIGHEST)
    rb = jnp.dot(tri_after, g, preferred_element_type=F32, precision=HIGHEST)
    b_scr[...] = b
    qe_scr[...] = (q * jnp.exp(b)).astype(BF16)
    kd_scr[...] = (k * jnp.exp(rb)).astype(BF16)

    def add_level(level, qt, kt, mask):
        qt = qt.astype(BF16)
        kt = kt.astype(BF16)
        for h in range(heads):
            hs = slice(h * dk, (h + 1) * dk)
            p = jnp.where(mask, _dot_nt(qt[:, hs], kt[:, hs]), 0.0)
            if level == 0:
                att_scr[h] = p
            else:
                att_scr[h] += p

    add_level(0, q, k, rr == cc)
    level = 1
    m = 1
    while m < L:
        pos = t % (2 * m)
        upper = pos >= m
        if m == 1:
            d = jnp.where(upper, g, 0.0)
        elif m == 2:
            d = jnp.where(pos == 2, g,
                          jnp.where(pos == 3, g + _shift_rows(g, 1),
                                    jnp.where(pos == 0, _shift_rows_up(g, 1), 0.0)))
        else:
            b3 = b.reshape(R // (2 * m), 2 * m, C)
            d3 = b3 - b3[:, m - 1:m, :]
            d = d3.reshape(R, C)
            d = jnp.where(upper, d, -d)
        x = jnp.where(upper, q, k) * jnp.exp(d)
        add_level(level, jnp.where(upper, x, 0.0), jnp.where(upper, 0.0, x),
                  (rr // (2 * m)) == (cc // (2 * m)))
        level += 1
        m *= 2

    gn = gn_ref[...]
    seq_o = _row_iota((R, dv)) // L
    seq_k = _row_iota((R, dk)) // L
    pad_rows = (-R) % V7X_LANES
    for h in range(heads):
        hs = slice(h * dk, (h + 1) * dk)
        vs = slice(h * dv, (h + 1) * dv)
        vh = v_ref[:, vs].astype(BF16)
        vh_p = jnp.concatenate([vh, jnp.zeros((pad_rows, dv), BF16)], axis=0) if pad_rows else vh
        o = _dot(att_scr[h].astype(BF16), vh)
        qe = qe_scr[:, hs]
        kd = kd_scr[:, hs]
        for j in range(nb):
            st = s_scr[j, h]
            inter = _dot(qe, st.astype(BF16))
            kd_j = kd
            if nb > 1:
                inter = jnp.where(seq_o == j, inter, 0.0)
                kd_j = jnp.where(seq_k == j, kd, jnp.zeros_like(kd))
            o = o + inter
            if pad_rows:
                kd_j = jnp.concatenate([kd_j, jnp.zeros((pad_rows, dk), BF16)], axis=0)
            upd = _dot_tn(kd_j, vh_p)
            dl = jnp.exp(b_scr[j * L + L - 1:j * L + L, hs])
            colb = jnp.broadcast_to(dl, (dk, dk)).T
            if dv != dk:
                colb = jnp.concatenate([colb] * (dv // dk), axis=1)
            s_scr[j, h] = colb * st + upd
        on = o * lax.rsqrt(jnp.mean(o * o, axis=-1, keepdims=True) + EPS) * gn
        o_ref[:, vs] = (on * _silu(g_ref[:, vs])).astype(o_ref.dtype)

    @pl.when(c == pl.num_programs(1) - 1)
    def _():
        sout_ref[...] = s_scr[...]


def gla_branch(mode, p, col_blocks, consts, s0, out_prev, sout_prev, *, layer, slab, row_base, n_seq, seq_total,
               chunk, nb, heads, dk, dv):
    L = chunk
    R = nb * L
    n_chunks = seq_total // L
    base_blk = row_base // R
    width = heads * dv

    def rows(i, c):
        return base_blk + i * n_chunks + c

    in_specs = [pl.BlockSpec((R, w), functools.partial(lambda i, c, bi: (rows(i, c), bi), bi=bi))
                for (w, bi) in col_blocks]
    args = [p] * len(col_blocks)
    for a in consts:
        in_specs.append(pl.BlockSpec(a.shape, lambda i, c: (0, 0)))
        args.append(a)
    has_init = s0 is not None
    state_spec = pl.BlockSpec((None, nb, heads, dk, dv), lambda i, c: (layer, i, 0, 0, 0))
    if has_init:
        in_specs.append(state_spec)
        args.append(s0)
    n_real = len(args)
    aliases = {}
    for out_idx, prev in enumerate((out_prev, sout_prev)):
        if prev is not None:
            aliases[len(args)] = out_idx
            in_specs.append(pl.BlockSpec(memory_space=pl.ANY))
            args.append(prev)
    n_in = len(args)

    def kern(*refs):
        _gla_kernel(*refs[:n_real], *refs[n_in:], mode=mode, seq_len=L, nb=nb, heads=heads, dk=dk, dv=dv,
                    has_init=has_init)

    return pl.pallas_call(
        kern,
        out_shape=(jax.ShapeDtypeStruct(out_prev.shape, BF16),
                   jax.ShapeDtypeStruct((DEPTH, n_seq, heads, dk, dv), F32)),
        grid=(n_seq // nb, n_chunks),
        in_specs=in_specs,
        out_specs=(pl.BlockSpec((None, R, width), lambda i, c: (slab, rows(i, c), 0)), state_spec),
        scratch_shapes=[pltpu.VMEM((nb, heads, dk, dv), F32), pltpu.VMEM((heads, R, R), F32),
                        pltpu.VMEM((R, heads * dk), BF16), pltpu.VMEM((R, heads * dk), BF16),
                        pltpu.VMEM((R, heads * dk), F32)],
        input_output_aliases=aliases,
        compiler_params=_params("parallel", "arbitrary"),
        name=mode + "_branch",
    )(*args)


def _lru_gates(xc, wr_ref, br_ref, wi_ref, bi_ref, lam_ref):
    xcb = xc.astype(BF16)
    r_parts, i_parts = [], []
    for blk in range(xc.shape[1] // LRU_BW):
        bs = slice(blk * LRU_BW, (blk + 1) * LRU_BW)
        r_parts.append(_dot(xcb[:, bs], wr_ref[blk]))
        i_parts.append(_dot(xcb[:, bs], wi_ref[blk]))
    r = _sigmoid(jnp.concatenate(r_parts, axis=1) + br_ref[...])
    ig = _sigmoid(jnp.concatenate(i_parts, axis=1) + bi_ref[...])
    log_a = (-LRU_C) * r * _softplus(-lam_ref[...])
    a = jnp.exp(log_a)
    u = jnp.sqrt(1.0 - jnp.exp(2.0 * log_a)) * (ig * xc)
    return a, u


def _lru_prompt_kernel(x_ref, g_ref, cw_ref, cb_ref, wr_ref, br_ref, wi_ref, bi_ref, lam_ref,
                       o_ref, hfin_ref, cfin_ref, tail_scr, hc_scr, a_scr, u_scr):
    tc, w = x_ref.shape
    sub = V7X_SUBLANES
    c = pl.program_id(1)

    @pl.when(c == 0)
    def _():
        tail_scr[...] = jnp.zeros_like(tail_scr)
        hc_scr[...] = jnp.zeros_like(hc_scr)

    x = x_ref[...]
    tail = tail_scr[...]
    row8 = _row_iota((sub, w))
    cw = cw_ref[...]
    xc = cb_ref[...] + cw[CONV_W - 1:CONV_W] * x
    for k in range(1, CONV_W):
        xs = _shift_rows(x, k)
        head = jnp.where(row8 < k, _shift_rows(tail, k), xs[0:sub])
        xs = jnp.concatenate([head, xs[sub:]], axis=0)
        xc = xc + cw[CONV_W - 1 - k:CONV_W - k] * xs
    last8 = x[tc - sub:tc]
    tail_scr[...] = last8

    a, u = _lru_gates(xc, wr_ref, br_ref, wi_ref, bi_ref, lam_ref)
    a_scr[...] = a
    u_scr[...] = u

    def body(i, carry):
        r0 = pl.multiple_of(i * sub, sub)
        at = a_scr[pl.ds(r0, sub), :]
        ut = u_scr[pl.ds(r0, sub), :]
        for k in (1, 2, 4):
            keep = row8 >= k
            ut = ut + at * jnp.where(keep, _shift_rows(ut, k), 0.0)
            at = at * jnp.where(keep, _shift_rows(at, k), 1.0)
        ht = ut + at * carry
        u_scr[pl.ds(r0, sub), :] = ht
        return jnp.broadcast_to(ht[sub - 1:sub], (sub, w))

    carry = lax.fori_loop(0, tc // sub, body, hc_scr[...])
    hc_scr[...] = carry
    o_ref[...] = (u_scr[...] * _silu(g_ref[...])).astype(o_ref.dtype)

    @pl.when(c == pl.num_programs(1) - 1)
    def _():
        hfin_ref[...] = carry[0:1]
        cfin_ref[...] = _shift_rows(last8, CONV_W - 1)[0:CONV_W - 1]


def lru_prompt(p_d, consts, out_prev, *, slab, n_seq, seq_total, chunk):
    w = BRANCH_W
    n_chunks = seq_total // chunk
    const_specs = [pl.BlockSpec(a.shape, functools.partial(lambda i, c, nd: (0,) * nd, nd=a.ndim)) for a in consts]

    def kern(*refs):
        n_in = 2 + len(consts)
        _lru_prompt_kernel(*refs[:n_in], *refs[n_in + 1:])

    return pl.pallas_call(
        kern,
        out_shape=(jax.ShapeDtypeStruct(out_prev.shape, BF16),
                   jax.ShapeDtypeStruct((n_seq, 1, w), F32),
                   jax.ShapeDtypeStruct((n_seq, CONV_W - 1, w), F32)),
        grid=(n_seq, n_chunks),
        in_specs=[pl.BlockSpec((chunk, w), lambda i, c: (i * n_chunks + c, 0)),
                  pl.BlockSpec((chunk, w), lambda i, c: (i * n_chunks + c, 1))] + const_specs
        + [pl.BlockSpec(memory_space=pl.ANY)],
        out_specs=(pl.BlockSpec((None, chunk, w), lambda i, c: (slab, i * n_chunks + c, 0)),
                   pl.BlockSpec((None, 1, w), lambda i, c: (i, 0, 0)),
                   pl.BlockSpec((None, CONV_W - 1, w), lambda i, c: (i, 0, 0))),
        scratch_shapes=[pltpu.VMEM((V7X_SUBLANES, w), F32), pltpu.VMEM((V7X_SUBLANES, w), F32),
                        pltpu.VMEM((chunk, w), F32), pltpu.VMEM((chunk, w), F32)],
        input_output_aliases={2 + len(consts): 0},
        compiler_params=_params("parallel", "arbitrary"),
        name="lru_prompt",
    )(p_d, p_d, *consts, out_prev)


def _lru_sample_kernel(x_ref, g_ref, buf_ref, h0_ref, cw_ref, cb_ref, wr_ref, br_ref, wi_ref, bi_ref, lam_ref,
                       prev_ref, o_ref, hfin_ref, cfin_ref, o_scr):
    del prev_ref
    steps = DEC_SEQ
    n = x_ref.shape[0] // steps
    step_rows = lambda t: pl.ds(t, n, stride=steps)
    xcat = [buf_ref[j] for j in range(CONV_W - 1)] + [x_ref[step_rows(t), :] for t in range(steps)]
    cw = cw_ref[...]
    xcs = []
    for t in range(steps):
        xc = cb_ref[...]
        for j in range(CONV_W):
            xc = xc + cw[j:j + 1] * xcat[t + j]
        xcs.append(xc)
    a, u = _lru_gates(jnp.concatenate(xcs, axis=0), wr_ref, br_ref, wi_ref, bi_ref, lam_ref)
    h = h0_ref[...]
    for t in range(steps):
        h = a[t * n:(t + 1) * n] * h + u[t * n:(t + 1) * n]
        o_scr[step_rows(t), :] = h * _silu(g_ref[step_rows(t), :])
    o_ref[...] = o_scr[...].astype(o_ref.dtype)
    hfin_ref[...] = h
    for j in range(CONV_W - 1):
        cfin_ref[j] = xcat[steps + j]


def lru_sample(p_d, conv_buf, h0, consts, out_prev, *, slab, row_base):
    w = BRANCH_W
    bw = LRU_BW
    n = h0.shape[0]
    rows = n * DEC_SEQ
    blk = row_base // rows
    nbk = w // bw
    conv_w_, conv_b_, w_r, b_r, w_i, b_i, lam = consts
    lane = lambda r: pl.BlockSpec((r, bw), lambda i: (0, i))
    wblk = pl.BlockSpec((1, bw, bw), lambda i: (i, 0, 0))
    tail = pl.BlockSpec((CONV_W - 1, n, bw), lambda i: (0, 0, i))
    return pl.pallas_call(
        _lru_sample_kernel,
        out_shape=(jax.ShapeDtypeStruct(out_prev.shape, BF16),
                   jax.ShapeDtypeStruct((n, w), F32),
                   jax.ShapeDtypeStruct((CONV_W - 1, n, w), F32)),
        grid=(nbk,),
        in_specs=[pl.BlockSpec((rows, bw), lambda i: (blk, i)), pl.BlockSpec((rows, bw), lambda i: (blk, nbk + i)),
                  tail, lane(n), lane(CONV_W), lane(1), wblk, lane(1), wblk, lane(1), lane(1),
                  pl.BlockSpec(memory_space=pl.ANY)],
        out_specs=(pl.BlockSpec((None, rows, bw), lambda i: (slab, blk, i)), lane(n), tail),
        scratch_shapes=[pltpu.VMEM((rows, bw), F32)],
        input_output_aliases={11: 0},
        compiler_params=_params("parallel"),
        name="lru_sample",
    )(p_d, p_d, conv_buf, h0, conv_w_, conv_b_, w_r, b_r, w_i, b_i, lam, out_prev)


def _mem_attn_kernel(*refs, nbm, seq_rows, aliased):
    if aliased:
        q_ref, g_ref, k_ref, v_ref, _, o_ref = refs
    else:
        q_ref, g_ref, k_ref, v_ref, o_ref = refs
    rows = q_ref.shape[0]
    q = (q_ref[...] * (MEM_HD ** -0.5)).astype(BF16)
    seq = _row_iota((rows, MEM_HD)) // seq_rows
    for h in range(MEM_HEADS):
        hs = slice(h * MEM_HD, (h + 1) * MEM_HD)
        acc = None
        for j in range(nbm):
            kh = k_ref[j, :, hs]
            vh = v_ref[j, :, hs]
            s = _dot_nt(q[:, hs], kh)
            e = jnp.exp(s - jnp.max(s, axis=-1, keepdims=True))
            pr = e / jnp.sum(e, axis=-1, keepdims=True)
            oh = _dot(pr.astype(BF16), vh)
            acc = oh if acc is None else jnp.where(seq == j, oh, acc)
        o_ref[:, hs] = (acc * _silu(g_ref[:, hs])).astype(o_ref.dtype)


def mem_attention(p_e, mem_k, mem_v, out_prev, *, layer, slab, row_base, n_seq, seq_total, rows_blk, nbm):
    w = BRANCH_W
    base_blk = row_base // rows_blk
    if nbm == 1:
        t_blocks = seq_total // rows_blk
        grid = (n_seq, t_blocks)
        rmap = lambda i, t: base_blk + i * t_blocks + t
    else:
        grid = (n_seq // nbm, 1)
        rmap = lambda i, t: base_blk + i
    mem_spec = pl.BlockSpec((None, nbm, N_MEM, w), lambda i, t: (layer, i, 0, 0))
    in_specs = [pl.BlockSpec((rows_blk, w), lambda i, t: (rmap(i, t), 0)),
                pl.BlockSpec((rows_blk, w), lambda i, t: (rmap(i, t), 1)),
                mem_spec, mem_spec, pl.BlockSpec(memory_space=pl.ANY)]
    args = [p_e, p_e, mem_k, mem_v, out_prev]
    aliases = {4: 0}
    return pl.pallas_call(
        functools.partial(_mem_attn_kernel, nbm=nbm, seq_rows=seq_total, aliased=True),
        out_shape=jax.ShapeDtypeStruct(out_prev.shape, BF16),
        grid=grid,
        in_specs=in_specs,
        out_specs=pl.BlockSpec((None, rows_blk, w), lambda i, t: (slab, rmap(i, t), 0)),
        input_output_aliases=aliases,
        compiler_params=_params("parallel", "arbitrary"),
        name="mem_attention",
    )(*args)


def _cmul(ar, ai, br, bi):
    return ar * br - ai * bi, ar * bi + ai * br


def _s5_abar(lr, li, log_dt):
    dt = jnp.exp(log_dt)
    mag = jnp.exp(lr * dt)
    return mag * jnp.cos(li * dt), mag * jnp.sin(li * dt)


def _s5_prep_kernel(lra, lia, dta, bra, bia, cfr, cfi, lrc, lic, dtc, ctr, cti,
                    m_ref, wre_ref, wim_ref, vre_ref, vim_ref):
    gi = S5_GB * S5_GROUP
    gp = S5_GB * S5_STATE
    lr, li = lra[...], lia[...]
    ar, ai = _s5_abar(lr, li, dta[...])
    den = lr * lr + li * li
    zr = ((ar - 1.0) * lr + ai * li) / den
    zi = (ai * lr - (ar - 1.0) * li) / den
    bbr, bbi = _cmul(zr, zi, bra[...], bia[...])
    wmask = (_row_iota((gi, gp)) // S5_GROUP) == (_col_iota((gi, gp)) // S5_STATE)
    mmask = (_row_iota((gi, gi)) // S5_GROUP) == (_col_iota((gi, gi)) // S5_GROUP)
    cr, ci = cfr[...], cfi[...]
    pr, pi = jnp.ones_like(ar), jnp.zeros_like(ar)
    for tau in range(S5_CHUNK):
        lrr, lii = _cmul(pr, pi, bbr, bbi)
        wre_ref[S5_CHUNK - 1 - tau] = jnp.where(wmask, lrr, 0.0).astype(BF16)
        wim_ref[S5_CHUNK - 1 - tau] = jnp.where(wmask, lii, 0.0).astype(BF16)
        x = (lax.dot_general(lrr[:, :S5_STATE], cr, (((1,), (1,)), ((), ())), precision=HIGHEST,
                             preferred_element_type=F32)
             - lax.dot_general(lii[:, :S5_STATE], ci, (((1,), (1,)), ((), ())), precision=HIGHEST,
                               preferred_element_type=F32))
        m_ref[tau] = jnp.where(mmask, x, 0.0).astype(BF16)
        pr, pi = _cmul(pr, pi, ar, ai)

    arc, aic = _s5_abar(lrc[...], lic[...], dtc[...])
    vmask = (_row_iota((gp, gi)) // S5_STATE) == (_col_iota((gp, gi)) // S5_GROUP)
    pr, pi = arc, aic
    for t in range(S5_CHUNK):
        vr, vi = _cmul(ctr[...], cti[...], pr, pi)
        vre_ref[t] = jnp.where(vmask, vr, 0.0).astype(BF16)
        vim_ref[t] = jnp.where(vmask, -vi, 0.0).astype(BF16)
        pr, pi = _cmul(pr, pi, arc, aic)


def s5_prepare(lam_re, lam_im, log_dt, b_re, b_im, c_re, c_im):
    dp, g, p = lam_re.shape
    i = S5_GROUP
    gi, gp = S5_GB * i, S5_GB * p
    nb = g // S5_GB

    def a_layout(x_gp):
        return jnp.broadcast_to(x_gp[:, :, None, None, :], (dp, g, i, S5_GB, p)).reshape(dp, g * i, gp)

    def c_layout(x_gp):
        return jnp.broadcast_to(x_gp[:, :, :, None], (dp, g, p, gi)).reshape(dp, g * p, gi)

    ldt = jnp.broadcast_to(log_dt[:, :, None], (dp, g, p))
    bt = lambda b: jnp.broadcast_to(jnp.transpose(b, (0, 1, 3, 2))[:, :, :, None, :],
                                    (dp, g, i, S5_GB, p)).reshape(dp, g * i, gp)
    ct = lambda c: jnp.broadcast_to(jnp.transpose(c, (0, 1, 3, 2))[:, :, :, None, :],
                                    (dp, g, p, S5_GB, i)).reshape(dp, g * p, gi)
    args = [a_layout(lam_re), a_layout(lam_im), a_layout(ldt), bt(b_re), bt(b_im),
            c_re.reshape(dp, g * i, p), c_im.reshape(dp, g * i, p),
            c_layout(lam_re), c_layout(lam_im), c_layout(ldt), ct(c_re), ct(c_im)]
    spec_a = pl.BlockSpec((None, gi, gp), lambda l, b: (l, b, 0))
    spec_f = pl.BlockSpec((None, gi, p), lambda l, b: (l, b, 0))
    spec_c = pl.BlockSpec((None, gp, gi), lambda l, b: (l, b, 0))
    out5 = lambda r, c: pl.BlockSpec((None, S5_CHUNK, None, r, c), lambda l, b: (l, 0, b, 0, 0))
    shp = lambda r, c: jax.ShapeDtypeStruct((dp, S5_CHUNK, nb, r, c), BF16)
    return pl.pallas_call(
        _s5_prep_kernel,
        out_shape=(shp(gi, gi), shp(gi, gp), shp(gi, gp), shp(gp, gi), shp(gp, gi)),
        grid=(dp, nb),
        in_specs=[spec_a] * 5 + [spec_f] * 2 + [spec_c] * 5,
        out_specs=(out5(gi, gi), out5(gi, gp), out5(gi, gp), out5(gp, gi), out5(gp, gi)),
        compiler_params=_params("parallel", "parallel"),
        name="s5_prepare",
    )(*args)


S5_GB_STEP = 1


def _s5_steps(u_ref):
    n = u_ref.shape[0] // S5_CHUNK
    return [u_ref[pl.ds(s, n, stride=S5_CHUNK), :].astype(BF16) for s in range(S5_CHUNK)]


def _s5_e_kernel(u_ref, wre_ref, wim_ref, ere_ref, eim_ref):
    gi, gp = S5_GB * S5_GROUP, S5_GB * S5_STATE
    us = _s5_steps(u_ref)
    for half in range(S5_GB_STEP):
        accr = acci = None
        for s, u in enumerate(us):
            ub = u[:, half * gi:(half + 1) * gi]
            r = _dot(ub, wre_ref[s, half])
            i = _dot(ub, wim_ref[s, half])
            accr = r if accr is None else accr + r
            acci = i if acci is None else acci + i
        ere_ref[:, half * gp:(half + 1) * gp] = accr
        eim_ref[:, half * gp:(half + 1) * gp] = acci


def s5_chunk_inputs(p_a, wre, wim, layer, tr):
    rows = p_a.shape[0] // S5_CHUNK
    gi, gp = S5_GB * S5_GROUP, S5_GB * S5_STATE
    n_b = S5_GROUPS // (S5_GB * S5_GB_STEP)
    wspec = pl.BlockSpec((None, S5_CHUNK, S5_GB_STEP, gi, gp), lambda i, b: (layer, 0, b, 0, 0))
    ospec = pl.BlockSpec((tr, S5_GB_STEP * gp), lambda i, b: (i, b))
    oshape = jax.ShapeDtypeStruct((rows, S5_GROUPS * S5_STATE), F32)
    return pl.pallas_call(
        _s5_e_kernel,
        out_shape=(oshape, oshape),
        grid=(rows // tr, n_b),
        in_specs=[pl.BlockSpec((S5_CHUNK * tr, S5_GB_STEP * gi), lambda i, b: (i, b)), wspec, wspec],
        out_specs=(ospec, ospec),
        compiler_params=_params("parallel", "parallel"),
        name="s5_chunk_inputs",
    )(p_a, wre, wim)


def _s5_scan_kernel(ere, eim, h0r, h0i, lr_ref, li_ref, ldt_ref, hre, him, fpr, fpi, fsr, fsi,
                    *, n_seq, rows_per_seq):
    sub = V7X_SUBLANES
    cw = ere.shape[1]
    ar, ai = _s5_abar(lr_ref[...], li_ref[...], ldt_ref[...])
    a2 = _cmul(ar, ai, ar, ai)
    p1 = _cmul(*a2, *a2)
    p2 = _cmul(*p1, *p1)
    p4 = _cmul(*p2, *p2)
    p8 = _cmul(*p4, *p4)
    row8 = _row_iota((sub, cw))
    tr_, ti_ = jnp.ones((sub, cw), F32), jnp.zeros((sub, cw), F32)
    for bit, pw in ((1, p1), (2, p2), (4, p4)):
        nr, ni = _cmul(tr_, ti_, *pw)
        sel = (row8 & bit) != 0
        tr_, ti_ = jnp.where(sel, nr, tr_), jnp.where(sel, ni, ti_)

    base = n_seq * rows_per_seq
    n_s = h0r.shape[0]
    h0r_v, h0i_v = h0r[...], h0i[...]
    hre[base:base + n_s, :] = h0r_v
    him[base:base + n_s, :] = h0i_v
    dr, di = _cmul(p1[0], p1[1], h0r_v, h0i_v)
    fsr[...] = dr + ere[base:base + n_s, :]
    fsi[...] = di + eim[base:base + n_s, :]

    for n in range(n_seq):
        def body(j, carry, n=n):
            cr, ci = carry
            r0 = pl.multiple_of(n * rows_per_seq + j * sub, sub)
            xr = ere[pl.ds(r0, sub), :]
            xi = eim[pl.ds(r0, sub), :]
            for k, pw in ((1, p1), (2, p2), (4, p4)):
                keep = row8 >= k
                sr = jnp.where(keep, _shift_rows(xr, k), 0.0)
                si = jnp.where(keep, _shift_rows(xi, k), 0.0)
                mr, mi = _cmul(pw[0], pw[1], sr, si)
                xr, xi = xr + mr, xi + mi
            er = jnp.where(row8 >= 1, _shift_rows(xr, 1), 0.0)
            ei = jnp.where(row8 >= 1, _shift_rows(xi, 1), 0.0)
            qr, qi = _cmul(tr_, ti_, cr, ci)
            hre[pl.ds(r0, sub), :] = er + qr
            him[pl.ds(r0, sub), :] = ei + qi
            nr, ni = _cmul(p8[0], p8[1], cr, ci)
            return xr[sub - 1:sub] + nr, xi[sub - 1:sub] + ni

        zero = jnp.zeros((1, cw), F32)
        cr, ci = lax.fori_loop(0, rows_per_seq // sub, body, (zero, zero))
        fpr[n:n + 1, :] = cr
        fpi[n:n + 1, :] = ci


def s5_scan(ere, eim, h0r, h0i, lam_re_row, lam_im_row, log_dt_row, layer, *, n_seq, rows_per_seq, cw):
    rows, width = ere.shape
    n_s = h0r.shape[0]
    col = lambda r: pl.BlockSpec((r, cw), lambda j: (0, j))
    prow = pl.BlockSpec((None, 1, cw), lambda j: (layer, 0, j))
    shp = lambda r: jax.ShapeDtypeStruct((r, width), F32)
    return pl.pallas_call(
        functools.partial(_s5_scan_kernel, n_seq=n_seq, rows_per_seq=rows_per_seq),
        out_shape=(shp(rows), shp(rows), shp(n_seq), shp(n_seq), shp(n_s), shp(n_s)),
        grid=(width // cw,),
        in_specs=[col(rows), col(rows), col(n_s), col(n_s), prow, prow, prow],
        out_specs=(col(rows), col(rows), col(n_seq), col(n_seq), col(n_s), col(n_s)),
        compiler_params=_params("parallel"),
        name="s5_scan",
    )(ere, eim, h0r, h0i, lam_re_row, lam_im_row, log_dt_row)


def _s5_y_kernel(u_ref, m_ref, hre_ref, him_ref, vre_ref, vim_ref, y_ref):
    gi, gp = S5_GB * S5_GROUP, S5_GB * S5_STATE
    us = _s5_steps(u_ref)
    n = hre_ref.shape[0]
    for half in range(S5_GB_STEP):
        hs = slice(half * gp, (half + 1) * gp)
        ls = slice(half * gi, (half + 1) * gi)
        hr = hre_ref[:, hs].astype(BF16)
        hi = him_ref[:, hs].astype(BF16)
        for t in range(S5_CHUNK):
            acc = _dot(hr, vre_ref[t, half]) + _dot(hi, vim_ref[t, half])
            for s in range(t + 1):
                acc = acc + _dot(us[s][:, ls], m_ref[t - s, half])
            y_ref[pl.ds(t, n, stride=S5_CHUNK), ls] = acc


def s5_outputs(p_a, m, hre, him, vre, vim, layer, tr):
    rows = p_a.shape[0] // S5_CHUNK
    gi, gp = S5_GB * S5_GROUP, S5_GB * S5_STATE
    n_b = S5_GROUPS // (S5_GB * S5_GB_STEP)
    mspec = pl.BlockSpec((None, S5_CHUNK, S5_GB_STEP, gi, gi), lambda i, b: (layer, 0, b, 0, 0))
    vspec = pl.BlockSpec((None, S5_CHUNK, S5_GB_STEP, gp, gi), lambda i, b: (layer, 0, b, 0, 0))
    hspec = pl.BlockSpec((tr, S5_GB_STEP * gp), lambda i, b: (i, b))
    uspec = pl.BlockSpec((S5_CHUNK * tr, S5_GB_STEP * gi), lambda i, b: (i, b))
    return pl.pallas_call(
        _s5_y_kernel,
        out_shape=jax.ShapeDtypeStruct((p_a.shape[0], BRANCH_W), F32),
        grid=(rows // tr, n_b),
        in_specs=[uspec, mspec, hspec, hspec, vspec, vspec],
        out_specs=uspec,
        compiler_params=_params("parallel", "parallel"),
        name="s5_outputs",
    )(p_a, m, hre, him, vre, vim)


def _s5_epilogue_kernel(y_ref, u_ref, g_ref, d_ref, w_ref, o_ref):
    z = _gelu_tanh(y_ref[...] + d_ref[...] * u_ref[...])
    o = z * _sigmoid(_dot(z.astype(BF16), w_ref[...]))
    o_ref[...] = (o * _silu(g_ref[...])).astype(o_ref.dtype)


def s5_epilogue(y, p_a, d_skip, w_glu, tr):
    rows, w = y.shape
    return pl.pallas_call(
        _s5_epilogue_kernel,
        out_shape=jax.ShapeDtypeStruct((N_BRANCH, rows, w), BF16),
        grid=(rows // tr,),
        in_specs=[pl.BlockSpec((tr, w), lambda i: (i, 0)),
                  pl.BlockSpec((tr, w), lambda i: (i, 0)),
                  pl.BlockSpec((tr, w), lambda i: (i, 1)),
                  pl.BlockSpec((1, w), lambda i: (0, 0)),
                  pl.BlockSpec((w, w), lambda i: (0, 0))],
        out_specs=pl.BlockSpec((None, tr, w), lambda i: (0, i, 0)),
        compiler_params=_params("parallel"),
        name="s5_epilogue",
    )(y, p_a, p_a, d_skip, w_glu)


_IN_WIDTHS = (("a_x", 1024), ("a_g", 1024), ("b_q", 1024), ("b_f", 1024), ("b_i", 1024), ("b_g", 1024),
              ("c_q", 512), ("c_k", 512), ("c_v", 1024), ("c_r", 16), ("c_g", 1024),
              ("d_x", 1024), ("d_g", 1024), ("e_q", 1024), ("e_g", 1024))

TM_NORM = 512
TM_MM = 1088
TM_MERGE = 1088
TN_MERGE = 1024
TM_OUT = 544
TR_S5 = 544
TR_S5_EPI = 544
CW_S5_SCAN = 512
CHUNK_GLA = 128
NB_GLA_SAMPLE = 8
CHUNK_LRU = 128
ROWS_MEM_PROMPT = 512
NB_MEM_SAMPLE = 8


def _in_cols(w_in_l):
    cols, off = {}, 0
    for name, width in _IN_WIDTHS:
        cols[name] = w_in_l[:, off:off + width]
        off += width
    return cols


def kernel(x_prompt, x_sample, mem_prompt, state_s5_re, state_s5_im, state_hgrn, state_gla, state_rglru, state_conv, cache_mem_k, cache_mem_v, norm_pre, norm_post, w_in, s5_lam_re, s5_lam_im, s5_log_dt, s5_b_re, s5_b_im, s5_c_re, s5_c_im, s5_d, s5_w_glu, hg_lb_logits, hg_norm, gla_w_up, gla_b_up, gla_norm, conv_w, conv_b, lru_w_r, lru_b_r, lru_w_i, lru_b_i, lru_lam, mem_norm, w_mem_k, w_mem_v, w_branch, w_merge, w_out):
    d, w = D_MODEL, BRANCH_W
    m_p, m_s = BATCH * SEQ, DEC_BATCH * DEC_SEQ
    m_all = m_p + m_s
    bf = lambda a: a.astype(BF16)
    row = lambda v: v[None, :]

    x = jnp.concatenate([x_prompt.reshape(m_p, d), x_sample.reshape(m_s, d)], axis=0)

    mem2 = mem_prompt.reshape(BATCH * N_MEM, d)
    mk_l, mv_l = [], []
    for l in range(DEPTH):
        mn = rmsnorm_rows(mem2, row(mem_norm[l]), TM_NORM)
        mk_l.append(matmul(mn, bf(w_mem_k[l]), BATCH * N_MEM, w // 2))
        mv_l.append(matmul(mn, bf(w_mem_v[l]), BATCH * N_MEM, w // 2))
    cache_k4 = bf(cache_mem_k).reshape(DEPTH, DEC_BATCH, N_MEM, w)
    cache_v4 = bf(cache_mem_v).reshape(DEPTH, DEC_BATCH, N_MEM, w)

    lb = hg_lower_bounds(hg_lb_logits)
    s5_m, s5_wre, s5_wim, s5_vre, s5_vim = s5_prepare(s5_lam_re, s5_lam_im, s5_log_dt, s5_b_re, s5_b_im,
                                                      s5_c_re, s5_c_im)
    gs = S5_GROUPS * S5_STATE
    lam_re_row = s5_lam_re.reshape(DEPTH, 1, gs)
    lam_im_row = s5_lam_im.reshape(DEPTH, 1, gs)
    log_dt_row = jnp.broadcast_to(s5_log_dt[:, :, None], (DEPTH, S5_GROUPS, S5_STATE)).reshape(DEPTH, 1, gs)
    s5_h0r = state_s5_re.reshape(DEPTH, DEC_BATCH, gs)
    s5_h0i = state_s5_im.reshape(DEPTH, DEC_BATCH, gs)
    w_up_pad = jnp.zeros((DEPTH, V7X_LANES, GLA_KEY_W), F32).at[:, :GLA_RANK].set(gla_w_up)

    hg_p = hg_s = gl_p = gl_s = None
    outs = {k: [] for k in ("p_s5r", "p_s5i", "p_lru", "p_conv", "s_s5r", "s_s5i", "s_lru", "s_conv")}
    hg_cols = [(w, 0), (w, 1), (w, 2), (w, 3)]
    gl_cols = [(GLA_KEY_W, 4), (GLA_KEY_W, 5), (w, 0), (V7X_LANES, 3 * w // V7X_LANES), (w, 1)]
    for l in range(DEPTH):
        cols = _in_cols(w_in[l])
        w_a = bf(jnp.concatenate([cols["a_x"], cols["a_g"]], axis=1))
        w_b = bf(jnp.concatenate([cols["b_q"], cols["b_f"], cols["b_i"], cols["b_g"]], axis=1))
        w_c = bf(jnp.concatenate([cols["c_v"], cols["c_g"], cols["c_q"], cols["c_k"], cols["c_r"],
                                  jnp.zeros((d, V7X_LANES - GLA_RANK), F32)], axis=1))
        w_d = bf(jnp.concatenate([cols["d_x"], cols["d_g"]], axis=1))
        w_e = bf(jnp.concatenate([cols["e_q"], cols["e_g"]], axis=1))

        h = rmsnorm_rows(x, row(norm_pre[l]), TM_NORM)
        p_a = matmul(h, w_a, TM_MM, w)
        p_b = matmul(h, w_b, TM_MM, w)
        p_c = matmul(h, w_c, TM_MM, 5 * V7X_LANES)
        p_d = matmul(h, w_d, TM_MM, w)
        p_e = matmul(h, w_e, TM_MM, w)

        ere, eim = s5_chunk_inputs(p_a, s5_wre, s5_wim, l, TR_S5)
        hre, him, fpr, fpi, fsr, fsi = s5_scan(ere, eim, s5_h0r[l], s5_h0i[l], lam_re_row, lam_im_row, log_dt_row,
                                               l, n_seq=BATCH, rows_per_seq=SEQ // S5_CHUNK, cw=CW_S5_SCAN)
        y = s5_outputs(p_a, s5_m, hre, him, s5_vre, s5_vim, l, TR_S5)
        br = s5_epilogue(y, p_a, row(s5_d[l]), bf(s5_w_glu[l]), TR_S5_EPI)
        outs["p_s5r"].append(fpr); outs["p_s5i"].append(fpi)
        outs["s_s5r"].append(fsr); outs["s_s5i"].append(fsi)

        hg_consts = [row(lb[l]), row(hg_norm[l])]
        hg_kw = dict(layer=l, slab=1, heads=HG_HEADS, dk=HG_DK, dv=HG_DV)
        br, hg_p = gla_branch("hgrn", p_b, hg_cols, hg_consts, None, br, hg_p, row_base=0, n_seq=BATCH,
                              seq_total=SEQ, chunk=CHUNK_GLA, nb=1, **hg_kw)
        br, hg_s = gla_branch("hgrn", p_b, hg_cols, hg_consts, state_hgrn, br, hg_s, row_base=m_p,
                              n_seq=DEC_BATCH, seq_total=DEC_SEQ, chunk=DEC_SEQ, nb=NB_GLA_SAMPLE, **hg_kw)

        gl_consts = [w_up_pad[l], row(gla_b_up[l]), row(gla_norm[l])]
        gl_kw = dict(layer=l, slab=2, heads=GLA_HEADS, dk=GLA_DK, dv=GLA_DV)
        br, gl_p = gla_branch("gla", p_c, gl_cols, gl_consts, None, br, gl_p, row_base=0, n_seq=BATCH,
                              seq_total=SEQ, chunk=CHUNK_GLA, nb=1, **gl_kw)
        br, gl_s = gla_branch("gla", p_c, gl_cols, gl_consts, state_gla, br, gl_s, row_base=m_p,
                              n_seq=DEC_BATCH, seq_total=DEC_SEQ, chunk=DEC_SEQ, nb=NB_GLA_SAMPLE, **gl_kw)

        lru_consts = [conv_w[l], row(conv_b[l]), bf(lru_w_r[l]), row(lru_b_r[l]), bf(lru_w_i[l]), row(lru_b_i[l]),
                      row(lru_lam[l])]
        br, hfin_p, cfin_p = lru_prompt(p_d, lru_consts, br, slab=3, n_seq=BATCH, seq_total=SEQ, chunk=CHUNK_LRU)
        br, hfin_s, cfin_s = lru_sample(p_d, jnp.transpose(state_conv[l], (1, 0, 2)), state_rglru[l],
                                        lru_consts, br, slab=3, row_base=m_p)
        outs["p_lru"].append(hfin_p); outs["p_conv"].append(cfin_p)
        outs["s_lru"].append(hfin_s); outs["s_conv"].append(cfin_s)

        br = mem_attention(p_e, bf(mk_l[l]).reshape(1, BATCH, N_MEM, w), bf(mv_l[l]).reshape(1, BATCH, N_MEM, w), br,
                           layer=0, slab=4, row_base=0, n_seq=BATCH, seq_total=SEQ, rows_blk=ROWS_MEM_PROMPT, nbm=1)
        br = mem_attention(p_e, cache_k4, cache_v4, br, layer=l, slab=4, row_base=m_p, n_seq=DEC_BATCH,
                           seq_total=DEC_SEQ, rows_blk=NB_MEM_SAMPLE * DEC_SEQ, nbm=NB_MEM_SAMPLE)

        merged = merge_branches(h, br, bf(w_merge[l]), bf(w_branch[l]), TM_MERGE, TN_MERGE)
        x = out_proj_residual(merged, bf(w_out[l]), row(norm_post[l]), x, TM_OUT)

    st = lambda k, shape: jnp.stack(outs[k]).reshape(shape)
    s5_p_shape = (DEPTH, BATCH, S5_GROUPS, S5_STATE)
    s5_s_shape = (DEPTH, DEC_BATCH, S5_GROUPS, S5_STATE)
    mem_shape = (DEPTH, BATCH, N_MEM, MEM_HEADS, MEM_HD)
    return (x[:m_p].reshape(BATCH, SEQ, d), x[m_p:].reshape(DEC_BATCH, DEC_SEQ, d),
            st("p_s5r", s5_p_shape), st("p_s5i", s5_p_shape), hg_p, gl_p,
            st("p_lru", (DEPTH, BATCH, w)), st("p_conv", (DEPTH, BATCH, CONV_W - 1, w)),
            jnp.stack(mk_l).reshape(mem_shape), jnp.stack(mv_l).reshape(mem_shape),
            st("s_s5r", s5_s_shape), st("s_s5i", s5_s_shape), hg_s, gl_s,
            st("s_lru", (DEPTH, DEC_BATCH, w)), jnp.transpose(jnp.stack(outs["s_conv"]), (0, 2, 1, 3)))
```

```python
import functools
import math

import jax
import jax.numpy as jnp
from jax import lax
from jax.experimental import pallas as pl
from jax.experimental.pallas import tpu as pltpu

F32 = jnp.float32
BF16 = jnp.bfloat16
HIGHEST = lax.Precision.HIGHEST

V7X_LANES = 128
V7X_SUBLANES = 8
V7X_VMEM_LIMIT_BYTES = 56 * 1024 * 1024

EPS = 1e-6
D_MODEL = 2048
DEPTH = 4
BATCH = 4
SEQ = 2048
DEC_BATCH = 128
DEC_SEQ = 4
BRANCH_W = 1024
N_BRANCH = 5
S5_GROUP = 16
S5_GROUPS = 64
S5_STATE = 64
S5_CHUNK = 4
S5_GB = 8
HG_HEADS = 8
HG_DK = 128
HG_DV = 128
GLA_HEADS = 4
GLA_KEY_W = 512
GLA_DK = 128
GLA_DV = 256
GLA_RANK = 16
GLA_GATE_TEMP = 16.0
LRU_BLOCKS = 8
LRU_BW = 128
CONV_W = 4
LRU_C = 8.0
N_MEM = 256
MEM_HEADS = 4
MEM_HD = 256


def _params(*sem):
    return pltpu.CompilerParams(dimension_semantics=sem, vmem_limit_bytes=V7X_VMEM_LIMIT_BYTES)


def _dot(a, b):
    return jnp.dot(a, b, preferred_element_type=F32)


def _dot_nt(a, b):
    return lax.dot_general(a, b, (((1,), (1,)), ((), ())), preferred_element_type=F32)


def _dot_tn(a, b):
    return lax.dot_general(a, b, (((0,), (0,)), ((), ())), preferred_element_type=F32)


def _sigmoid(x):
    return 1.0 / (1.0 + jnp.exp(-x))


def _silu(x):
    return x * _sigmoid(x)


def _log1p_exp_neg_abs(x):
    return jnp.log(1.0 + jnp.exp(-jnp.abs(x)))


def _log_sigmoid(x):
    return jnp.minimum(x, 0.0) - _log1p_exp_neg_abs(x)


def _softplus(x):
    return jnp.maximum(x, 0.0) + _log1p_exp_neg_abs(x)


def _gelu_tanh(x):
    return 0.5 * x * (1.0 + jnp.tanh(math.sqrt(2.0 / math.pi) * (x + 0.044715 * (x * x * x))))


def _row_iota(shape):
    return lax.broadcasted_iota(jnp.int32, shape, 0)


def _col_iota(shape):
    return lax.broadcasted_iota(jnp.int32, shape, 1)


def _shift_rows(x, k):
    return pltpu.roll(x, k, 0)


def _shift_rows_up(x, k):
    return pltpu.roll(x, x.shape[0] - k, 0)


def _rmsnorm_kernel(x_ref, g_ref, o_ref):
    x = x_ref[...]
    y = x * lax.rsqrt(jnp.mean(x * x, axis=-1, keepdims=True) + EPS)
    o_ref[...] = (y * g_ref[...]).astype(o_ref.dtype)


def rmsnorm_rows(x, g, tm):
    m, d = x.shape
    return pl.pallas_call(
        _rmsnorm_kernel,
        out_shape=jax.ShapeDtypeStruct((m, d), BF16),
        grid=(m // tm,),
        in_specs=[pl.BlockSpec((tm, d), lambda i: (i, 0)), pl.BlockSpec((1, d), lambda i: (0, 0))],
        out_specs=pl.BlockSpec((tm, d), lambda i: (i, 0)),
        compiler_params=_params("parallel"),
        name="rmsnorm_rows",
    )(x, g)


def _matmul_kernel(a_ref, w_ref, o_ref):
    o_ref[...] = _dot(a_ref[...], w_ref[...]).astype(o_ref.dtype)


def matmul(a, w, tm, tn, out_dtype=F32):
    m, k = a.shape
    n = w.shape[1]
    return pl.pallas_call(
        _matmul_kernel,
        out_shape=jax.ShapeDtypeStruct((m, n), out_dtype),
        grid=(m // tm, n // tn),
        in_specs=[pl.BlockSpec((tm, k), lambda i, j: (i, 0)), pl.BlockSpec((k, tn), lambda i, j: (0, j))],
        out_specs=pl.BlockSpec((tm, tn), lambda i, j: (i, j)),
        compiler_params=_params("parallel", "parallel"),
        name="matmul",
    )(a, w)


def _merge_kernel(h_ref, b_ref, wm_ref, wb_ref, o_ref, acc_ref):
    c = pl.program_id(1)
    j = pl.program_id(2)
    nj, _, tn = acc_ref.shape
    contrib = _sigmoid(_dot(h_ref[...], wm_ref[...])) * _dot(b_ref[...], wb_ref[...])

    @pl.when(c == 0)
    def _():
        acc_ref[j] = contrib

    @pl.when(jnp.logical_and(c > 0, c < N_BRANCH - 1))
    def _():
        acc_ref[j] += contrib

    for jj in range(nj):
        @pl.when(jnp.logical_and(c == N_BRANCH - 1, j == jj))
        def _(jj=jj):
            o_ref[:, jj * tn:(jj + 1) * tn] = (acc_ref[jj] + contrib).astype(o_ref.dtype)


def merge_branches(h, branches, w_merge, w_branch, tm, tn):
    m, d = h.shape
    w = branches.shape[2]
    nj = d // tn
    return pl.pallas_call(
        _merge_kernel,
        out_shape=jax.ShapeDtypeStruct((m, d), BF16),
        grid=(m // tm, N_BRANCH, nj),
        in_specs=[pl.BlockSpec((tm, d), lambda i, c, j: (i, 0)),
                  pl.BlockSpec((None, tm, w), lambda i, c, j: (c, i, 0)),
                  pl.BlockSpec((d, tn), lambda i, c, j: (0, c * nj + j)),
                  pl.BlockSpec((None, w, tn), lambda i, c, j: (c, 0, j))],
        out_specs=pl.BlockSpec((tm, d), lambda i, c, j: (i, 0)),
        scratch_shapes=[pltpu.VMEM((nj, tm, tn), F32)],
        compiler_params=_params("parallel", "arbitrary", "arbitrary"),
        name="merge_branches",
    )(h, branches, w_merge, w_branch)


def _out_kernel(m_ref, w_ref, g_ref, x_ref, o_ref):
    y = _dot(m_ref[...], w_ref[...])
    y = y * lax.rsqrt(jnp.mean(y * y, axis=-1, keepdims=True) + EPS)
    o_ref[...] = x_ref[...] + y * g_ref[...]


def out_proj_residual(merged, w_out, g, x, tm):
    m, d = x.shape
    return pl.pallas_call(
        _out_kernel,
        out_shape=jax.ShapeDtypeStruct((m, d), F32),
        grid=(m // tm,),
        in_specs=[pl.BlockSpec((tm, d), lambda i: (i, 0)), pl.BlockSpec((d, d), lambda i: (0, 0)),
                  pl.BlockSpec((1, d), lambda i: (0, 0)), pl.BlockSpec((tm, d), lambda i: (i, 0))],
        out_specs=pl.BlockSpec((tm, d), lambda i: (i, 0)),
        compiler_params=_params("parallel"),
        name="out_proj_residual",
    )(merged, w_out, g, x)


def _hg_lb_kernel(logit_ref, lb_ref):
    x = logit_ref[...]
    e = jnp.exp(x - jnp.max(x, axis=0, keepdims=True))
    sm = e / jnp.sum(e, axis=0, keepdims=True)
    acc = jnp.zeros_like(sm[0:1])
    lb_ref[0:1, :] = acc
    for l in range(1, DEPTH):
        acc = acc + sm[l:l + 1]
        lb_ref[l:l + 1, :] = acc


def hg_lower_bounds(logits):
    return pl.pallas_call(
        _hg_lb_kernel,
        out_shape=jax.ShapeDtypeStruct(logits.shape, F32),
        name="hg_lower_bounds",
    )(logits)


def _gla_kernel(*refs, mode, seq_len, nb, heads, dk, dv, has_init):
    L = seq_len
    R = nb * L
    C = heads * dk
    if mode == "hgrn":
        q_ref, f_ref, v_ref, g_ref, lb_ref, gn_ref = refs[:6]
        rest = refs[6:]
    else:
        q_ref, k_ref, v_ref, r_ref, g_ref, wup_ref, bup_ref, gn_ref = refs[:8]
        rest = refs[8:]
    if has_init:
        s0_ref, rest = rest[0], rest[1:]
    o_ref, sout_ref, s_scr, att_scr, qe_scr, kd_scr, b_scr = rest

    c = pl.program_id(1)

    @pl.when(c == 0)
    def _():
        if has_init:
            s_scr[...] = s0_ref[...]
        else:
            s_scr[...] = jnp.zeros_like(s_scr)

    t = _row_iota((R, C)) % L
    if mode == "hgrn":
        lb = lb_ref[...]
        fz = f_ref[...]
        la = jnp.log(lb)
        lc = jnp.log1p(-lb) + _log_sigmoid(fz)
        g = jnp.maximum(la, lc) + _log1p_exp_neg_abs(la - lc)
        k = (1.0 - lb) * _sigmoid(-fz)
        q = q_ref[...]
    else:
        pre = jnp.dot(r_ref[...], wup_ref[...], preferred_element_type=F32, precision=HIGHEST)
        g = _log_sigmoid(pre + bup_ref[...]) * (1.0 / GLA_GATE_TEMP)
        k = k_ref[...]
        q = q_ref[...] * (dk ** -0.5)

    rr = _row_iota((R, R))
    cc = _col_iota((R, R))
    same_seq = (rr // L) == (cc // L)
    tri_incl = jnp.where(same_seq, jnp.where(cc <= rr, 1.0, 0.0), 0.0)
    tri_after = jnp.where(same_seq, jnp.where(cc > rr, 1.0, 0.0), 0.0)
    b = jnp.dot(tri_incl, g, preferred_element_type=F32, precision=HIGHEST)
    rb = jnp.dot(tri_after, g, preferred_element_type=F32, precision=HIGHEST)
    b_scr[...] = b
    qe_scr[...] = (q * jnp.exp(b)).astype(BF16)
    kd_scr[...] = (k * jnp.exp(rb)).astype(BF16)

    def add_level(level, qt, kt, mask):
        same = kt is qt
        qt = qt.astype(BF16)
        kt = qt if same else kt.astype(BF16)
        for h in range(heads):
            hs = slice(h * dk, (h + 1) * dk)
            p = jnp.where(mask, _dot_nt(qt[:, hs], kt[:, hs]), 0.0)
            if level == 0:
                att_scr[h] = p
            else:
                att_scr[h] += p

    add_level(0, q, k, rr == cc)
    level = 1
    m = 1
    while m < L:
        pos = t % (2 * m)
        upper = pos >= m
        if m == 1:
            d = jnp.where(upper, g, 0.0)
        elif m == 2:
            d = jnp.where(pos == 2, g,
                          jnp.where(pos == 3, g + _shift_rows(g, 1),
                                    jnp.where(pos == 0, _shift_rows_up(g, 1), 0.0)))
        else:
            b3 = b.reshape(R // (2 * m), 2 * m, C)
            d3 = b3 - b3[:, m - 1:m, :]
            d = -jnp.abs(d3.reshape(R, C))
        x = jnp.where(upper, q, k) * jnp.exp(d)
        blk = 2 * m
        pair = jnp.logical_and((rr // blk) == (cc // blk),
                               jnp.logical_and((rr % blk) >= m, (cc % blk) < m))
        add_level(level, x, x, pair)
        level += 1
        m *= 2

    gn = gn_ref[...]
    seq_o = _row_iota((R, dv)) // L
    seq_k = _row_iota((R, dk)) // L
    pad_rows = (-R) % V7X_LANES
    for h in range(heads):
        hs = slice(h * dk, (h + 1) * dk)
        vs = slice(h * dv, (h + 1) * dv)
        vh = v_ref[:, vs].astype(BF16)
        vh_p = jnp.concatenate([vh, jnp.zeros((pad_rows, dv), BF16)], axis=0) if pad_rows else vh
        o = _dot(att_scr[h].astype(BF16), vh)
        qe = qe_scr[:, hs]
        kd = kd_scr[:, hs]
        for j in range(nb):
            st = s_scr[j, h]
            inter = _dot(qe, st.astype(BF16))
            kd_j = kd
            if nb > 1:
                inter = jnp.where(seq_o == j, inter, 0.0)
                kd_j = jnp.where(seq_k == j, kd, jnp.zeros_like(kd))
            o = o + inter
            if pad_rows:
                kd_j = jnp.concatenate([kd_j, jnp.zeros((pad_rows, dk), BF16)], axis=0)
            upd = _dot_tn(kd_j, vh_p)
            dl = jnp.exp(b_scr[j * L + L - 1:j * L + L, hs])
            colb = jnp.broadcast_to(dl, (dk, dk)).T
            if dv != dk:
                colb = jnp.concatenate([colb] * (dv // dk), axis=1)
            s_scr[j, h] = colb * st + upd
        on = o * lax.rsqrt(jnp.mean(o * o, axis=-1, keepdims=True) + EPS) * gn
        o_ref[:, vs] = (on * _silu(g_ref[:, vs])).astype(o_ref.dtype)

    @pl.when(c == pl.num_programs(1) - 1)
    def _():
        sout_ref[...] = s_scr[...]


def gla_branch(mode, p, col_blocks, consts, s0, out_prev, sout_prev, *, layer, slab, row_base, n_seq, seq_total,
               chunk, nb, heads, dk, dv):
    L = chunk
    R = nb * L
    n_chunks = seq_total // L
    base_blk = row_base // R
    width = heads * dv

    def rows(i, c):
        return base_blk + i * n_chunks + c

    in_specs = [pl.BlockSpec((R, w), functools.partial(lambda i, c, bi: (rows(i, c), bi), bi=bi))
                for (w, bi) in col_blocks]
    args = [p] * len(col_blocks)
    for a in consts:
        in_specs.append(pl.BlockSpec(a.shape, lambda i, c: (0, 0)))
        args.append(a)
    has_init = s0 is not None
    state_spec = pl.BlockSpec((None, nb, heads, dk, dv), lambda i, c: (layer, i, 0, 0, 0))
    if has_init:
        in_specs.append(state_spec)
        args.append(s0)
    n_real = len(args)
    aliases = {}
    for out_idx, prev in enumerate((out_prev, sout_prev)):
        if prev is not None:
            aliases[len(args)] = out_idx
            in_specs.append(pl.BlockSpec(memory_space=pl.ANY))
            args.append(prev)
    n_in = len(args)

    def kern(*refs):
        _gla_kernel(*refs[:n_real], *refs[n_in:], mode=mode, seq_len=L, nb=nb, heads=heads, dk=dk, dv=dv,
                    has_init=has_init)

    return pl.pallas_call(
        kern,
        out_shape=(jax.ShapeDtypeStruct(out_prev.shape, BF16),
                   jax.ShapeDtypeStruct((DEPTH, n_seq, heads, dk, dv), F32)),
        grid=(n_seq // nb, n_chunks),
        in_specs=in_specs,
        out_specs=(pl.BlockSpec((None, R, width), lambda i, c: (slab, rows(i, c), 0)), state_spec),
        scratch_shapes=[pltpu.VMEM((nb, heads, dk, dv), F32), pltpu.VMEM((heads, R, R), F32),
                        pltpu.VMEM((R, heads * dk), BF16), pltpu.VMEM((R, heads * dk), BF16),
                        pltpu.VMEM((R, heads * dk), F32)],
        input_output_aliases=aliases,
        compiler_params=_params("parallel", "arbitrary"),
        name=mode + "_branch",
    )(*args)


def _lru_gates(xc, wr_ref, br_ref, wi_ref, bi_ref, lam_ref):
    xcb = xc.astype(BF16)
    r_parts, i_parts = [], []
    for blk in range(xc.shape[1] // LRU_BW):
        bs = slice(blk * LRU_BW, (blk + 1) * LRU_BW)
        r_parts.append(_dot(xcb[:, bs], wr_ref[blk]))
        i_parts.append(_dot(xcb[:, bs], wi_ref[blk]))
    r = _sigmoid(jnp.concatenate(r_parts, axis=1) + br_ref[...])
    ig = _sigmoid(jnp.concatenate(i_parts, axis=1) + bi_ref[...])
    log_a = (-LRU_C) * r * _softplus(-lam_ref[...])
    a = jnp.exp(log_a)
    u = jnp.sqrt(1.0 - jnp.exp(2.0 * log_a)) * (ig * xc)
    return a, u


def _lru_prompt_kernel(x_ref, g_ref, cw_ref, cb_ref, wr_ref, br_ref, wi_ref, bi_ref, lam_ref,
                       o_ref, hfin_ref, cfin_ref, tail_scr, hc_scr, a_scr, u_scr):
    tc, w = x_ref.shape
    sub = V7X_SUBLANES
    c = pl.program_id(1)

    @pl.when(c == 0)
    def _():
        tail_scr[...] = jnp.zeros_like(tail_scr)
        hc_scr[...] = jnp.zeros_like(hc_scr)

    x = x_ref[...]
    tail = tail_scr[...]
    row8 = _row_iota((sub, w))
    cw = cw_ref[...]
    xc = cb_ref[...] + cw[CONV_W - 1:CONV_W] * x
    for k in range(1, CONV_W):
        xs = _shift_rows(x, k)
        head = jnp.where(row8 < k, _shift_rows(tail, k), xs[0:sub])
        xs = jnp.concatenate([head, xs[sub:]], axis=0)
        xc = xc + cw[CONV_W - 1 - k:CONV_W - k] * xs
    last8 = x[tc - sub:tc]
    tail_scr[...] = last8

    a, u = _lru_gates(xc, wr_ref, br_ref, wi_ref, bi_ref, lam_ref)
    a_scr[...] = a
    u_scr[...] = u

    def body(i, carry):
        r0 = pl.multiple_of(i * sub, sub)
        at = a_scr[pl.ds(r0, sub), :]
        ut = u_scr[pl.ds(r0, sub), :]
        for k in (1, 2, 4):
            keep = row8 >= k
            ut = ut + at * jnp.where(keep, _shift_rows(ut, k), 0.0)
            at = at * jnp.where(keep, _shift_rows(at, k), 1.0)
        ht = ut + at * carry
        u_scr[pl.ds(r0, sub), :] = ht
        return jnp.broadcast_to(ht[sub - 1:sub], (sub, w))

    carry = lax.fori_loop(0, tc // sub, body, hc_scr[...])
    hc_scr[...] = carry
    o_ref[...] = (u_scr[...] * _silu(g_ref[...])).astype(o_ref.dtype)

    @pl.when(c == pl.num_programs(1) - 1)
    def _():
        hfin_ref[...] = carry[0:1]
        cfin_ref[...] = _shift_rows(last8, CONV_W - 1)[0:CONV_W - 1]


def lru_prompt(p_d, consts, out_prev, *, slab, n_seq, seq_total, chunk):
    w = BRANCH_W
    n_chunks = seq_total // chunk
    const_specs = [pl.BlockSpec(a.shape, functools.partial(lambda i, c, nd: (0,) * nd, nd=a.ndim)) for a in consts]

    def kern(*refs):
        n_in = 2 + len(consts)
        _lru_prompt_kernel(*refs[:n_in], *refs[n_in + 1:])

    return pl.pallas_call(
        kern,
        out_shape=(jax.ShapeDtypeStruct(out_prev.shape, BF16),
                   jax.ShapeDtypeStruct((n_seq, 1, w), F32),
                   jax.ShapeDtypeStruct((n_seq, CONV_W - 1, w), F32)),
        grid=(n_seq, n_chunks),
        in_specs=[pl.BlockSpec((chunk, w), lambda i, c: (i * n_chunks + c, 0)),
                  pl.BlockSpec((chunk, w), lambda i, c: (i * n_chunks + c, 1))] + const_specs
        + [pl.BlockSpec(memory_space=pl.ANY)],
        out_specs=(pl.BlockSpec((None, chunk, w), lambda i, c: (slab, i * n_chunks + c, 0)),
                   pl.BlockSpec((None, 1, w), lambda i, c: (i, 0, 0)),
                   pl.BlockSpec((None, CONV_W - 1, w), lambda i, c: (i, 0, 0))),
        scratch_shapes=[pltpu.VMEM((V7X_SUBLANES, w), F32), pltpu.VMEM((V7X_SUBLANES, w), F32),
                        pltpu.VMEM((chunk, w), F32), pltpu.VMEM((chunk, w), F32)],
        input_output_aliases={2 + len(consts): 0},
        compiler_params=_params("parallel", "arbitrary"),
        name="lru_prompt",
    )(p_d, p_d, *consts, out_prev)


def _lru_sample_kernel(x_ref, g_ref, buf_ref, h0_ref, cw_ref, cb_ref, wr_ref, br_ref, wi_ref, bi_ref, lam_ref,
                       prev_ref, o_ref, hfin_ref, cfin_ref, o_scr):
    del prev_ref
    steps = DEC_SEQ
    n = x_ref.shape[0] // steps
    step_rows = lambda t: pl.ds(t, n, stride=steps)
    xcat = [buf_ref[j] for j in range(CONV_W - 1)] + [x_ref[step_rows(t), :] for t in range(steps)]
    cw = cw_ref[...]
    xcs = []
    for t in range(steps):
        xc = cb_ref[...]
        for j in range(CONV_W):
            xc = xc + cw[j:j + 1] * xcat[t + j]
        xcs.append(xc)
    a, u = _lru_gates(jnp.concatenate(xcs, axis=0), wr_ref, br_ref, wi_ref, bi_ref, lam_ref)
    h = h0_ref[...]
    for t in range(steps):
        h = a[t * n:(t + 1) * n] * h + u[t * n:(t + 1) * n]
        o_scr[step_rows(t), :] = h * _silu(g_ref[step_rows(t), :])
    o_ref[...] = o_scr[...].astype(o_ref.dtype)
    hfin_ref[...] = h
    for j in range(CONV_W - 1):
        cfin_ref[j] = xcat[steps + j]


def lru_sample(p_d, conv_buf, h0, consts, out_prev, *, slab, row_base):
    w = BRANCH_W
    bw = LRU_BW
    n = h0.shape[0]
    rows = n * DEC_SEQ
    blk = row_base // rows
    nbk = w // bw
    conv_w_, conv_b_, w_r, b_r, w_i, b_i, lam = consts
    lane = lambda r: pl.BlockSpec((r, bw), lambda i: (0, i))
    wblk = pl.BlockSpec((1, bw, bw), lambda i: (i, 0, 0))
    tail = pl.BlockSpec((CONV_W - 1, n, bw), lambda i: (0, 0, i))
    return pl.pallas_call(
        _lru_sample_kernel,
        out_shape=(jax.ShapeDtypeStruct(out_prev.shape, BF16),
                   jax.ShapeDtypeStruct((n, w), F32),
                   jax.ShapeDtypeStruct((CONV_W - 1, n, w), F32)),
        grid=(nbk,),
        in_specs=[pl.BlockSpec((rows, bw), lambda i: (blk, i)), pl.BlockSpec((rows, bw), lambda i: (blk, nbk + i)),
                  tail, lane(n), lane(CONV_W), lane(1), wblk, lane(1), wblk, lane(1), lane(1),
                  pl.BlockSpec(memory_space=pl.ANY)],
        out_specs=(pl.BlockSpec((None, rows, bw), lambda i: (slab, blk, i)), lane(n), tail),
        scratch_shapes=[pltpu.VMEM((rows, bw), F32)],
        input_output_aliases={11: 0},
        compiler_params=_params("parallel"),
        name="lru_sample",
    )(p_d, p_d, conv_buf, h0, conv_w_, conv_b_, w_r, b_r, w_i, b_i, lam, out_prev)


def _mem_attn_kernel(q_ref, g_ref, k_ref, v_ref, prev_ref, o_ref, s_scr, p_scr, *, nbm, seq_rows):
    del prev_ref
    rows = q_ref.shape[0]
    q = (q_ref[...] * (MEM_HD ** -0.5)).astype(BF16)
    for h in range(MEM_HEADS):
        hs = slice(h * MEM_HD, (h + 1) * MEM_HD)
        for j in range(nbm):
            pair = h * nbm + j
            s_scr[pair * rows:(pair + 1) * rows, :] = _dot_nt(q[:, hs], k_ref[j, :, hs].astype(BF16))
    s = s_scr[...]
    e = jnp.exp(s - jnp.max(s, axis=-1, keepdims=True))
    p_scr[...] = (e / jnp.sum(e, axis=-1, keepdims=True)).astype(BF16)
    seq = _row_iota((rows, MEM_HD)) // seq_rows
    for h in range(MEM_HEADS):
        hs = slice(h * MEM_HD, (h + 1) * MEM_HD)
        acc = None
        for j in range(nbm):
            pair = h * nbm + j
            oh = _dot(p_scr[pair * rows:(pair + 1) * rows, :], v_ref[j, :, hs].astype(BF16))
            acc = oh if acc is None else jnp.where(seq == j, oh, acc)
        o_ref[:, hs] = (acc * _silu(g_ref[:, hs])).astype(o_ref.dtype)


def mem_attention(p_e, mem_k, mem_v, out_prev, *, layer, slab, row_base, n_seq, seq_total, rows_blk, nbm):
    w = BRANCH_W
    base_blk = row_base // rows_blk
    if nbm == 1:
        t_blocks = seq_total // rows_blk
        grid = (n_seq, t_blocks)
        rmap = lambda i, t: base_blk + i * t_blocks + t
    else:
        grid = (n_seq // nbm, 1)
        rmap = lambda i, t: base_blk + i
    mem_spec = pl.BlockSpec((None, nbm, N_MEM, w), lambda i, t: (layer, i, 0, 0))
    in_specs = [pl.BlockSpec((rows_blk, w), lambda i, t: (rmap(i, t), 0)),
                pl.BlockSpec((rows_blk, w), lambda i, t: (rmap(i, t), 1)),
                mem_spec, mem_spec, pl.BlockSpec(memory_space=pl.ANY)]
    args = [p_e, p_e, mem_k, mem_v, out_prev]
    aliases = {4: 0}
    return pl.pallas_call(
        functools.partial(_mem_attn_kernel, nbm=nbm, seq_rows=seq_total),
        out_shape=jax.ShapeDtypeStruct(out_prev.shape, BF16),
        grid=grid,
        in_specs=in_specs,
        out_specs=pl.BlockSpec((None, rows_blk, w), lambda i, t: (slab, rmap(i, t), 0)),
        scratch_shapes=[pltpu.VMEM((MEM_HEADS * nbm * rows_blk, N_MEM), F32),
                        pltpu.VMEM((MEM_HEADS * nbm * rows_blk, N_MEM), BF16)],
        input_output_aliases=aliases,
        compiler_params=_params("parallel", "arbitrary"),
        name="mem_attention",
    )(*args)


def _cmul(ar, ai, br, bi):
    return ar * br - ai * bi, ar * bi + ai * br


def _s5_abar(lr, li, log_dt):
    dt = jnp.exp(log_dt)
    mag = jnp.exp(lr * dt)
    return mag * jnp.cos(li * dt), mag * jnp.sin(li * dt)


def _s5_prep_kernel(lra, lia, dta, bra, bia, cfr, cfi, lrc, lic, dtc, ctr, cti,
                    m_ref, wre_ref, wim_ref, vre_ref, vim_ref):
    gi = S5_GB * S5_GROUP
    gp = S5_GB * S5_STATE
    lr, li = lra[...], lia[...]
    ar, ai = _s5_abar(lr, li, dta[...])
    den = lr * lr + li * li
    zr = ((ar - 1.0) * lr + ai * li) / den
    zi = (ai * lr - (ar - 1.0) * li) / den
    bbr, bbi = _cmul(zr, zi, bra[...], bia[...])
    wmask = (_row_iota((gi, gp)) // S5_GROUP) == (_col_iota((gi, gp)) // S5_STATE)
    mmask = (_row_iota((gi, gi)) // S5_GROUP) == (_col_iota((gi, gi)) // S5_GROUP)
    cr, ci = cfr[...], cfi[...]
    pr, pi = jnp.ones_like(ar), jnp.zeros_like(ar)
    for tau in range(S5_CHUNK):
        lrr, lii = _cmul(pr, pi, bbr, bbi)
        wre_ref[S5_CHUNK - 1 - tau] = jnp.where(wmask, lrr, 0.0).astype(BF16)
        wim_ref[S5_CHUNK - 1 - tau] = jnp.where(wmask, lii, 0.0).astype(BF16)
        x = (lax.dot_general(lrr[:, :S5_STATE], cr, (((1,), (1,)), ((), ())), precision=HIGHEST,
                             preferred_element_type=F32)
             - lax.dot_general(lii[:, :S5_STATE], ci, (((1,), (1,)), ((), ())), precision=HIGHEST,
                               preferred_element_type=F32))
        m_ref[tau] = jnp.where(mmask, x, 0.0).astype(BF16)
        pr, pi = _cmul(pr, pi, ar, ai)

    arc, aic = _s5_abar(lrc[...], lic[...], dtc[...])
    vmask = (_row_iota((gp, gi)) // S5_STATE) == (_col_iota((gp, gi)) // S5_GROUP)
    pr, pi = arc, aic
    for t in range(S5_CHUNK):
        vr, vi = _cmul(ctr[...], cti[...], pr, pi)
        vre_ref[t] = jnp.where(vmask, vr, 0.0).astype(BF16)
        vim_ref[t] = jnp.where(vmask, -vi, 0.0).astype(BF16)
        pr, pi = _cmul(pr, pi, arc, aic)


def s5_prepare(lam_re, lam_im, log_dt, b_re, b_im, c_re, c_im):
    dp, g, p = lam_re.shape
    i = S5_GROUP
    gi, gp = S5_GB * i, S5_GB * p
    nb = g // S5_GB

    def a_layout(x_gp):
        return jnp.broadcast_to(x_gp[:, :, None, None, :], (dp, g, i, S5_GB, p)).reshape(dp, g * i, gp)

    def c_layout(x_gp):
        return jnp.broadcast_to(x_gp[:, :, :, None], (dp, g, p, gi)).reshape(dp, g * p, gi)

    ldt = jnp.broadcast_to(log_dt[:, :, None], (dp, g, p))
    bt = lambda b: jnp.broadcast_to(jnp.transpose(b, (0, 1, 3, 2))[:, :, :, None, :],
                                    (dp, g, i, S5_GB, p)).reshape(dp, g * i, gp)
    ct = lambda c: jnp.broadcast_to(jnp.transpose(c, (0, 1, 3, 2))[:, :, :, None, :],
                                    (dp, g, p, S5_GB, i)).reshape(dp, g * p, gi)
    args = [a_layout(lam_re), a_layout(lam_im), a_layout(ldt), bt(b_re), bt(b_im),
            c_re.reshape(dp, g * i, p), c_im.reshape(dp, g * i, p),
            c_layout(lam_re), c_layout(lam_im), c_layout(ldt), ct(c_re), ct(c_im)]
    spec_a = pl.BlockSpec((None, gi, gp), lambda l, b: (l, b, 0))
    spec_f = pl.BlockSpec((None, gi, p), lambda l, b: (l, b, 0))
    spec_c = pl.BlockSpec((None, gp, gi), lambda l, b: (l, b, 0))
    out5 = lambda r, c: pl.BlockSpec((None, S5_CHUNK, None, r, c), lambda l, b: (l, 0, b, 0, 0))
    shp = lambda r, c: jax.ShapeDtypeStruct((dp, S5_CHUNK, nb, r, c), BF16)
    return pl.pallas_call(
        _s5_prep_kernel,
        out_shape=(shp(gi, gi), shp(gi, gp), shp(gi, gp), shp(gp, gi), shp(gp, gi)),
        grid=(dp, nb),
        in_specs=[spec_a] * 5 + [spec_f] * 2 + [spec_c] * 5,
        out_specs=(out5(gi, gi), out5(gi, gp), out5(gi, gp), out5(gp, gi), out5(gp, gi)),
        compiler_params=_params("parallel", "parallel"),
        name="s5_prepare",
    )(*args)


S5_GB_STEP = 1


def _s5_steps(u_ref):
    n = u_ref.shape[0] // S5_CHUNK
    return [u_ref[pl.ds(s, n, stride=S5_CHUNK), :].astype(BF16) for s in range(S5_CHUNK)]


def _s5_e_kernel(u_ref, wre_ref, wim_ref, ere_ref, eim_ref):
    gi, gp = S5_GB * S5_GROUP, S5_GB * S5_STATE
    us = _s5_steps(u_ref)
    for half in range(S5_GB_STEP):
        ucat = jnp.concatenate([u[:, half * gi:(half + 1) * gi] for u in us], axis=1)
        wre = jnp.concatenate([wre_ref[s, half] for s in range(S5_CHUNK)], axis=0)
        wim = jnp.concatenate([wim_ref[s, half] for s in range(S5_CHUNK)], axis=0)
        ere_ref[:, half * gp:(half + 1) * gp] = _dot(ucat, wre)
        eim_ref[:, half * gp:(half + 1) * gp] = _dot(ucat, wim)


def s5_chunk_inputs(p_a, wre, wim, layer, tr):
    rows = p_a.shape[0] // S5_CHUNK
    gi, gp = S5_GB * S5_GROUP, S5_GB * S5_STATE
    n_b = S5_GROUPS // (S5_GB * S5_GB_STEP)
    wspec = pl.BlockSpec((None, S5_CHUNK, S5_GB_STEP, gi, gp), lambda i, b: (layer, 0, b, 0, 0))
    ospec = pl.BlockSpec((tr, S5_GB_STEP * gp), lambda i, b: (i, b))
    oshape = jax.ShapeDtypeStruct((rows, S5_GROUPS * S5_STATE), F32)
    return pl.pallas_call(
        _s5_e_kernel,
        out_shape=(oshape, oshape),
        grid=(rows // tr, n_b),
        in_specs=[pl.BlockSpec((S5_CHUNK * tr, S5_GB_STEP * gi), lambda i, b: (i, b)), wspec, wspec],
        out_specs=(ospec, ospec),
        compiler_params=_params("parallel", "parallel"),
        name="s5_chunk_inputs",
    )(p_a, wre, wim)


def _s5_scan_kernel(ere, eim, h0r, h0i, lr_ref, li_ref, ldt_ref, hre, him, fpr, fpi, fsr, fsi,
                    *, n_seq, rows_per_seq):
    sub = V7X_SUBLANES
    cw = ere.shape[1]
    ar, ai = _s5_abar(lr_ref[...], li_ref[...], ldt_ref[...])
    a2 = _cmul(ar, ai, ar, ai)
    p1 = _cmul(*a2, *a2)
    p2 = _cmul(*p1, *p1)
    p4 = _cmul(*p2, *p2)
    p8 = _cmul(*p4, *p4)
    row8 = _row_iota((sub, cw))
    tr_, ti_ = jnp.ones((sub, cw), F32), jnp.zeros((sub, cw), F32)
    for bit, pw in ((1, p1), (2, p2), (4, p4)):
        nr, ni = _cmul(tr_, ti_, *pw)
        sel = (row8 & bit) != 0
        tr_, ti_ = jnp.where(sel, nr, tr_), jnp.where(sel, ni, ti_)

    base = n_seq * rows_per_seq
    n_s = h0r.shape[0]
    h0r_v, h0i_v = h0r[...], h0i[...]
    hre[base:base + n_s, :] = h0r_v
    him[base:base + n_s, :] = h0i_v
    dr, di = _cmul(p1[0], p1[1], h0r_v, h0i_v)
    fsr[...] = dr + ere[base:base + n_s, :]
    fsi[...] = di + eim[base:base + n_s, :]

    for n in range(n_seq):
        def body(j, carry, n=n):
            cr, ci = carry
            r0 = pl.multiple_of(n * rows_per_seq + j * sub, sub)
            xr = ere[pl.ds(r0, sub), :]
            xi = eim[pl.ds(r0, sub), :]
            for k, pw in ((1, p1), (2, p2), (4, p4)):
                keep = row8 >= k
                sr = jnp.where(keep, _shift_rows(xr, k), 0.0)
                si = jnp.where(keep, _shift_rows(xi, k), 0.0)
                mr, mi = _cmul(pw[0], pw[1], sr, si)
                xr, xi = xr + mr, xi + mi
            er = jnp.where(row8 >= 1, _shift_rows(xr, 1), 0.0)
            ei = jnp.where(row8 >= 1, _shift_rows(xi, 1), 0.0)
            qr, qi = _cmul(tr_, ti_, cr, ci)
            hre[pl.ds(r0, sub), :] = er + qr
            him[pl.ds(r0, sub), :] = ei + qi
            nr, ni = _cmul(p8[0], p8[1], cr, ci)
            return xr[sub - 1:sub] + nr, xi[sub - 1:sub] + ni

        zero = jnp.zeros((1, cw), F32)
        cr, ci = lax.fori_loop(0, rows_per_seq // sub, body, (zero, zero))
        fpr[n:n + 1, :] = cr
        fpi[n:n + 1, :] = ci


def s5_scan(ere, eim, h0r, h0i, lam_re_row, lam_im_row, log_dt_row, layer, *, n_seq, rows_per_seq, cw):
    rows, width = ere.shape
    n_s = h0r.shape[0]
    col = lambda r: pl.BlockSpec((r, cw), lambda j: (0, j))
    prow = pl.BlockSpec((None, 1, cw), lambda j: (layer, 0, j))
    shp = lambda r: jax.ShapeDtypeStruct((r, width), F32)
    return pl.pallas_call(
        functools.partial(_s5_scan_kernel, n_seq=n_seq, rows_per_seq=rows_per_seq),
        out_shape=(shp(rows), shp(rows), shp(n_seq), shp(n_seq), shp(n_s), shp(n_s)),
        grid=(width // cw,),
        in_specs=[col(rows), col(rows), col(n_s), col(n_s), prow, prow, prow],
        out_specs=(col(rows), col(rows), col(n_seq), col(n_seq), col(n_s), col(n_s)),
        compiler_params=_params("parallel"),
        name="s5_scan",
    )(ere, eim, h0r, h0i, lam_re_row, lam_im_row, log_dt_row)


def _s5_y_kernel(u_ref, m_ref, hre_ref, him_ref, vre_ref, vim_ref, y_ref):
    gi, gp = S5_GB * S5_GROUP, S5_GB * S5_STATE
    us = _s5_steps(u_ref)
    n = hre_ref.shape[0]
    for half in range(S5_GB_STEP):
        hs = slice(half * gp, (half + 1) * gp)
        ls = slice(half * gi, (half + 1) * gi)
        hr = hre_ref[:, hs].astype(BF16)
        hi = him_ref[:, hs].astype(BF16)
        zero_m = jnp.zeros((gi, gi), BF16)
        for t0 in range(0, S5_CHUNK, 2):
            t1 = t0 + 1
            lhs = jnp.concatenate([hr, hi] + [us[s][:, ls] for s in range(t1 + 1)], axis=1)
            rows = [jnp.concatenate([vre_ref[t0, half], vre_ref[t1, half]], axis=1),
                    jnp.concatenate([vim_ref[t0, half], vim_ref[t1, half]], axis=1)]
            for s in range(t1 + 1):
                left = m_ref[t0 - s, half] if s <= t0 else zero_m
                rows.append(jnp.concatenate([left, m_ref[t1 - s, half]], axis=1))
            acc = _dot(lhs, jnp.concatenate(rows, axis=0))
            y_ref[pl.ds(t0, n, stride=S5_CHUNK), ls] = acc[:, :gi]
            y_ref[pl.ds(t1, n, stride=S5_CHUNK), ls] = acc[:, gi:]


def s5_outputs(p_a, m, hre, him, vre, vim, layer, tr):
    rows = p_a.shape[0] // S5_CHUNK
    gi, gp = S5_GB * S5_GROUP, S5_GB * S5_STATE
    n_b = S5_GROUPS // (S5_GB * S5_GB_STEP)
    mspec = pl.BlockSpec((None, S5_CHUNK, S5_GB_STEP, gi, gi), lambda i, b: (layer, 0, b, 0, 0))
    vspec = pl.BlockSpec((None, S5_CHUNK, S5_GB_STEP, gp, gi), lambda i, b: (layer, 0, b, 0, 0))
    hspec = pl.BlockSpec((tr, S5_GB_STEP * gp), lambda i, b: (i, b))
    uspec = pl.BlockSpec((S5_CHUNK * tr, S5_GB_STEP * gi), lambda i, b: (i, b))
    return pl.pallas_call(
        _s5_y_kernel,
        out_shape=jax.ShapeDtypeStruct((p_a.shape[0], BRANCH_W), F32),
        grid=(rows // tr, n_b),
        in_specs=[uspec, mspec, hspec, hspec, vspec, vspec],
        out_specs=uspec,
        compiler_params=_params("parallel", "parallel"),
        name="s5_outputs",
    )(p_a, m, hre, him, vre, vim)


def _s5_epilogue_kernel(y_ref, u_ref, g_ref, d_ref, w_ref, o_ref):
    z = _gelu_tanh(y_ref[...] + d_ref[...] * u_ref[...])
    o = z * _sigmoid(_dot(z.astype(BF16), w_ref[...]))
    o_ref[...] = (o * _silu(g_ref[...])).astype(o_ref.dtype)


def s5_epilogue(y, p_a, d_skip, w_glu, tr):
    rows, w = y.shape
    return pl.pallas_call(
        _s5_epilogue_kernel,
        out_shape=jax.ShapeDtypeStruct((N_BRANCH, rows, w), BF16),
        grid=(rows // tr,),
        in_specs=[pl.BlockSpec((tr, w), lambda i: (i, 0)),
                  pl.BlockSpec((tr, w), lambda i: (i, 0)),
                  pl.BlockSpec((tr, w), lambda i: (i, 1)),
                  pl.BlockSpec((1, w), lambda i: (0, 0)),
                  pl.BlockSpec((w, w), lambda i: (0, 0))],
        out_specs=pl.BlockSpec((None, tr, w), lambda i: (0, i, 0)),
        compiler_params=_params("parallel"),
        name="s5_epilogue",
    )(y, p_a, p_a, d_skip, w_glu)


_IN_WIDTHS = (("a_x", 1024), ("a_g", 1024), ("b_q", 1024), ("b_f", 1024), ("b_i", 1024), ("b_g", 1024),
              ("c_q", 512), ("c_k", 512), ("c_v", 1024), ("c_r", 16), ("c_g", 1024),
              ("d_x", 1024), ("d_g", 1024), ("e_q", 1024), ("e_g", 1024))

TM_NORM = 512
TM_MM = 1088
TN_MM = 2048
TM_MERGE = 1088
TN_MERGE = 1024
TM_OUT = 544
TR_S5 = 544
TR_S5_EPI = 544
CW_S5_SCAN = 512
CHUNK_GLA = 128
NB_GLA_SAMPLE = 8
CHUNK_LRU = 128
ROWS_MEM_PROMPT = 512
NB_MEM_SAMPLE = 8


def _in_cols(w_in_l):
    cols, off = {}, 0
    for name, width in _IN_WIDTHS:
        cols[name] = w_in_l[:, off:off + width]
        off += width
    return cols


def kernel(x_prompt, x_sample, mem_prompt, state_s5_re, state_s5_im, state_hgrn, state_gla, state_rglru, state_conv, cache_mem_k, cache_mem_v, norm_pre, norm_post, w_in, s5_lam_re, s5_lam_im, s5_log_dt, s5_b_re, s5_b_im, s5_c_re, s5_c_im, s5_d, s5_w_glu, hg_lb_logits, hg_norm, gla_w_up, gla_b_up, gla_norm, conv_w, conv_b, lru_w_r, lru_b_r, lru_w_i, lru_b_i, lru_lam, mem_norm, w_mem_k, w_mem_v, w_branch, w_merge, w_out):
    d, w = D_MODEL, BRANCH_W
    m_p, m_s = BATCH * SEQ, DEC_BATCH * DEC_SEQ
    m_all = m_p + m_s
    bf = lambda a: a.astype(BF16)
    row = lambda v: v[None, :]

    x = jnp.concatenate([x_prompt.reshape(m_p, d), x_sample.reshape(m_s, d)], axis=0)

    mem2 = mem_prompt.reshape(BATCH * N_MEM, d)
    mk_l, mv_l = [], []
    for l in range(DEPTH):
        mn = rmsnorm_rows(mem2, row(mem_norm[l]), TM_NORM)
        mk_l.append(matmul(mn, bf(w_mem_k[l]), BATCH * N_MEM, w // 2))
        mv_l.append(matmul(mn, bf(w_mem_v[l]), BATCH * N_MEM, w // 2))
    cache_k4 = cache_mem_k.reshape(DEPTH, DEC_BATCH, N_MEM, w)
    cache_v4 = cache_mem_v.reshape(DEPTH, DEC_BATCH, N_MEM, w)

    lb = hg_lower_bounds(hg_lb_logits)
    s5_m, s5_wre, s5_wim, s5_vre, s5_vim = s5_prepare(s5_lam_re, s5_lam_im, s5_log_dt, s5_b_re, s5_b_im,
                                                      s5_c_re, s5_c_im)
    gs = S5_GROUPS * S5_STATE
    lam_re_row = s5_lam_re.reshape(DEPTH, 1, gs)
    lam_im_row = s5_lam_im.reshape(DEPTH, 1, gs)
    log_dt_row = jnp.broadcast_to(s5_log_dt[:, :, None], (DEPTH, S5_GROUPS, S5_STATE)).reshape(DEPTH, 1, gs)
    s5_h0r = state_s5_re.reshape(DEPTH, DEC_BATCH, gs)
    s5_h0i = state_s5_im.reshape(DEPTH, DEC_BATCH, gs)
    w_up_pad = jnp.zeros((DEPTH, V7X_LANES, GLA_KEY_W), F32).at[:, :GLA_RANK].set(gla_w_up)

    hg_p = hg_s = gl_p = gl_s = None
    outs = {k: [] for k in ("p_s5r", "p_s5i", "p_lru", "p_conv", "s_s5r", "s_s5i", "s_lru", "s_conv")}
    hg_cols = [(w, 0), (w, 1), (w, 2), (w, 3)]
    gl_cols = [(GLA_KEY_W, 4), (GLA_KEY_W, 5), (w, 0), (V7X_LANES, 3 * w // V7X_LANES), (w, 1)]
    for l in range(DEPTH):
        cols = _in_cols(w_in[l])
        w_a = bf(jnp.concatenate([cols["a_x"], cols["a_g"]], axis=1))
        w_b = bf(jnp.concatenate([cols["b_q"], cols["b_f"], cols["b_i"], cols["b_g"]], axis=1))
        w_c = bf(jnp.concatenate([cols["c_v"], cols["c_g"], cols["c_q"], cols["c_k"], cols["c_r"],
                                  jnp.zeros((d, V7X_LANES - GLA_RANK), F32)], axis=1))
        w_d = bf(jnp.concatenate([cols["d_x"], cols["d_g"]], axis=1))
        w_e = bf(jnp.concatenate([cols["e_q"], cols["e_g"]], axis=1))

        h = rmsnorm_rows(x, row(norm_pre[l]), TM_NORM)
        p_a = matmul(h, w_a, TM_MM, TN_MM)
        p_b = matmul(h, w_b, TM_MM, TN_MM)
        p_c = matmul(h, w_c, TM_MM, 5 * V7X_LANES)
        p_d = matmul(h, w_d, TM_MM, TN_MM)
        p_e = matmul(h, w_e, TM_MM, TN_MM)

        ere, eim = s5_chunk_inputs(p_a, s5_wre, s5_wim, l, TR_S5)
        hre, him, fpr, fpi, fsr, fsi = s5_scan(ere, eim, s5_h0r[l], s5_h0i[l], lam_re_row, lam_im_row, log_dt_row,
                                               l, n_seq=BATCH, rows_per_seq=SEQ // S5_CHUNK, cw=CW_S5_SCAN)
        y = s5_outputs(p_a, s5_m, hre, him, s5_vre, s5_vim, l, TR_S5)
        br = s5_epilogue(y, p_a, row(s5_d[l]), bf(s5_w_glu[l]), TR_S5_EPI)
        outs["p_s5r"].append(fpr); outs["p_s5i"].append(fpi)
        outs["s_s5r"].append(fsr); outs["s_s5i"].append(fsi)

        hg_consts = [row(lb[l]), row(hg_norm[l])]
        hg_kw = dict(layer=l, slab=1, heads=HG_HEADS, dk=HG_DK, dv=HG_DV)
        br, hg_p = gla_branch("hgrn", p_b, hg_cols, hg_consts, None, br, hg_p, row_base=0, n_seq=BATCH,
                              seq_total=SEQ, chunk=CHUNK_GLA, nb=1, **hg_kw)
        br, hg_s = gla_branch("hgrn", p_b, hg_cols, hg_consts, state_hgrn, br, hg_s, row_base=m_p,
                              n_seq=DEC_BATCH, seq_total=DEC_SEQ, chunk=DEC_SEQ, nb=NB_GLA_SAMPLE, **hg_kw)

        gl_consts = [w_up_pad[l], row(gla_b_up[l]), row(gla_norm[l])]
        gl_kw = dict(layer=l, slab=2, heads=GLA_HEADS, dk=GLA_DK, dv=GLA_DV)
        br, gl_p = gla_branch("gla", p_c, gl_cols, gl_consts, None, br, gl_p, row_base=0, n_seq=BATCH,
                              seq_total=SEQ, chunk=CHUNK_GLA, nb=1, **gl_kw)
        br, gl_s = gla_branch("gla", p_c, gl_cols, gl_consts, state_gla, br, gl_s, row_base=m_p,
                              n_seq=DEC_BATCH, seq_total=DEC_SEQ, chunk=DEC_SEQ, nb=NB_GLA_SAMPLE, **gl_kw)

        lru_consts = [conv_w[l], row(conv_b[l]), bf(lru_w_r[l]), row(lru_b_r[l]), bf(lru_w_i[l]), row(lru_b_i[l]),
                      row(lru_lam[l])]
        br, hfin_p, cfin_p = lru_prompt(p_d, lru_consts, br, slab=3, n_seq=BATCH, seq_total=SEQ, chunk=CHUNK_LRU)
        br, hfin_s, cfin_s = lru_sample(p_d, jnp.transpose(state_conv[l], (1, 0, 2)), state_rglru[l],
                                        lru_consts, br, slab=3, row_base=m_p)
        outs["p_lru"].append(hfin_p); outs["p_conv"].append(cfin_p)
        outs["s_lru"].append(hfin_s); outs["s_conv"].append(cfin_s)

        br = mem_attention(p_e, mk_l[l].reshape(1, BATCH, N_MEM, w), mv_l[l].reshape(1, BATCH, N_MEM, w), br,
                           layer=0, slab=4, row_base=0, n_seq=BATCH, seq_total=SEQ, rows_blk=ROWS_MEM_PROMPT, nbm=1)
        br = mem_attention(p_e, cache_k4, cache_v4, br, layer=l, slab=4, row_base=m_p, n_seq=DEC_BATCH,
                           seq_total=DEC_SEQ, rows_blk=NB_MEM_SAMPLE * DEC_SEQ, nbm=NB_MEM_SAMPLE)

        merged = merge_branches(h, br, bf(w_merge[l]), bf(w_branch[l]), TM_MERGE, TN_MERGE)
        x = out_proj_residual(merged, bf(w_out[l]), row(norm_post[l]), x, TM_OUT)

    st = lambda k, shape: jnp.stack(outs[k]).reshape(shape)
    s5_p_shape = (DEPTH, BATCH, S5_GROUPS, S5_STATE)
    s5_s_shape = (DEPTH, DEC_BATCH, S5_GROUPS, S5_STATE)
    mem_shape = (DEPTH, BATCH, N_MEM, MEM_HEADS, MEM_HD)
    return (x[:m_p].reshape(BATCH, SEQ, d), x[m_p:].reshape(DEC_BATCH, DEC_SEQ, d),
            st("p_s5r", s5_p_shape), st("p_s5i", s5_p_shape), hg_p, gl_p,
            st("p_lru", (DEPTH, BATCH, w)), st("p_conv", (DEPTH, BATCH, CONV_W - 1, w)),
            jnp.stack(mk_l).reshape(mem_shape), jnp.stack(mv_l).reshape(mem_shape),
            st("s_s5r", s5_s_shape), st("s_s5i", s5_s_shape), hg_s, gl_s,
            st("s_lru", (DEPTH, DEC_BATCH, w)), jnp.transpose(jnp.stack(outs["s_conv"]), (0, 2, 1, 3)))
```

```python
import functools
import math

import jax
import jax.numpy as jnp
from jax import lax
from jax.experimental import pallas as pl
from jax.experimental.pallas import tpu as pltpu

F32 = jnp.float32
BF16 = jnp.bfloat16
HIGHEST = lax.Precision.HIGHEST

V7X_LANES = 128
V7X_SUBLANES = 8
V7X_VMEM_LIMIT_BYTES = 56 * 1024 * 1024

EPS = 1e-6
D_MODEL = 2048
DEPTH = 4
BATCH = 4
SEQ = 2048
DEC_BATCH = 128
DEC_SEQ = 4
BRANCH_W = 1024
N_BRANCH = 5
S5_GROUP = 16
S5_GROUPS = 64
S5_STATE = 64
S5_CHUNK = 4
S5_GB = 8
HG_HEADS = 8
HG_DK = 128
HG_DV = 128
GLA_HEADS = 4
GLA_KEY_W = 512
GLA_DK = 128
GLA_DV = 256
GLA_RANK = 16
GLA_GATE_TEMP = 16.0
LRU_BLOCKS = 8
LRU_BW = 128
CONV_W = 4
LRU_C = 8.0
N_MEM = 256
MEM_HEADS = 4
MEM_HD = 256


def _params(*sem):
    return pltpu.CompilerParams(dimension_semantics=sem, vmem_limit_bytes=V7X_VMEM_LIMIT_BYTES)


def _dot(a, b):
    return jnp.dot(a, b, preferred_element_type=F32)


def _dot_nt(a, b):
    return lax.dot_general(a, b, (((1,), (1,)), ((), ())), preferred_element_type=F32)


def _dot_tn(a, b):
    return lax.dot_general(a, b, (((0,), (0,)), ((), ())), preferred_element_type=F32)


def _sigmoid(x):
    return 1.0 / (1.0 + jnp.exp(-x))


def _silu(x):
    return x * _sigmoid(x)


def _log1p_exp_neg_abs(x):
    return jnp.log(1.0 + jnp.exp(-jnp.abs(x)))


def _log_sigmoid(x):
    return jnp.minimum(x, 0.0) - _log1p_exp_neg_abs(x)


def _softplus(x):
    return jnp.maximum(x, 0.0) + _log1p_exp_neg_abs(x)


def _gelu_tanh(x):
    return 0.5 * x * (1.0 + jnp.tanh(math.sqrt(2.0 / math.pi) * (x + 0.044715 * (x * x * x))))


def _row_iota(shape):
    return lax.broadcasted_iota(jnp.int32, shape, 0)


def _col_iota(shape):
    return lax.broadcasted_iota(jnp.int32, shape, 1)


def _shift_rows(x, k):
    return pltpu.roll(x, k, 0)


def _shift_rows_up(x, k):
    return pltpu.roll(x, x.shape[0] - k, 0)


def _rmsnorm_kernel(x_ref, g_ref, o_ref):
    x = x_ref[...]
    y = x * lax.rsqrt(jnp.mean(x * x, axis=-1, keepdims=True) + EPS)
    o_ref[...] = (y * g_ref[...]).astype(o_ref.dtype)


def rmsnorm_rows(x, g, tm):
    m, d = x.shape
    return pl.pallas_call(
        _rmsnorm_kernel,
        out_shape=jax.ShapeDtypeStruct((m, d), BF16),
        grid=(m // tm,),
        in_specs=[pl.BlockSpec((tm, d), lambda i: (i, 0)), pl.BlockSpec((1, d), lambda i: (0, 0))],
        out_specs=pl.BlockSpec((tm, d), lambda i: (i, 0)),
        compiler_params=_params("parallel"),
        name="rmsnorm_rows",
    )(x, g)


def _matmul_kernel(a_ref, w_ref, o_ref):
    o_ref[...] = _dot(a_ref[...], w_ref[...]).astype(o_ref.dtype)


def matmul(a, w, tm, tn, out_dtype=F32):
    m, k = a.shape
    n = w.shape[1]
    return pl.pallas_call(
        _matmul_kernel,
        out_shape=jax.ShapeDtypeStruct((m, n), out_dtype),
        grid=(m // tm, n // tn),
        in_specs=[pl.BlockSpec((tm, k), lambda i, j: (i, 0)), pl.BlockSpec((k, tn), lambda i, j: (0, j))],
        out_specs=pl.BlockSpec((tm, tn), lambda i, j: (i, j)),
        compiler_params=_params("parallel", "parallel"),
        name="matmul",
    )(a, w)


def _merge_kernel(h_ref, b_ref, wm_ref, wb_ref, o_ref, acc_ref):
    c = pl.program_id(1)
    j = pl.program_id(2)
    nj, _, tn = acc_ref.shape
    contrib = _sigmoid(_dot(h_ref[...], wm_ref[...])) * _dot(b_ref[...], wb_ref[...])

    @pl.when(c == 0)
    def _():
        acc_ref[j] = contrib

    @pl.when(jnp.logical_and(c > 0, c < N_BRANCH - 1))
    def _():
        acc_ref[j] += contrib

    for jj in range(nj):
        @pl.when(jnp.logical_and(c == N_BRANCH - 1, j == jj))
        def _(jj=jj):
            o_ref[:, jj * tn:(jj + 1) * tn] = (acc_ref[jj] + contrib).astype(o_ref.dtype)


def merge_branches(h, branches, w_merge, w_branch, tm, tn):
    m, d = h.shape
    w = branches.shape[2]
    nj = d // tn
    return pl.pallas_call(
        _merge_kernel,
        out_shape=jax.ShapeDtypeStruct((m, d), BF16),
        grid=(m // tm, N_BRANCH, nj),
        in_specs=[pl.BlockSpec((tm, d), lambda i, c, j: (i, 0)),
                  pl.BlockSpec((None, tm, w), lambda i, c, j: (c, i, 0)),
                  pl.BlockSpec((d, tn), lambda i, c, j: (0, c * nj + j)),
                  pl.BlockSpec((None, w, tn), lambda i, c, j: (c, 0, j))],
        out_specs=pl.BlockSpec((tm, d), lambda i, c, j: (i, 0)),
        scratch_shapes=[pltpu.VMEM((nj, tm, tn), F32)],
        compiler_params=_params("parallel", "arbitrary", "arbitrary"),
        name="merge_branches",
    )(h, branches, w_merge, w_branch)


def _out_kernel(m_ref, w_ref, g_ref, x_ref, o_ref):
    y = _dot(m_ref[...], w_ref[...])
    y = y * lax.rsqrt(jnp.mean(y * y, axis=-1, keepdims=True) + EPS)
    o_ref[...] = x_ref[...] + y * g_ref[...]


def out_proj_residual(merged, w_out, g, x, tm):
    m, d = x.shape
    return pl.pallas_call(
        _out_kernel,
        out_shape=jax.ShapeDtypeStruct((m, d), F32),
        grid=(m // tm,),
        in_specs=[pl.BlockSpec((tm, d), lambda i: (i, 0)), pl.BlockSpec((d, d), lambda i: (0, 0)),
                  pl.BlockSpec((1, d), lambda i: (0, 0)), pl.BlockSpec((tm, d), lambda i: (i, 0))],
        out_specs=pl.BlockSpec((tm, d), lambda i: (i, 0)),
        compiler_params=_params("parallel"),
        name="out_proj_residual",
    )(merged, w_out, g, x)


def _hg_lb_kernel(logit_ref, lb_ref):
    x = logit_ref[...]
    e = jnp.exp(x - jnp.max(x, axis=0, keepdims=True))
    sm = e / jnp.sum(e, axis=0, keepdims=True)
    acc = jnp.zeros_like(sm[0:1])
    lb_ref[0:1, :] = acc
    for l in range(1, DEPTH):
        acc = acc + sm[l:l + 1]
        lb_ref[l:l + 1, :] = acc


def hg_lower_bounds(logits):
    return pl.pallas_call(
        _hg_lb_kernel,
        out_shape=jax.ShapeDtypeStruct(logits.shape, F32),
        name="hg_lower_bounds",
    )(logits)


def _gla_kernel(*refs, mode, seq_len, nb, heads, dk, dv, has_init):
    L = seq_len
    R = nb * L
    C = heads * dk
    if mode == "hgrn":
        q_ref, f_ref, v_ref, g_ref, lb_ref, gn_ref = refs[:6]
        rest = refs[6:]
    else:
        q_ref, k_ref, v_ref, r_ref, g_ref, wup_ref, bup_ref, gn_ref = refs[:8]
        rest = refs[8:]
    if has_init:
        s0_ref, rest = rest[0], rest[1:]
    o_ref, sout_ref, s_scr, att_scr, qe_scr, kd_scr, b_scr = rest

    c = pl.program_id(1)

    @pl.when(c == 0)
    def _():
        if has_init:
            s_scr[...] = s0_ref[...]
        else:
            s_scr[...] = jnp.zeros_like(s_scr)

    t = _row_iota((R, C)) % L
    if mode == "hgrn":
        lb = lb_ref[...]
        fz = f_ref[...]
        la = jnp.log(lb)
        lc = jnp.log1p(-lb) + _log_sigmoid(fz)
        g = jnp.maximum(la, lc) + _log1p_exp_neg_abs(la - lc)
        k = (1.0 - lb) * _sigmoid(-fz)
        q = q_ref[...]
    else:
        pre = jnp.dot(r_ref[...], wup_ref[...], preferred_element_type=F32, precision=HIGHEST)
        g = _log_sigmoid(pre + bup_ref[...]) * (1.0 / GLA_GATE_TEMP)
        k = k_ref[...]
        q = q_ref[...] * (dk ** -0.5)

    rr = _row_iota((R, R))
    cc = _col_iota((R, R))
    same_seq = (rr // L) == (cc // L)
    tri_incl = jnp.where(same_seq, jnp.where(cc <= rr, 1.0, 0.0), 0.0)
    tri_after = jnp.where(same_seq, jnp.where(cc > rr, 1.0, 0.0), 0.0)
    b = jnp.dot(tri_incl, g, preferred_element_type=F32, precision=HIGHEST)
    rb = jnp.dot(tri_after, g, preferred_element_type=F32, precision=HIGHEST)
    b_scr[...] = b
    qe_scr[...] = (q * jnp.exp(b)).astype(BF16)
    kd_scr[...] = (k * jnp.exp(rb)).astype(BF16)

    def add_level(level, qt, kt, mask):
        same = kt is qt
        qt = qt.astype(BF16)
        kt = qt if same else kt.astype(BF16)
        for h in range(heads):
            hs = slice(h * dk, (h + 1) * dk)
            p = jnp.where(mask, _dot_nt(qt[:, hs], kt[:, hs]), 0.0)
            if level == 0:
                att_scr[h] = p
            else:
                att_scr[h] += p

    add_level(0, q, k, rr == cc)
    level = 1
    m = 1
    while m < L:
        pos = t % (2 * m)
        upper = pos >= m
        if m == 1:
            d = jnp.where(upper, g, 0.0)
        elif m == 2:
            d = jnp.where(pos == 2, g,
                          jnp.where(pos == 3, g + _shift_rows(g, 1),
                                    jnp.where(pos == 0, _shift_rows_up(g, 1), 0.0)))
        else:
            b3 = b.reshape(R // (2 * m), 2 * m, C)
            d3 = b3 - b3[:, m - 1:m, :]
            d = -jnp.abs(d3.reshape(R, C))
        x = jnp.where(upper, q, k) * jnp.exp(d)
        blk = 2 * m
        pair = jnp.logical_and((rr // blk) == (cc // blk),
                               jnp.logical_and((rr % blk) >= m, (cc % blk) < m))
        add_level(level, x, x, pair)
        level += 1
        m *= 2

    gn = gn_ref[...]
    seq_o = _row_iota((R, dv)) // L
    seq_k = _row_iota((R, dk)) // L
    pad_rows = (-R) % V7X_LANES
    for h in range(heads):
        hs = slice(h * dk, (h + 1) * dk)
        vs = slice(h * dv, (h + 1) * dv)
        vh = v_ref[:, vs].astype(BF16)
        vh_p = jnp.concatenate([vh, jnp.zeros((pad_rows, dv), BF16)], axis=0) if pad_rows else vh
        o = _dot(att_scr[h].astype(BF16), vh)
        qe = qe_scr[:, hs]
        kd = kd_scr[:, hs]
        for j in range(nb):
            st = s_scr[j, h]
            inter = _dot(qe, st.astype(BF16))
            kd_j = kd
            if nb > 1:
                inter = jnp.where(seq_o == j, inter, 0.0)
                kd_j = jnp.where(seq_k == j, kd, jnp.zeros_like(kd))
            o = o + inter
            if pad_rows:
                kd_j = jnp.concatenate([kd_j, jnp.zeros((pad_rows, dk), BF16)], axis=0)
            upd = _dot_tn(kd_j, vh_p)
            dl = jnp.exp(b_scr[j * L + L - 1:j * L + L, hs])
            colb = jnp.broadcast_to(dl, (dk, dk)).T
            if dv != dk:
                colb = jnp.concatenate([colb] * (dv // dk), axis=1)
            s_scr[j, h] = colb * st + upd
        on = o * lax.rsqrt(jnp.mean(o * o, axis=-1, keepdims=True) + EPS) * gn
        o_ref[:, vs] = (on * _silu(g_ref[:, vs])).astype(o_ref.dtype)

    @pl.when(c == pl.num_programs(1) - 1)
    def _():
        sout_ref[...] = s_scr[...]


def gla_branch(mode, col_blocks, consts, s0, out_prev, sout_prev, *, layer, slab, row_base, n_seq, seq_total,
               chunk, nb, heads, dk, dv):
    L = chunk
    R = nb * L
    n_chunks = seq_total // L
    base_blk = row_base // R
    width = heads * dv

    def rows(i, c):
        return base_blk + i * n_chunks + c

    in_specs = [pl.BlockSpec((R, w), functools.partial(lambda i, c, bi: (rows(i, c), bi), bi=bi))
                for (_, w, bi) in col_blocks]
    args = [a for (a, _, _) in col_blocks]
    for a in consts:
        in_specs.append(pl.BlockSpec(a.shape, lambda i, c: (0, 0)))
        args.append(a)
    has_init = s0 is not None
    state_spec = pl.BlockSpec((None, nb, heads, dk, dv), lambda i, c: (layer, i, 0, 0, 0))
    if has_init:
        in_specs.append(state_spec)
        args.append(s0)
    n_real = len(args)
    aliases = {}
    for out_idx, prev in enumerate((out_prev, sout_prev)):
        if prev is not None:
            aliases[len(args)] = out_idx
            in_specs.append(pl.BlockSpec(memory_space=pl.ANY))
            args.append(prev)
    n_in = len(args)

    def kern(*refs):
        _gla_kernel(*refs[:n_real], *refs[n_in:], mode=mode, seq_len=L, nb=nb, heads=heads, dk=dk, dv=dv,
                    has_init=has_init)

    return pl.pallas_call(
        kern,
        out_shape=(jax.ShapeDtypeStruct(out_prev.shape, BF16),
                   jax.ShapeDtypeStruct((DEPTH, n_seq, heads, dk, dv), F32)),
        grid=(n_seq // nb, n_chunks),
        in_specs=in_specs,
        out_specs=(pl.BlockSpec((None, R, width), lambda i, c: (slab, rows(i, c), 0)), state_spec),
        scratch_shapes=[pltpu.VMEM((nb, heads, dk, dv), F32), pltpu.VMEM((heads, R, R), F32),
                        pltpu.VMEM((R, heads * dk), BF16), pltpu.VMEM((R, heads * dk), BF16),
                        pltpu.VMEM((R, heads * dk), F32)],
        input_output_aliases=aliases,
        compiler_params=_params("parallel", "arbitrary"),
        name=mode + "_branch",
    )(*args)


def _lru_gates(xc, wr_ref, br_ref, wi_ref, bi_ref, lam_ref):
    xcb = xc.astype(BF16)
    r_parts, i_parts = [], []
    for blk in range(xc.shape[1] // LRU_BW):
        bs = slice(blk * LRU_BW, (blk + 1) * LRU_BW)
        r_parts.append(_dot(xcb[:, bs], wr_ref[blk]))
        i_parts.append(_dot(xcb[:, bs], wi_ref[blk]))
    r = _sigmoid(jnp.concatenate(r_parts, axis=1) + br_ref[...])
    ig = _sigmoid(jnp.concatenate(i_parts, axis=1) + bi_ref[...])
    log_a = (-LRU_C) * r * _softplus(-lam_ref[...])
    a = jnp.exp(log_a)
    u = jnp.sqrt(1.0 - jnp.exp(2.0 * log_a)) * (ig * xc)
    return a, u


def _lru_prompt_kernel(x_ref, g_ref, cw_ref, cb_ref, wr_ref, br_ref, wi_ref, bi_ref, lam_ref,
                       o_ref, hfin_ref, cfin_ref, tail_scr, hc_scr, a_scr, u_scr):
    tc, w = x_ref.shape
    sub = V7X_SUBLANES
    c = pl.program_id(1)

    @pl.when(c == 0)
    def _():
        tail_scr[...] = jnp.zeros_like(tail_scr)
        hc_scr[...] = jnp.zeros_like(hc_scr)

    x = x_ref[...]
    tail = tail_scr[...]
    row8 = _row_iota((sub, w))
    cw = cw_ref[...]
    xc = cb_ref[...] + cw[CONV_W - 1:CONV_W] * x
    for k in range(1, CONV_W):
        xs = _shift_rows(x, k)
        head = jnp.where(row8 < k, _shift_rows(tail, k), xs[0:sub])
        xs = jnp.concatenate([head, xs[sub:]], axis=0)
        xc = xc + cw[CONV_W - 1 - k:CONV_W - k] * xs
    last8 = x[tc - sub:tc]
    tail_scr[...] = last8

    a, u = _lru_gates(xc, wr_ref, br_ref, wi_ref, bi_ref, lam_ref)
    a_scr[...] = a
    u_scr[...] = u

    def body(i, carry):
        r0 = pl.multiple_of(i * sub, sub)
        at = a_scr[pl.ds(r0, sub), :]
        ut = u_scr[pl.ds(r0, sub), :]
        for k in (1, 2, 4):
            keep = row8 >= k
            ut = ut + at * jnp.where(keep, _shift_rows(ut, k), 0.0)
            at = at * jnp.where(keep, _shift_rows(at, k), 1.0)
        ht = ut + at * carry
        u_scr[pl.ds(r0, sub), :] = ht
        return jnp.broadcast_to(ht[sub - 1:sub], (sub, w))

    carry = lax.fori_loop(0, tc // sub, body, hc_scr[...])
    hc_scr[...] = carry
    o_ref[...] = (u_scr[...] * _silu(g_ref[...])).astype(o_ref.dtype)

    @pl.when(c == pl.num_programs(1) - 1)
    def _():
        hfin_ref[...] = carry[0:1]
        cfin_ref[...] = _shift_rows(last8, CONV_W - 1)[0:CONV_W - 1]


def lru_prompt(p_d, consts, out_prev, *, slab, n_seq, seq_total, chunk):
    w = BRANCH_W
    n_chunks = seq_total // chunk
    const_specs = [pl.BlockSpec(a.shape, functools.partial(lambda i, c, nd: (0,) * nd, nd=a.ndim)) for a in consts]

    def kern(*refs):
        n_in = 2 + len(consts)
        _lru_prompt_kernel(*refs[:n_in], *refs[n_in + 1:])

    return pl.pallas_call(
        kern,
        out_shape=(jax.ShapeDtypeStruct(out_prev.shape, BF16),
                   jax.ShapeDtypeStruct((n_seq, 1, w), F32),
                   jax.ShapeDtypeStruct((n_seq, CONV_W - 1, w), F32)),
        grid=(n_seq, n_chunks),
        in_specs=[pl.BlockSpec((chunk, w), lambda i, c: (i * n_chunks + c, 0)),
                  pl.BlockSpec((chunk, w), lambda i, c: (i * n_chunks + c, 1))] + const_specs
        + [pl.BlockSpec(memory_space=pl.ANY)],
        out_specs=(pl.BlockSpec((None, chunk, w), lambda i, c: (slab, i * n_chunks + c, 0)),
                   pl.BlockSpec((None, 1, w), lambda i, c: (i, 0, 0)),
                   pl.BlockSpec((None, CONV_W - 1, w), lambda i, c: (i, 0, 0))),
        scratch_shapes=[pltpu.VMEM((V7X_SUBLANES, w), F32), pltpu.VMEM((V7X_SUBLANES, w), F32),
                        pltpu.VMEM((chunk, w), F32), pltpu.VMEM((chunk, w), F32)],
        input_output_aliases={2 + len(consts): 0},
        compiler_params=_params("parallel", "arbitrary"),
        name="lru_prompt",
    )(p_d, p_d, *consts, out_prev)


def _lru_sample_kernel(x_ref, g_ref, buf_ref, h0_ref, cw_ref, cb_ref, wr_ref, br_ref, wi_ref, bi_ref, lam_ref,
                       prev_ref, o_ref, hfin_ref, cfin_ref, o_scr):
    del prev_ref
    steps = DEC_SEQ
    n = x_ref.shape[0] // steps
    step_rows = lambda t: pl.ds(t, n, stride=steps)
    xcat = [buf_ref[j] for j in range(CONV_W - 1)] + [x_ref[step_rows(t), :] for t in range(steps)]
    cw = cw_ref[...]
    xcs = []
    for t in range(steps):
        xc = cb_ref[...]
        for j in range(CONV_W):
            xc = xc + cw[j:j + 1] * xcat[t + j]
        xcs.append(xc)
    a, u = _lru_gates(jnp.concatenate(xcs, axis=0), wr_ref, br_ref, wi_ref, bi_ref, lam_ref)
    h = h0_ref[...]
    for t in range(steps):
        h = a[t * n:(t + 1) * n] * h + u[t * n:(t + 1) * n]
        o_scr[step_rows(t), :] = h * _silu(g_ref[step_rows(t), :])
    o_ref[...] = o_scr[...].astype(o_ref.dtype)
    hfin_ref[...] = h
    for j in range(CONV_W - 1):
        cfin_ref[j] = xcat[steps + j]


def lru_sample(p_d, conv_buf, h0, consts, out_prev, *, slab, row_base):
    w = BRANCH_W
    bw = LRU_BW
    n = h0.shape[0]
    rows = n * DEC_SEQ
    blk = row_base // rows
    nbk = w // bw
    conv_w_, conv_b_, w_r, b_r, w_i, b_i, lam = consts
    lane = lambda r: pl.BlockSpec((r, bw), lambda i: (0, i))
    wblk = pl.BlockSpec((1, bw, bw), lambda i: (i, 0, 0))
    tail = pl.BlockSpec((CONV_W - 1, n, bw), lambda i: (0, 0, i))
    return pl.pallas_call(
        _lru_sample_kernel,
        out_shape=(jax.ShapeDtypeStruct(out_prev.shape, BF16),
                   jax.ShapeDtypeStruct((n, w), F32),
                   jax.ShapeDtypeStruct((CONV_W - 1, n, w), F32)),
        grid=(nbk,),
        in_specs=[pl.BlockSpec((rows, bw), lambda i: (blk, i)), pl.BlockSpec((rows, bw), lambda i: (blk, nbk + i)),
                  tail, lane(n), lane(CONV_W), lane(1), wblk, lane(1), wblk, lane(1), lane(1),
                  pl.BlockSpec(memory_space=pl.ANY)],
        out_specs=(pl.BlockSpec((None, rows, bw), lambda i: (slab, blk, i)), lane(n), tail),
        scratch_shapes=[pltpu.VMEM((rows, bw), F32)],
        input_output_aliases={11: 0},
        compiler_params=_params("parallel"),
        name="lru_sample",
    )(p_d, p_d, conv_buf, h0, conv_w_, conv_b_, w_r, b_r, w_i, b_i, lam, out_prev)


def _mem_attn_kernel(q_ref, g_ref, k_ref, v_ref, prev_ref, o_ref, s_scr, p_scr, *, nbm, seq_rows):
    del prev_ref
    rows = q_ref.shape[0]
    q = (q_ref[...] * (MEM_HD ** -0.5)).astype(BF16)
    for h in range(MEM_HEADS):
        hs = slice(h * MEM_HD, (h + 1) * MEM_HD)
        for j in range(nbm):
            pair = h * nbm + j
            s_scr[pair * rows:(pair + 1) * rows, :] = _dot_nt(q[:, hs], k_ref[j, :, hs].astype(BF16))
    s = s_scr[...]
    e = jnp.exp(s - jnp.max(s, axis=-1, keepdims=True))
    p_scr[...] = (e / jnp.sum(e, axis=-1, keepdims=True)).astype(BF16)
    seq = _row_iota((rows, MEM_HD)) // seq_rows
    for h in range(MEM_HEADS):
        hs = slice(h * MEM_HD, (h + 1) * MEM_HD)
        acc = None
        for j in range(nbm):
            pair = h * nbm + j
            oh = _dot(p_scr[pair * rows:(pair + 1) * rows, :], v_ref[j, :, hs].astype(BF16))
            acc = oh if acc is None else jnp.where(seq == j, oh, acc)
        o_ref[:, hs] = (acc * _silu(g_ref[:, hs])).astype(o_ref.dtype)


def mem_attention(p_e, mem_k, mem_v, out_prev, *, layer, slab, row_base, n_seq, seq_total, rows_blk, nbm):
    w = BRANCH_W
    base_blk = row_base // rows_blk
    if nbm == 1:
        t_blocks = seq_total // rows_blk
        grid = (n_seq, t_blocks)
        rmap = lambda i, t: base_blk + i * t_blocks + t
    else:
        grid = (n_seq // nbm, 1)
        rmap = lambda i, t: base_blk + i
    mem_spec = pl.BlockSpec((None, nbm, N_MEM, w), lambda i, t: (layer, i, 0, 0))
    in_specs = [pl.BlockSpec((rows_blk, w), lambda i, t: (rmap(i, t), 0)),
                pl.BlockSpec((rows_blk, w), lambda i, t: (rmap(i, t), 1)),
                mem_spec, mem_spec, pl.BlockSpec(memory_space=pl.ANY)]
    args = [p_e, p_e, mem_k, mem_v, out_prev]
    aliases = {4: 0}
    return pl.pallas_call(
        functools.partial(_mem_attn_kernel, nbm=nbm, seq_rows=seq_total),
        out_shape=jax.ShapeDtypeStruct(out_prev.shape, BF16),
        grid=grid,
        in_specs=in_specs,
        out_specs=pl.BlockSpec((None, rows_blk, w), lambda i, t: (slab, rmap(i, t), 0)),
        scratch_shapes=[pltpu.VMEM((MEM_HEADS * nbm * rows_blk, N_MEM), F32),
                        pltpu.VMEM((MEM_HEADS * nbm * rows_blk, N_MEM), BF16)],
        input_output_aliases=aliases,
        compiler_params=_params("parallel", "arbitrary"),
        name="mem_attention",
    )(*args)


def _mem_attn_cache_kernel(q_ref, g_ref, k_hbm, v_hbm, prev_ref, o_ref, kbuf, vbuf, sems, s_scr, p_scr,
                           *, layer, nbm, seq_rows):
    del prev_ref
    i = pl.program_id(0)
    n_steps = pl.num_programs(0)

    def slab_copies(step, slot):
        out = []
        for h in range(MEM_HEADS):
            for which, (src, dst) in enumerate(((k_hbm, kbuf), (v_hbm, vbuf))):
                out.append(pltpu.make_async_copy(src.at[layer, pl.ds(step * nbm, nbm), :, h, :],
                                                 dst.at[slot, h], sems.at[slot, which, h]))
        return out

    slot = i % 2

    @pl.when(i == 0)
    def _():
        for c in slab_copies(0, 0):
            c.start()

    @pl.when(i + 1 < n_steps)
    def _():
        for c in slab_copies(i + 1, 1 - slot):
            c.start()

    for c in slab_copies(i, slot):
        c.wait()

    rows = q_ref.shape[0]
    q = (q_ref[...] * (MEM_HD ** -0.5)).astype(BF16)
    for h in range(MEM_HEADS):
        hs = slice(h * MEM_HD, (h + 1) * MEM_HD)
        for j in range(nbm):
            pair = h * nbm + j
            s_scr[pair * rows:(pair + 1) * rows, :] = _dot_nt(q[:, hs], kbuf[slot, h, j].astype(BF16))
    s = s_scr[...]
    e = jnp.exp(s - jnp.max(s, axis=-1, keepdims=True))
    p_scr[...] = (e / jnp.sum(e, axis=-1, keepdims=True)).astype(BF16)
    seq = _row_iota((rows, MEM_HD)) // seq_rows
    for h in range(MEM_HEADS):
        hs = slice(h * MEM_HD, (h + 1) * MEM_HD)
        acc = None
        for j in range(nbm):
            pair = h * nbm + j
            oh = _dot(p_scr[pair * rows:(pair + 1) * rows, :], vbuf[slot, h, j].astype(BF16))
            acc = oh if acc is None else jnp.where(seq == j, oh, acc)
        o_ref[:, hs] = (acc * _silu(g_ref[:, hs])).astype(o_ref.dtype)


def mem_attention_cache(p_e, cache_k, cache_v, out_prev, *, layer, slab, row_base, n_seq, seq_total, nbm):
    w = BRANCH_W
    rows_blk = nbm * seq_total
    base_blk = row_base // rows_blk
    buf = pltpu.VMEM((2, MEM_HEADS, nbm, N_MEM, MEM_HD), F32)
    return pl.pallas_call(
        functools.partial(_mem_attn_cache_kernel, layer=layer, nbm=nbm, seq_rows=seq_total),
        out_shape=jax.ShapeDtypeStruct(out_prev.shape, BF16),
        grid=(n_seq // nbm,),
        in_specs=[pl.BlockSpec((rows_blk, w), lambda i: (base_blk + i, 0)),
                  pl.BlockSpec((rows_blk, w), lambda i: (base_blk + i, 1)),
                  pl.BlockSpec(memory_space=pl.ANY), pl.BlockSpec(memory_space=pl.ANY),
                  pl.BlockSpec(memory_space=pl.ANY)],
        out_specs=pl.BlockSpec((None, rows_blk, w), lambda i: (slab, base_blk + i, 0)),
        scratch_shapes=[buf, buf, pltpu.SemaphoreType.DMA((2, 2, MEM_HEADS)),
                        pltpu.VMEM((MEM_HEADS * nbm * rows_blk, N_MEM), F32),
                        pltpu.VMEM((MEM_HEADS * nbm * rows_blk, N_MEM), BF16)],
        input_output_aliases={4: 0},
        compiler_params=_params("arbitrary"),
        name="mem_attention_cache",
    )(p_e, p_e, cache_k, cache_v, out_prev)


def _cmul(ar, ai, br, bi):
    return ar * br - ai * bi, ar * bi + ai * br


def _s5_abar(lr, li, log_dt):
    dt = jnp.exp(log_dt)
    mag = jnp.exp(lr * dt)
    return mag * jnp.cos(li * dt), mag * jnp.sin(li * dt)


def _s5_prep_kernel(lra, lia, dta, bra, bia, cfr, cfi, lrc, lic, dtc, ctr, cti,
                    m_ref, wre_ref, wim_ref, vre_ref, vim_ref):
    gi = S5_GB * S5_GROUP
    gp = S5_GB * S5_STATE
    lr, li = lra[...], lia[...]
    ar, ai = _s5_abar(lr, li, dta[...])
    den = lr * lr + li * li
    zr = ((ar - 1.0) * lr + ai * li) / den
    zi = (ai * lr - (ar - 1.0) * li) / den
    bbr, bbi = _cmul(zr, zi, bra[...], bia[...])
    wmask = (_row_iota((gi, gp)) // S5_GROUP) == (_col_iota((gi, gp)) // S5_STATE)
    mmask = (_row_iota((gi, gi)) // S5_GROUP) == (_col_iota((gi, gi)) // S5_GROUP)
    cr, ci = cfr[...], cfi[...]
    pr, pi = jnp.ones_like(ar), jnp.zeros_like(ar)
    for tau in range(S5_CHUNK):
        lrr, lii = _cmul(pr, pi, bbr, bbi)
        wre_ref[S5_CHUNK - 1 - tau] = jnp.where(wmask, lrr, 0.0).astype(BF16)
        wim_ref[S5_CHUNK - 1 - tau] = jnp.where(wmask, lii, 0.0).astype(BF16)
        x = (lax.dot_general(lrr[:, :S5_STATE], cr, (((1,), (1,)), ((), ())), precision=HIGHEST,
                             preferred_element_type=F32)
             - lax.dot_general(lii[:, :S5_STATE], ci, (((1,), (1,)), ((), ())), precision=HIGHEST,
                               preferred_element_type=F32))
        m_ref[tau] = jnp.where(mmask, x, 0.0).astype(BF16)
        pr, pi = _cmul(pr, pi, ar, ai)

    arc, aic = _s5_abar(lrc[...], lic[...], dtc[...])
    vmask = (_row_iota((gp, gi)) // S5_STATE) == (_col_iota((gp, gi)) // S5_GROUP)
    pr, pi = arc, aic
    for t in range(S5_CHUNK):
        vr, vi = _cmul(ctr[...], cti[...], pr, pi)
        vre_ref[t] = jnp.where(vmask, vr, 0.0).astype(BF16)
        vim_ref[t] = jnp.where(vmask, -vi, 0.0).astype(BF16)
        pr, pi = _cmul(pr, pi, arc, aic)


def s5_prepare(lam_re, lam_im, log_dt, b_re, b_im, c_re, c_im):
    dp, g, p = lam_re.shape
    i = S5_GROUP
    gi, gp = S5_GB * i, S5_GB * p
    nb = g // S5_GB

    def a_layout(x_gp):
        return jnp.broadcast_to(x_gp[:, :, None, None, :], (dp, g, i, S5_GB, p)).reshape(dp, g * i, gp)

    def c_layout(x_gp):
        return jnp.broadcast_to(x_gp[:, :, :, None], (dp, g, p, gi)).reshape(dp, g * p, gi)

    ldt = jnp.broadcast_to(log_dt[:, :, None], (dp, g, p))
    bt = lambda b: jnp.broadcast_to(jnp.transpose(b, (0, 1, 3, 2))[:, :, :, None, :],
                                    (dp, g, i, S5_GB, p)).reshape(dp, g * i, gp)
    ct = lambda c: jnp.broadcast_to(jnp.transpose(c, (0, 1, 3, 2))[:, :, :, None, :],
                                    (dp, g, p, S5_GB, i)).reshape(dp, g * p, gi)
    args = [a_layout(lam_re), a_layout(lam_im), a_layout(ldt), bt(b_re), bt(b_im),
            c_re.reshape(dp, g * i, p), c_im.reshape(dp, g * i, p),
            c_layout(lam_re), c_layout(lam_im), c_layout(ldt), ct(c_re), ct(c_im)]
    spec_a = pl.BlockSpec((None, gi, gp), lambda l, b: (l, b, 0))
    spec_f = pl.BlockSpec((None, gi, p), lambda l, b: (l, b, 0))
    spec_c = pl.BlockSpec((None, gp, gi), lambda l, b: (l, b, 0))
    out5 = lambda r, c: pl.BlockSpec((None, S5_CHUNK, None, r, c), lambda l, b: (l, 0, b, 0, 0))
    shp = lambda r, c: jax.ShapeDtypeStruct((dp, S5_CHUNK, nb, r, c), BF16)
    return pl.pallas_call(
        _s5_prep_kernel,
        out_shape=(shp(gi, gi), shp(gi, gp), shp(gi, gp), shp(gp, gi), shp(gp, gi)),
        grid=(dp, nb),
        in_specs=[spec_a] * 5 + [spec_f] * 2 + [spec_c] * 5,
        out_specs=(out5(gi, gi), out5(gi, gp), out5(gi, gp), out5(gp, gi), out5(gp, gi)),
        compiler_params=_params("parallel", "parallel"),
        name="s5_prepare",
    )(*args)


S5_GB_STEP = 1


def _s5_steps(u_ref):
    n = u_ref.shape[0] // S5_CHUNK
    return [u_ref[pl.ds(s, n, stride=S5_CHUNK), :].astype(BF16) for s in range(S5_CHUNK)]


def _s5_e_kernel(u_ref, wre_ref, wim_ref, ere_ref, eim_ref):
    gi, gp = S5_GB * S5_GROUP, S5_GB * S5_STATE
    us = _s5_steps(u_ref)
    for half in range(S5_GB_STEP):
        ucat = jnp.concatenate([u[:, half * gi:(half + 1) * gi] for u in us], axis=1)
        wre = jnp.concatenate([wre_ref[s, half] for s in range(S5_CHUNK)], axis=0)
        wim = jnp.concatenate([wim_ref[s, half] for s in range(S5_CHUNK)], axis=0)
        ere_ref[:, half * gp:(half + 1) * gp] = _dot(ucat, wre)
        eim_ref[:, half * gp:(half + 1) * gp] = _dot(ucat, wim)


def s5_chunk_inputs(p_a, wre, wim, layer, tr):
    rows = p_a.shape[0] // S5_CHUNK
    gi, gp = S5_GB * S5_GROUP, S5_GB * S5_STATE
    n_b = S5_GROUPS // (S5_GB * S5_GB_STEP)
    wspec = pl.BlockSpec((None, S5_CHUNK, S5_GB_STEP, gi, gp), lambda i, b: (layer, 0, b, 0, 0))
    ospec = pl.BlockSpec((tr, S5_GB_STEP * gp), lambda i, b: (i, b))
    oshape = jax.ShapeDtypeStruct((rows, S5_GROUPS * S5_STATE), F32)
    return pl.pallas_call(
        _s5_e_kernel,
        out_shape=(oshape, oshape),
        grid=(rows // tr, n_b),
        in_specs=[pl.BlockSpec((S5_CHUNK * tr, S5_GB_STEP * gi), lambda i, b: (i, b)), wspec, wspec],
        out_specs=(ospec, ospec),
        compiler_params=_params("parallel", "parallel"),
        name="s5_chunk_inputs",
    )(p_a, wre, wim)


def _s5_scan_kernel(ere, eim, h0r, h0i, lr_ref, li_ref, ldt_ref, hre, him, fpr, fpi, fsr, fsi,
                    *, n_seq, rows_per_seq):
    sub = V7X_SUBLANES
    cw = ere.shape[1]
    ar, ai = _s5_abar(lr_ref[...], li_ref[...], ldt_ref[...])
    a2 = _cmul(ar, ai, ar, ai)
    p1 = _cmul(*a2, *a2)
    p2 = _cmul(*p1, *p1)
    p4 = _cmul(*p2, *p2)
    p8 = _cmul(*p4, *p4)
    row8 = _row_iota((sub, cw))
    tr_, ti_ = jnp.ones((sub, cw), F32), jnp.zeros((sub, cw), F32)
    for bit, pw in ((1, p1), (2, p2), (4, p4)):
        nr, ni = _cmul(tr_, ti_, *pw)
        sel = (row8 & bit) != 0
        tr_, ti_ = jnp.where(sel, nr, tr_), jnp.where(sel, ni, ti_)

    base = n_seq * rows_per_seq
    n_s = h0r.shape[0]
    h0r_v, h0i_v = h0r[...], h0i[...]
    hre[base:base + n_s, :] = h0r_v
    him[base:base + n_s, :] = h0i_v
    dr, di = _cmul(p1[0], p1[1], h0r_v, h0i_v)
    fsr[...] = dr + ere[base:base + n_s, :]
    fsi[...] = di + eim[base:base + n_s, :]

    for n in range(n_seq):
        def body(j, carry, n=n):
            cr, ci = carry
            r0 = pl.multiple_of(n * rows_per_seq + j * sub, sub)
            xr = ere[pl.ds(r0, sub), :]
            xi = eim[pl.ds(r0, sub), :]
            for k, pw in ((1, p1), (2, p2), (4, p4)):
                keep = row8 >= k
                sr = jnp.where(keep, _shift_rows(xr, k), 0.0)
                si = jnp.where(keep, _shift_rows(xi, k), 0.0)
                mr, mi = _cmul(pw[0], pw[1], sr, si)
                xr, xi = xr + mr, xi + mi
            er = jnp.where(row8 >= 1, _shift_rows(xr, 1), 0.0)
            ei = jnp.where(row8 >= 1, _shift_rows(xi, 1), 0.0)
            qr, qi = _cmul(tr_, ti_, cr, ci)
            hre[pl.ds(r0, sub), :] = er + qr
            him[pl.ds(r0, sub), :] = ei + qi
            nr, ni = _cmul(p8[0], p8[1], cr, ci)
            return xr[sub - 1:sub] + nr, xi[sub - 1:sub] + ni

        zero = jnp.zeros((1, cw), F32)
        cr, ci = lax.fori_loop(0, rows_per_seq // sub, body, (zero, zero))
        fpr[n:n + 1, :] = cr
        fpi[n:n + 1, :] = ci


def s5_scan(ere, eim, h0r, h0i, lam_re_row, lam_im_row, log_dt_row, layer, *, n_seq, rows_per_seq, cw):
    rows, width = ere.shape
    n_s = h0r.shape[0]
    col = lambda r: pl.BlockSpec((r, cw), lambda j: (0, j))
    prow = pl.BlockSpec((None, 1, cw), lambda j: (layer, 0, j))
    shp = lambda r: jax.ShapeDtypeStruct((r, width), F32)
    return pl.pallas_call(
        functools.partial(_s5_scan_kernel, n_seq=n_seq, rows_per_seq=rows_per_seq),
        out_shape=(shp(rows), shp(rows), shp(n_seq), shp(n_seq), shp(n_s), shp(n_s)),
        grid=(width // cw,),
        in_specs=[col(rows), col(rows), col(n_s), col(n_s), prow, prow, prow],
        out_specs=(col(rows), col(rows), col(n_seq), col(n_seq), col(n_s), col(n_s)),
        compiler_params=_params("parallel"),
        name="s5_scan",
    )(ere, eim, h0r, h0i, lam_re_row, lam_im_row, log_dt_row)


def _s5_y_kernel(u_ref, m_ref, hre_ref, him_ref, vre_ref, vim_ref, y_ref):
    gi, gp = S5_GB * S5_GROUP, S5_GB * S5_STATE
    us = _s5_steps(u_ref)
    n = hre_ref.shape[0]
    for half in range(S5_GB_STEP):
        hs = slice(half * gp, (half + 1) * gp)
        ls = slice(half * gi, (half + 1) * gi)
        hr = hre_ref[:, hs].astype(BF16)
        hi = him_ref[:, hs].astype(BF16)
        zero_m = jnp.zeros((gi, gi), BF16)
        for t0 in range(0, S5_CHUNK, 2):
            t1 = t0 + 1
            lhs = jnp.concatenate([hr, hi] + [us[s][:, ls] for s in range(t1 + 1)], axis=1)
            rows = [jnp.concatenate([vre_ref[t0, half], vre_ref[t1, half]], axis=1),
                    jnp.concatenate([vim_ref[t0, half], vim_ref[t1, half]], axis=1)]
            for s in range(t1 + 1):
                left = m_ref[t0 - s, half] if s <= t0 else zero_m
                rows.append(jnp.concatenate([left, m_ref[t1 - s, half]], axis=1))
            acc = _dot(lhs, jnp.concatenate(rows, axis=0))
            y_ref[pl.ds(t0, n, stride=S5_CHUNK), ls] = acc[:, :gi]
            y_ref[pl.ds(t1, n, stride=S5_CHUNK), ls] = acc[:, gi:]


def s5_outputs(p_a, m, hre, him, vre, vim, layer, tr):
    rows = p_a.shape[0] // S5_CHUNK
    gi, gp = S5_GB * S5_GROUP, S5_GB * S5_STATE
    n_b = S5_GROUPS // (S5_GB * S5_GB_STEP)
    mspec = pl.BlockSpec((None, S5_CHUNK, S5_GB_STEP, gi, gi), lambda i, b: (layer, 0, b, 0, 0))
    vspec = pl.BlockSpec((None, S5_CHUNK, S5_GB_STEP, gp, gi), lambda i, b: (layer, 0, b, 0, 0))
    hspec = pl.BlockSpec((tr, S5_GB_STEP * gp), lambda i, b: (i, b))
    uspec = pl.BlockSpec((S5_CHUNK * tr, S5_GB_STEP * gi), lambda i, b: (i, b))
    return pl.pallas_call(
        _s5_y_kernel,
        out_shape=jax.ShapeDtypeStruct((p_a.shape[0], BRANCH_W), F32),
        grid=(rows // tr, n_b),
        in_specs=[uspec, mspec, hspec, hspec, vspec, vspec],
        out_specs=uspec,
        compiler_params=_params("parallel", "parallel"),
        name="s5_outputs",
    )(p_a, m, hre, him, vre, vim)


def _s5_epilogue_kernel(y_ref, u_ref, g_ref, d_ref, w_ref, o_ref):
    z = _gelu_tanh(y_ref[...] + d_ref[...] * u_ref[...])
    o = z * _sigmoid(_dot(z.astype(BF16), w_ref[...]))
    o_ref[...] = (o * _silu(g_ref[...])).astype(o_ref.dtype)


def s5_epilogue(y, p_a, d_skip, w_glu, tr):
    rows, w = y.shape
    return pl.pallas_call(
        _s5_epilogue_kernel,
        out_shape=jax.ShapeDtypeStruct((N_BRANCH, rows, w), BF16),
        grid=(rows // tr,),
        in_specs=[pl.BlockSpec((tr, w), lambda i: (i, 0)),
                  pl.BlockSpec((tr, w), lambda i: (i, 0)),
                  pl.BlockSpec((tr, w), lambda i: (i, 1)),
                  pl.BlockSpec((1, w), lambda i: (0, 0)),
                  pl.BlockSpec((w, w), lambda i: (0, 0))],
        out_specs=pl.BlockSpec((None, tr, w), lambda i: (0, i, 0)),
        compiler_params=_params("parallel"),
        name="s5_epilogue",
    )(y, p_a, p_a, d_skip, w_glu)


_IN_WIDTHS = (("a_x", 1024), ("a_g", 1024), ("b_q", 1024), ("b_f", 1024), ("b_i", 1024), ("b_g", 1024),
              ("c_q", 512), ("c_k", 512), ("c_v", 1024), ("c_r", 16), ("c_g", 1024),
              ("d_x", 1024), ("d_g", 1024), ("e_q", 1024), ("e_g", 1024))

TM_NORM = 512
TM_MM = 1088
TN_MM = 2048
TN_CQ = 1280
TM_MERGE = 1088
TN_MERGE = 1024
TM_OUT = 544
TR_S5 = 544
TR_S5_EPI = 544
CW_S5_SCAN = 512
CHUNK_GLA = 128
NB_GLA_SAMPLE = 8
CHUNK_LRU = 128
ROWS_MEM_PROMPT = 512
NB_MEM_SAMPLE = 8


def _in_cols(w_in_l):
    cols, off = {}, 0
    for name, width in _IN_WIDTHS:
        cols[name] = w_in_l[:, off:off + width]
        off += width
    return cols


def kernel(x_prompt, x_sample, mem_prompt, state_s5_re, state_s5_im, state_hgrn, state_gla, state_rglru, state_conv, cache_mem_k, cache_mem_v, norm_pre, norm_post, w_in, s5_lam_re, s5_lam_im, s5_log_dt, s5_b_re, s5_b_im, s5_c_re, s5_c_im, s5_d, s5_w_glu, hg_lb_logits, hg_norm, gla_w_up, gla_b_up, gla_norm, conv_w, conv_b, lru_w_r, lru_b_r, lru_w_i, lru_b_i, lru_lam, mem_norm, w_mem_k, w_mem_v, w_branch, w_merge, w_out):
    d, w = D_MODEL, BRANCH_W
    m_p, m_s = BATCH * SEQ, DEC_BATCH * DEC_SEQ
    m_all = m_p + m_s
    bf = lambda a: a.astype(BF16)
    row = lambda v: v[None, :]

    x = jnp.concatenate([x_prompt.reshape(m_p, d), x_sample.reshape(m_s, d)], axis=0)

    mem2 = mem_prompt.reshape(BATCH * N_MEM, d)
    mk_l, mv_l = [], []
    for l in range(DEPTH):
        mn = rmsnorm_rows(mem2, row(mem_norm[l]), TM_NORM)
        mk_l.append(matmul(mn, bf(w_mem_k[l]), BATCH * N_MEM, w // 2))
        mv_l.append(matmul(mn, bf(w_mem_v[l]), BATCH * N_MEM, w // 2))

    lb = hg_lower_bounds(hg_lb_logits)
    s5_m, s5_wre, s5_wim, s5_vre, s5_vim = s5_prepare(s5_lam_re, s5_lam_im, s5_log_dt, s5_b_re, s5_b_im,
                                                      s5_c_re, s5_c_im)
    gs = S5_GROUPS * S5_STATE
    lam_re_row = s5_lam_re.reshape(DEPTH, 1, gs)
    lam_im_row = s5_lam_im.reshape(DEPTH, 1, gs)
    log_dt_row = jnp.broadcast_to(s5_log_dt[:, :, None], (DEPTH, S5_GROUPS, S5_STATE)).reshape(DEPTH, 1, gs)
    s5_h0r = state_s5_re.reshape(DEPTH, DEC_BATCH, gs)
    s5_h0i = state_s5_im.reshape(DEPTH, DEC_BATCH, gs)
    w_up_pad = jnp.zeros((DEPTH, V7X_LANES, GLA_KEY_W), F32).at[:, :GLA_RANK].set(gla_w_up)

    hg_p = hg_s = gl_p = gl_s = None
    outs = {k: [] for k in ("p_s5r", "p_s5i", "p_lru", "p_conv", "s_s5r", "s_s5i", "s_lru", "s_conv")}
    for l in range(DEPTH):
        cols = _in_cols(w_in[l])
        w_a = bf(jnp.concatenate([cols["a_x"], cols["a_g"]], axis=1))
        w_b = bf(jnp.concatenate([cols["b_q"], cols["b_f"], cols["b_i"], cols["b_g"]], axis=1))
        w_cv = bf(jnp.concatenate([cols["c_v"], cols["c_g"]], axis=1))
        w_cq = bf(jnp.concatenate([cols["c_q"], cols["c_k"], cols["c_r"],
                                   jnp.zeros((d, TN_CQ - 2 * GLA_KEY_W - GLA_RANK), F32)], axis=1))
        w_d = bf(jnp.concatenate([cols["d_x"], cols["d_g"]], axis=1))
        w_e = bf(jnp.concatenate([cols["e_q"], cols["e_g"]], axis=1))

        h = rmsnorm_rows(x, row(norm_pre[l]), TM_NORM)
        p_a = matmul(h, w_a, TM_MM, TN_MM)
        p_b = matmul(h, w_b, TM_MM, TN_MM)
        p_cv = matmul(h, w_cv, TM_MM, TN_MM)
        p_cq = matmul(h, w_cq, TM_MM, TN_CQ)
        p_d = matmul(h, w_d, TM_MM, TN_MM)
        p_e = matmul(h, w_e, TM_MM, TN_MM)

        ere, eim = s5_chunk_inputs(p_a, s5_wre, s5_wim, l, TR_S5)
        hre, him, fpr, fpi, fsr, fsi = s5_scan(ere, eim, s5_h0r[l], s5_h0i[l], lam_re_row, lam_im_row, log_dt_row,
                                               l, n_seq=BATCH, rows_per_seq=SEQ // S5_CHUNK, cw=CW_S5_SCAN)
        y = s5_outputs(p_a, s5_m, hre, him, s5_vre, s5_vim, l, TR_S5)
        br = s5_epilogue(y, p_a, row(s5_d[l]), bf(s5_w_glu[l]), TR_S5_EPI)
        outs["p_s5r"].append(fpr); outs["p_s5i"].append(fpi)
        outs["s_s5r"].append(fsr); outs["s_s5i"].append(fsi)

        hg_consts = [row(lb[l]), row(hg_norm[l])]
        hg_kw = dict(layer=l, slab=1, heads=HG_HEADS, dk=HG_DK, dv=HG_DV)
        hg_cols = [(p_b, w, 0), (p_b, w, 1), (p_b, w, 2), (p_b, w, 3)]
        br, hg_p = gla_branch("hgrn", hg_cols, hg_consts, None, br, hg_p, row_base=0, n_seq=BATCH,
                              seq_total=SEQ, chunk=CHUNK_GLA, nb=1, **hg_kw)
        br, hg_s = gla_branch("hgrn", hg_cols, hg_consts, state_hgrn, br, hg_s, row_base=m_p,
                              n_seq=DEC_BATCH, seq_total=DEC_SEQ, chunk=DEC_SEQ, nb=NB_GLA_SAMPLE, **hg_kw)

        gl_consts = [w_up_pad[l], row(gla_b_up[l]), row(gla_norm[l])]
        gl_kw = dict(layer=l, slab=2, heads=GLA_HEADS, dk=GLA_DK, dv=GLA_DV)
        gl_cols = [(p_cq, GLA_KEY_W, 0), (p_cq, GLA_KEY_W, 1), (p_cv, w, 0), (p_cq, V7X_LANES, w // V7X_LANES),
                   (p_cv, w, 1)]
        br, gl_p = gla_branch("gla", gl_cols, gl_consts, None, br, gl_p, row_base=0, n_seq=BATCH,
                              seq_total=SEQ, chunk=CHUNK_GLA, nb=1, **gl_kw)
        br, gl_s = gla_branch("gla", gl_cols, gl_consts, state_gla, br, gl_s, row_base=m_p,
                              n_seq=DEC_BATCH, seq_total=DEC_SEQ, chunk=DEC_SEQ, nb=NB_GLA_SAMPLE, **gl_kw)

        lru_consts = [conv_w[l], row(conv_b[l]), bf(lru_w_r[l]), row(lru_b_r[l]), bf(lru_w_i[l]), row(lru_b_i[l]),
                      row(lru_lam[l])]
        br, hfin_p, cfin_p = lru_prompt(p_d, lru_consts, br, slab=3, n_seq=BATCH, seq_total=SEQ, chunk=CHUNK_LRU)
        br, hfin_s, cfin_s = lru_sample(p_d, jnp.transpose(state_conv[l], (1, 0, 2)), state_rglru[l],
                                        lru_consts, br, slab=3, row_base=m_p)
        outs["p_lru"].append(hfin_p); outs["p_conv"].append(cfin_p)
        outs["s_lru"].append(hfin_s); outs["s_conv"].append(cfin_s)

        br = mem_attention(p_e, mk_l[l].reshape(1, BATCH, N_MEM, w), mv_l[l].reshape(1, BATCH, N_MEM, w), br,
                           layer=0, slab=4, row_base=0, n_seq=BATCH, seq_total=SEQ, rows_blk=ROWS_MEM_PROMPT, nbm=1)
        br = mem_attention_cache(p_e, cache_mem_k, cache_mem_v, br, layer=l, slab=4, row_base=m_p, n_seq=DEC_BATCH,
                                 seq_total=DEC_SEQ, nbm=NB_MEM_SAMPLE)

        merged = merge_branches(h, br, bf(w_merge[l]), bf(w_branch[l]), TM_MERGE, TN_MERGE)
        x = out_proj_residual(merged, bf(w_out[l]), row(norm_post[l]), x, TM_OUT)

    st = lambda k, shape: jnp.stack(outs[k]).reshape(shape)
    s5_p_shape = (DEPTH, BATCH, S5_GROUPS, S5_STATE)
    s5_s_shape = (DEPTH, DEC_BATCH, S5_GROUPS, S5_STATE)
    mem_shape = (DEPTH, BATCH, N_MEM, MEM_HEADS, MEM_HD)
    return (x[:m_p].reshape(BATCH, SEQ, d), x[m_p:].reshape(DEC_BATCH, DEC_SEQ, d),
            st("p_s5r", s5_p_shape), st("p_s5i", s5_p_shape), hg_p, gl_p,
            st("p_lru", (DEPTH, BATCH, w)), st("p_conv", (DEPTH, BATCH, CONV_W - 1, w)),
            jnp.stack(mk_l).reshape(mem_shape), jnp.stack(mv_l).reshape(mem_shape),
            st("s_s5r", s5_s_shape), st("s_s5i", s5_s_shape), hg_s, gl_s,
            st("s_lru", (DEPTH, DEC_BATCH, w)), jnp.transpose(jnp.stack(outs["s_conv"]), (0, 2, 1, 3)))
```

```python
import functools
import math

import jax
import jax.numpy as jnp
from jax import lax
from jax.experimental import pallas as pl
from jax.experimental.pallas import tpu as pltpu

F32 = jnp.float32
BF16 = jnp.bfloat16
HIGHEST = lax.Precision.HIGHEST

V7X_LANES = 128
V7X_SUBLANES = 8
V7X_VMEM_LIMIT_BYTES = 56 * 1024 * 1024

EPS = 1e-6
D_MODEL = 2048
DEPTH = 4
BATCH = 4
SEQ = 2048
DEC_BATCH = 128
DEC_SEQ = 4
BRANCH_W = 1024
N_BRANCH = 5
S5_GROUP = 16
S5_GROUPS = 64
S5_STATE = 64
S5_CHUNK = 4
S5_GB = 8
HG_HEADS = 8
HG_DK = 128
HG_DV = 128
GLA_HEADS = 4
GLA_KEY_W = 512
GLA_DK = 128
GLA_DV = 256
GLA_RANK = 16
GLA_GATE_TEMP = 16.0
LRU_BLOCKS = 8
LRU_BW = 128
CONV_W = 4
LRU_C = 8.0
N_MEM = 256
MEM_HEADS = 4
MEM_HD = 256


def _params(*sem):
    return pltpu.CompilerParams(dimension_semantics=sem, vmem_limit_bytes=V7X_VMEM_LIMIT_BYTES)


def _dot(a, b):
    return jnp.dot(a, b, preferred_element_type=F32)


def _dot_nt(a, b):
    return lax.dot_general(a, b, (((1,), (1,)), ((), ())), preferred_element_type=F32)


def _dot_tn(a, b):
    return lax.dot_general(a, b, (((0,), (0,)), ((), ())), preferred_element_type=F32)


def _sigmoid(x):
    return 1.0 / (1.0 + jnp.exp(-x))


def _silu(x):
    return x * _sigmoid(x)


def _log1p_exp_neg_abs(x):
    return jnp.log(1.0 + jnp.exp(-jnp.abs(x)))


def _log_sigmoid(x):
    return jnp.minimum(x, 0.0) - _log1p_exp_neg_abs(x)


def _softplus(x):
    return jnp.maximum(x, 0.0) + _log1p_exp_neg_abs(x)


def _gelu_tanh(x):
    return 0.5 * x * (1.0 + jnp.tanh(math.sqrt(2.0 / math.pi) * (x + 0.044715 * (x * x * x))))


def _row_iota(shape):
    return lax.broadcasted_iota(jnp.int32, shape, 0)


def _col_iota(shape):
    return lax.broadcasted_iota(jnp.int32, shape, 1)


def _shift_rows(x, k):
    return pltpu.roll(x, k, 0)


def _shift_rows_up(x, k):
    return pltpu.roll(x, x.shape[0] - k, 0)


def _rmsnorm_kernel(x_ref, g_ref, o_ref):
    x = x_ref[...]
    y = x * lax.rsqrt(jnp.mean(x * x, axis=-1, keepdims=True) + EPS)
    o_ref[...] = (y * g_ref[...]).astype(o_ref.dtype)


def rmsnorm_rows(x, g, tm):
    m, d = x.shape
    return pl.pallas_call(
        _rmsnorm_kernel,
        out_shape=jax.ShapeDtypeStruct((m, d), BF16),
        grid=(m // tm,),
        in_specs=[pl.BlockSpec((tm, d), lambda i: (i, 0)), pl.BlockSpec((1, d), lambda i: (0, 0))],
        out_specs=pl.BlockSpec((tm, d), lambda i: (i, 0)),
        compiler_params=_params("parallel"),
        name="rmsnorm_rows",
    )(x, g)


def _matmul_kernel(a_ref, w_ref, o_ref):
    o_ref[...] = _dot(a_ref[...], w_ref[...]).astype(o_ref.dtype)


def matmul(a, w, tm, tn, out_dtype=F32):
    m, k = a.shape
    n = w.shape[1]
    return pl.pallas_call(
        _matmul_kernel,
        out_shape=jax.ShapeDtypeStruct((m, n), out_dtype),
        grid=(m // tm, n // tn),
        in_specs=[pl.BlockSpec((tm, k), lambda i, j: (i, 0)), pl.BlockSpec((k, tn), lambda i, j: (0, j))],
        out_specs=pl.BlockSpec((tm, tn), lambda i, j: (i, j)),
        compiler_params=_params("parallel", "parallel"),
        name="matmul",
    )(a, w)


def _merge_kernel(h_ref, b_ref, wm_ref, wb_ref, o_ref, acc_ref):
    c = pl.program_id(1)
    j = pl.program_id(2)
    nj, _, tn = acc_ref.shape
    contrib = _sigmoid(_dot(h_ref[...], wm_ref[...])) * _dot(b_ref[...], wb_ref[...])

    @pl.when(c == 0)
    def _():
        acc_ref[j] = contrib

    @pl.when(jnp.logical_and(c > 0, c < N_BRANCH - 1))
    def _():
        acc_ref[j] += contrib

    for jj in range(nj):
        @pl.when(jnp.logical_and(c == N_BRANCH - 1, j == jj))
        def _(jj=jj):
            o_ref[:, jj * tn:(jj + 1) * tn] = (acc_ref[jj] + contrib).astype(o_ref.dtype)


def merge_branches(h, branches, w_merge, w_branch, tm, tn):
    m, d = h.shape
    w = branches.shape[2]
    nj = d // tn
    return pl.pallas_call(
        _merge_kernel,
        out_shape=jax.ShapeDtypeStruct((m, d), BF16),
        grid=(m // tm, N_BRANCH, nj),
        in_specs=[pl.BlockSpec((tm, d), lambda i, c, j: (i, 0)),
                  pl.BlockSpec((None, tm, w), lambda i, c, j: (c, i, 0)),
                  pl.BlockSpec((d, tn), lambda i, c, j: (0, c * nj + j)),
                  pl.BlockSpec((None, w, tn), lambda i, c, j: (c, 0, j))],
        out_specs=pl.BlockSpec((tm, d), lambda i, c, j: (i, 0)),
        scratch_shapes=[pltpu.VMEM((nj, tm, tn), F32)],
        compiler_params=_params("parallel", "arbitrary", "arbitrary"),
        name="merge_branches",
    )(h, branches, w_merge, w_branch)


def _out_kernel(m_ref, w_ref, g_ref, x_ref, *rest):
    y = _dot(m_ref[...], w_ref[...])
    y = y * lax.rsqrt(jnp.mean(y * y, axis=-1, keepdims=True) + EPS)
    xn = x_ref[...] + y * g_ref[...]
    if len(rest) == 1:
        (o_ref,) = rest
    else:
        gn_ref, o_ref, h_ref = rest
        hn = xn * lax.rsqrt(jnp.mean(xn * xn, axis=-1, keepdims=True) + EPS)
        h_ref[...] = (hn * gn_ref[...]).astype(h_ref.dtype)
    o_ref[...] = xn


def out_proj_residual(merged, w_out, g, x, g_next, tm):
    m, d = x.shape
    rows = pl.BlockSpec((tm, d), lambda i: (i, 0))
    vec = pl.BlockSpec((1, d), lambda i: (0, 0))
    in_specs = [rows, pl.BlockSpec((d, d), lambda i: (0, 0)), vec, rows]
    args = [merged, w_out, g, x]
    out_shape = jax.ShapeDtypeStruct((m, d), F32)
    out_specs = rows
    if g_next is not None:
        in_specs.append(vec)
        args.append(g_next)
        out_shape = (out_shape, jax.ShapeDtypeStruct((m, d), BF16))
        out_specs = (rows, rows)
    return pl.pallas_call(
        _out_kernel,
        out_shape=out_shape,
        grid=(m // tm,),
        in_specs=in_specs,
        out_specs=out_specs,
        compiler_params=_params("parallel"),
        name="out_proj_residual",
    )(*args)


def _hg_lb_kernel(logit_ref, lb_ref):
    x = logit_ref[...]
    e = jnp.exp(x - jnp.max(x, axis=0, keepdims=True))
    sm = e / jnp.sum(e, axis=0, keepdims=True)
    acc = jnp.zeros_like(sm[0:1])
    lb_ref[0:1, :] = acc
    for l in range(1, DEPTH):
        acc = acc + sm[l:l + 1]
        lb_ref[l:l + 1, :] = acc


def hg_lower_bounds(logits):
    return pl.pallas_call(
        _hg_lb_kernel,
        out_shape=jax.ShapeDtypeStruct(logits.shape, F32),
        name="hg_lower_bounds",
    )(logits)


def _gla_kernel(*refs, mode, seq_len, nb, heads, dk, dv, has_init, sub):
    n_row = 4 if mode == "hgrn" else 5
    n_const = 2 if mode == "hgrn" else 3
    row_refs, consts, rest = refs[:n_row], refs[n_row:n_row + n_const], refs[n_row + n_const:]
    if has_init:
        s0_ref, rest = rest[0], rest[1:]
    o_ref, sout_ref, s_scr = rest[:3]
    scratch = rest[3:]

    @pl.when(pl.program_id(1) == 0)
    def _():
        if has_init:
            s_scr[...] = s0_ref[...]
        else:
            s_scr[...] = jnp.zeros_like(s_scr)

    rows = nb * seq_len
    for sc in range(sub):
        window = pl.ds(sc * rows, rows)
        _gla_chunk(*[r.at[window] for r in row_refs], *consts, o_ref.at[window], s_scr, *scratch,
                   mode=mode, seq_len=seq_len, nb=nb, heads=heads, dk=dk, dv=dv)

    @pl.when(pl.program_id(1) == pl.num_programs(1) - 1)
    def _():
        sout_ref[...] = s_scr[...]


def _gla_chunk(*refs, mode, seq_len, nb, heads, dk, dv):
    L = seq_len
    R = nb * L
    C = heads * dk
    if mode == "hgrn":
        q_ref, f_ref, v_ref, g_ref, lb_ref, gn_ref = refs[:6]
        rest = refs[6:]
    else:
        q_ref, k_ref, v_ref, r_ref, g_ref, wup_ref, bup_ref, gn_ref = refs[:8]
        rest = refs[8:]
    o_ref, s_scr, att_scr, qe_scr, kd_scr, b_scr = rest

    t = _row_iota((R, C)) % L
    if mode == "hgrn":
        lb = lb_ref[...]
        fz = f_ref[...]
        la = jnp.log(lb)
        lc = jnp.log1p(-lb) + _log_sigmoid(fz)
        g = jnp.maximum(la, lc) + _log1p_exp_neg_abs(la - lc)
        k = (1.0 - lb) * _sigmoid(-fz)
        q = q_ref[...]
    else:
        pre = jnp.dot(r_ref[...], wup_ref[...], preferred_element_type=F32, precision=HIGHEST)
        g = _log_sigmoid(pre + bup_ref[...]) * (1.0 / GLA_GATE_TEMP)
        k = k_ref[...]
        q = q_ref[...] * (dk ** -0.5)

    rr = _row_iota((R, R))
    cc = _col_iota((R, R))
    same_seq = (rr // L) == (cc // L)
    tri_incl = jnp.where(same_seq, jnp.where(cc <= rr, 1.0, 0.0), 0.0)
    tri_after = jnp.where(same_seq, jnp.where(cc > rr, 1.0, 0.0), 0.0)
    tri = jnp.concatenate([tri_incl, tri_after], axis=0).astype(BF16)
    g_hi = g.astype(BF16)
    g_r1 = g - g_hi.astype(F32)
    g_mid = g_r1.astype(BF16)
    g_lo = (g_r1 - g_mid.astype(F32)).astype(BF16)
    sums = _dot(tri, g_hi) + _dot(tri, g_mid) + _dot(tri, g_lo)
    b = sums[:R]
    rb = sums[R:]
    b_scr[...] = b
    qe_scr[...] = (q * jnp.exp(b)).astype(BF16)
    kd_scr[...] = (k * jnp.exp(rb)).astype(BF16)

    def add_level(level, qt, kt, mask):
        same = kt is qt
        qt = qt.astype(BF16)
        kt = qt if same else kt.astype(BF16)
        for h in range(heads):
            hs = slice(h * dk, (h + 1) * dk)
            p = jnp.where(mask, _dot_nt(qt[:, hs], kt[:, hs]), 0.0)
            if level == 0:
                att_scr[h] = p
            else:
                att_scr[h] += p

    add_level(0, q, k, rr == cc)
    level = 1
    m = 1
    while m < L:
        pos = t % (2 * m)
        upper = pos >= m
        if m == 1:
            d = jnp.where(upper, g, 0.0)
        elif m == 2:
            d = jnp.where(pos == 2, g,
                          jnp.where(pos == 3, g + _shift_rows(g, 1),
                                    jnp.where(pos == 0, _shift_rows_up(g, 1), 0.0)))
        else:
            b3 = b.reshape(R // (2 * m), 2 * m, C)
            d3 = b3 - b3[:, m - 1:m, :]
            d = -jnp.abs(d3.reshape(R, C))
        x = jnp.where(upper, q, k) * jnp.exp(d)
        blk = 2 * m
        pair = jnp.logical_and((rr // blk) == (cc // blk),
                               jnp.logical_and((rr % blk) >= m, (cc % blk) < m))
        add_level(level, x, x, pair)
        level += 1
        m *= 2

    gn = gn_ref[...]
    seq_o = _row_iota((R, dv)) // L
    seq_k = _row_iota((R, dk)) // L
    pad_rows = (-R) % V7X_LANES
    for h in range(heads):
        hs = slice(h * dk, (h + 1) * dk)
        vs = slice(h * dv, (h + 1) * dv)
        vh = v_ref[:, vs].astype(BF16)
        vh_p = jnp.concatenate([vh, jnp.zeros((pad_rows, dv), BF16)], axis=0) if pad_rows else vh
        o = _dot(att_scr[h].astype(BF16), vh)
        qe = qe_scr[:, hs]
        kd = kd_scr[:, hs]
        for j in range(nb):
            st = s_scr[j, h]
            inter = _dot(qe, st.astype(BF16))
            kd_j = kd
            if nb > 1:
                inter = jnp.where(seq_o == j, inter, 0.0)
                kd_j = jnp.where(seq_k == j, kd, jnp.zeros_like(kd))
            o = o + inter
            if pad_rows:
                kd_j = jnp.concatenate([kd_j, jnp.zeros((pad_rows, dk), BF16)], axis=0)
            upd = _dot_tn(kd_j, vh_p)
            dl = jnp.exp(b_scr[j * L + L - 1:j * L + L, hs])
            colb = jnp.broadcast_to(dl, (dk, dk)).T
            if dv != dk:
                colb = jnp.concatenate([colb] * (dv // dk), axis=1)
            s_scr[j, h] = colb * st + upd
        on = o * lax.rsqrt(jnp.mean(o * o, axis=-1, keepdims=True) + EPS) * gn
        o_ref[:, vs] = (on * _silu(g_ref[:, vs])).astype(o_ref.dtype)


def gla_branch(mode, col_blocks, consts, s0, out_prev, sout_prev, *, layer, slab, row_base, n_seq, seq_total,
               chunk, nb, heads, dk, dv, sub=1):
    L = chunk
    R = nb * L
    rows_blk = sub * R
    n_chunks = seq_total // (sub * L)
    base_blk = row_base // rows_blk
    width = heads * dv

    def rows(i, c):
        return base_blk + i * n_chunks + c

    in_specs = [pl.BlockSpec((rows_blk, w), functools.partial(lambda i, c, bi: (rows(i, c), bi), bi=bi))
                for (_, w, bi) in col_blocks]
    args = [a for (a, _, _) in col_blocks]
    for a in consts:
        in_specs.append(pl.BlockSpec(a.shape, lambda i, c: (0, 0)))
        args.append(a)
    has_init = s0 is not None
    state_spec = pl.BlockSpec((None, nb, heads, dk, dv), lambda i, c: (layer, i, 0, 0, 0))
    if has_init:
        in_specs.append(state_spec)
        args.append(s0)
    n_real = len(args)
    aliases = {}
    for out_idx, prev in enumerate((out_prev, sout_prev)):
        if prev is not None:
            aliases[len(args)] = out_idx
            in_specs.append(pl.BlockSpec(memory_space=pl.ANY))
            args.append(prev)
    n_in = len(args)

    def kern(*refs):
        _gla_kernel(*refs[:n_real], *refs[n_in:], mode=mode, seq_len=L, nb=nb, heads=heads, dk=dk, dv=dv,
                    has_init=has_init, sub=sub)

    return pl.pallas_call(
        kern,
        out_shape=(jax.ShapeDtypeStruct(out_prev.shape, BF16),
                   jax.ShapeDtypeStruct((DEPTH, n_seq, heads, dk, dv), F32)),
        grid=(n_seq // nb, n_chunks),
        in_specs=in_specs,
        out_specs=(pl.BlockSpec((None, rows_blk, width), lambda i, c: (slab, rows(i, c), 0)), state_spec),
        scratch_shapes=[pltpu.VMEM((nb, heads, dk, dv), F32), pltpu.VMEM((heads, R, R), F32),
                        pltpu.VMEM((R, heads * dk), BF16), pltpu.VMEM((R, heads * dk), BF16),
                        pltpu.VMEM((R, heads * dk), F32)],
        input_output_aliases=aliases,
        compiler_params=_params("parallel", "arbitrary"),
        name=mode + "_branch",
    )(*args)


def _lru_gates(xc, wr_ref, br_ref, wi_ref, bi_ref, lam_ref):
    xcb = xc.astype(BF16)
    r_parts, i_parts = [], []
    for blk in range(xc.shape[1] // LRU_BW):
        bs = slice(blk * LRU_BW, (blk + 1) * LRU_BW)
        r_parts.append(_dot(xcb[:, bs], wr_ref[blk]))
        i_parts.append(_dot(xcb[:, bs], wi_ref[blk]))
    r = _sigmoid(jnp.concatenate(r_parts, axis=1) + br_ref[...])
    ig = _sigmoid(jnp.concatenate(i_parts, axis=1) + bi_ref[...])
    log_a = (-LRU_C) * r * _softplus(-lam_ref[...])
    a = jnp.exp(log_a)
    u = jnp.sqrt(1.0 - jnp.exp(2.0 * log_a)) * (ig * xc)
    return a, u


def _lru_prompt_kernel(x_ref, g_ref, cw_ref, cb_ref, wr_ref, br_ref, wi_ref, bi_ref, lam_ref,
                       o_ref, hfin_ref, cfin_ref, tail_scr, hc_scr, a_scr, u_scr):
    tc, w = x_ref.shape
    sub = V7X_SUBLANES
    c = pl.program_id(1)

    @pl.when(c == 0)
    def _():
        tail_scr[...] = jnp.zeros_like(tail_scr)
        hc_scr[...] = jnp.zeros_like(hc_scr)

    x = x_ref[...]
    tail = tail_scr[...]
    row8 = _row_iota((sub, w))
    cw = cw_ref[...]
    xc = cb_ref[...] + cw[CONV_W - 1:CONV_W] * x
    for k in range(1, CONV_W):
        xs = _shift_rows(x, k)
        head = jnp.where(row8 < k, _shift_rows(tail, k), xs[0:sub])
        xs = jnp.concatenate([head, xs[sub:]], axis=0)
        xc = xc + cw[CONV_W - 1 - k:CONV_W - k] * xs
    last8 = x[tc - sub:tc]
    tail_scr[...] = last8

    a, u = _lru_gates(xc, wr_ref, br_ref, wi_ref, bi_ref, lam_ref)
    a_scr[...] = a
    u_scr[...] = u

    def body(i, carry):
        r0 = pl.multiple_of(i * sub, sub)
        at = a_scr[pl.ds(r0, sub), :]
        ut = u_scr[pl.ds(r0, sub), :]
        for k in (1, 2, 4):
            keep = row8 >= k
            ut = ut + at * jnp.where(keep, _shift_rows(ut, k), 0.0)
            at = at * jnp.where(keep, _shift_rows(at, k), 1.0)
        ht = ut + at * carry
        u_scr[pl.ds(r0, sub), :] = ht
        return jnp.broadcast_to(ht[sub - 1:sub], (sub, w))

    carry = lax.fori_loop(0, tc // sub, body, hc_scr[...])
    hc_scr[...] = carry
    o_ref[...] = (u_scr[...] * _silu(g_ref[...])).astype(o_ref.dtype)

    @pl.when(c == pl.num_programs(1) - 1)
    def _():
        hfin_ref[...] = carry[0:1]
        cfin_ref[...] = _shift_rows(last8, CONV_W - 1)[0:CONV_W - 1]


def lru_prompt(p_d, consts, out_prev, *, slab, n_seq, seq_total, chunk):
    w = BRANCH_W
    n_chunks = seq_total // chunk
    const_specs = [pl.BlockSpec(a.shape, functools.partial(lambda i, c, nd: (0,) * nd, nd=a.ndim)) for a in consts]

    def kern(*refs):
        n_in = 2 + len(consts)
        _lru_prompt_kernel(*refs[:n_in], *refs[n_in + 1:])

    return pl.pallas_call(
        kern,
        out_shape=(jax.ShapeDtypeStruct(out_prev.shape, BF16),
                   jax.ShapeDtypeStruct((n_seq, 1, w), F32),
                   jax.ShapeDtypeStruct((n_seq, CONV_W - 1, w), F32)),
        grid=(n_seq, n_chunks),
        in_specs=[pl.BlockSpec((chunk, w), lambda i, c: (i * n_chunks + c, 0)),
                  pl.BlockSpec((chunk, w), lambda i, c: (i * n_chunks + c, 1))] + const_specs
        + [pl.BlockSpec(memory_space=pl.ANY)],
        out_specs=(pl.BlockSpec((None, chunk, w), lambda i, c: (slab, i * n_chunks + c, 0)),
                   pl.BlockSpec((None, 1, w), lambda i, c: (i, 0, 0)),
                   pl.BlockSpec((None, CONV_W - 1, w), lambda i, c: (i, 0, 0))),
        scratch_shapes=[pltpu.VMEM((V7X_SUBLANES, w), F32), pltpu.VMEM((V7X_SUBLANES, w), F32),
                        pltpu.VMEM((chunk, w), F32), pltpu.VMEM((chunk, w), F32)],
        input_output_aliases={2 + len(consts): 0},
        compiler_params=_params("parallel", "arbitrary"),
        name="lru_prompt",
    )(p_d, p_d, *consts, out_prev)


def _lru_sample_kernel(x_ref, g_ref, buf_ref, h0_ref, cw_ref, cb_ref, wr_ref, br_ref, wi_ref, bi_ref, lam_ref,
                       prev_ref, o_ref, hfin_ref, cfin_ref, o_scr):
    del prev_ref
    steps = DEC_SEQ
    n = x_ref.shape[0] // steps
    step_rows = lambda t: pl.ds(t, n, stride=steps)
    xcat = [buf_ref[j] for j in range(CONV_W - 1)] + [x_ref[step_rows(t), :] for t in range(steps)]
    cw = cw_ref[...]
    xcs = []
    for t in range(steps):
        xc = cb_ref[...]
        for j in range(CONV_W):
            xc = xc + cw[j:j + 1] * xcat[t + j]
        xcs.append(xc)
    a, u = _lru_gates(jnp.concatenate(xcs, axis=0), wr_ref, br_ref, wi_ref, bi_ref, lam_ref)
    h = h0_ref[...]
    for t in range(steps):
        h = a[t * n:(t + 1) * n] * h + u[t * n:(t + 1) * n]
        o_scr[step_rows(t), :] = h * _silu(g_ref[step_rows(t), :])
    o_ref[...] = o_scr[...].astype(o_ref.dtype)
    hfin_ref[...] = h
    for j in range(CONV_W - 1):
        cfin_ref[j] = xcat[steps + j]


def lru_sample(p_d, conv_buf, h0, consts, out_prev, *, slab, row_base):
    w = BRANCH_W
    bw = LRU_BW
    n = h0.shape[0]
    rows = n * DEC_SEQ
    blk = row_base // rows
    nbk = w // bw
    conv_w_, conv_b_, w_r, b_r, w_i, b_i, lam = consts
    lane = lambda r: pl.BlockSpec((r, bw), lambda i: (0, i))
    wblk = pl.BlockSpec((1, bw, bw), lambda i: (i, 0, 0))
    tail = pl.BlockSpec((CONV_W - 1, n, bw), lambda i: (0, 0, i))
    return pl.pallas_call(
        _lru_sample_kernel,
        out_shape=(jax.ShapeDtypeStruct(out_prev.shape, BF16),
                   jax.ShapeDtypeStruct((n, w), F32),
                   jax.ShapeDtypeStruct((CONV_W - 1, n, w), F32)),
        grid=(nbk,),
        in_specs=[pl.BlockSpec((rows, bw), lambda i: (blk, i)), pl.BlockSpec((rows, bw), lambda i: (blk, nbk + i)),
                  tail, lane(n), lane(CONV_W), lane(1), wblk, lane(1), wblk, lane(1), lane(1),
                  pl.BlockSpec(memory_space=pl.ANY)],
        out_specs=(pl.BlockSpec((None, rows, bw), lambda i: (slab, blk, i)), lane(n), tail),
        scratch_shapes=[pltpu.VMEM((rows, bw), F32)],
        input_output_aliases={11: 0},
        compiler_params=_params("parallel"),
        name="lru_sample",
    )(p_d, p_d, conv_buf, h0, conv_w_, conv_b_, w_r, b_r, w_i, b_i, lam, out_prev)


def _mem_attn_kernel(q_ref, g_ref, k_ref, v_ref, prev_ref, o_ref, s_scr, p_scr, *, nbm, seq_rows):
    del prev_ref
    rows = q_ref.shape[0]
    q = (q_ref[...] * (MEM_HD ** -0.5)).astype(BF16)
    for h in range(MEM_HEADS):
        hs = slice(h * MEM_HD, (h + 1) * MEM_HD)
        for j in range(nbm):
            pair = h * nbm + j
            s_scr[pair * rows:(pair + 1) * rows, :] = _dot_nt(q[:, hs], k_ref[j, :, hs].astype(BF16))
    s = s_scr[...]
    e = jnp.exp(s - jnp.max(s, axis=-1, keepdims=True))
    p_scr[...] = (e / jnp.sum(e, axis=-1, keepdims=True)).astype(BF16)
    seq = _row_iota((rows, MEM_HD)) // seq_rows
    for h in range(MEM_HEADS):
        hs = slice(h * MEM_HD, (h + 1) * MEM_HD)
        acc = None
        for j in range(nbm):
            pair = h * nbm + j
            oh = _dot(p_scr[pair * rows:(pair + 1) * rows, :], v_ref[j, :, hs].astype(BF16))
            acc = oh if acc is None else jnp.where(seq == j, oh, acc)
        o_ref[:, hs] = (acc * _silu(g_ref[:, hs])).astype(o_ref.dtype)


def mem_attention(p_e, mem_k, mem_v, out_prev, *, layer, slab, row_base, n_seq, seq_total, rows_blk, nbm):
    w = BRANCH_W
    base_blk = row_base // rows_blk
    if nbm == 1:
        t_blocks = seq_total // rows_blk
        grid = (n_seq, t_blocks)
        rmap = lambda i, t: base_blk + i * t_blocks + t
    else:
        grid = (n_seq // nbm, 1)
        rmap = lambda i, t: base_blk + i
    mem_spec = pl.BlockSpec((None, nbm, N_MEM, w), lambda i, t: (layer, i, 0, 0))
    in_specs = [pl.BlockSpec((rows_blk, w), lambda i, t: (rmap(i, t), 0)),
                pl.BlockSpec((rows_blk, w), lambda i, t: (rmap(i, t), 1)),
                mem_spec, mem_spec, pl.BlockSpec(memory_space=pl.ANY)]
    args = [p_e, p_e, mem_k, mem_v, out_prev]
    aliases = {4: 0}
    return pl.pallas_call(
        functools.partial(_mem_attn_kernel, nbm=nbm, seq_rows=seq_total),
        out_shape=jax.ShapeDtypeStruct(out_prev.shape, BF16),
        grid=grid,
        in_specs=in_specs,
        out_specs=pl.BlockSpec((None, rows_blk, w), lambda i, t: (slab, rmap(i, t), 0)),
        scratch_shapes=[pltpu.VMEM((MEM_HEADS * nbm * rows_blk, N_MEM), F32),
                        pltpu.VMEM((MEM_HEADS * nbm * rows_blk, N_MEM), BF16)],
        input_output_aliases=aliases,
        compiler_params=_params("parallel", "arbitrary"),
        name="mem_attention",
    )(*args)


def _mem_attn_cache_kernel(q_ref, g_ref, k_hbm, v_hbm, prev_ref, o_ref, kbuf, vbuf, sems, s_scr, p_scr,
                           *, layer, nbm, seq_rows):
    del prev_ref
    i = pl.program_id(0)
    n_steps = pl.num_programs(0)

    def slab_copies(step, slot):
        out = []
        for h in range(MEM_HEADS):
            for which, (src, dst) in enumerate(((k_hbm, kbuf), (v_hbm, vbuf))):
                out.append(pltpu.make_async_copy(src.at[layer, pl.ds(step * nbm, nbm), :, h, :],
                                                 dst.at[slot, h], sems.at[slot, which, h]))
        return out

    slot = i % 2

    @pl.when(i == 0)
    def _():
        for c in slab_copies(0, 0):
            c.start()

    @pl.when(i + 1 < n_steps)
    def _():
        for c in slab_copies(i + 1, 1 - slot):
            c.start()

    for c in slab_copies(i, slot):
        c.wait()

    rows = q_ref.shape[0]
    q = (q_ref[...] * (MEM_HD ** -0.5)).astype(BF16)
    for h in range(MEM_HEADS):
        hs = slice(h * MEM_HD, (h + 1) * MEM_HD)
        for j in range(nbm):
            pair = h * nbm + j
            s_scr[pair * rows:(pair + 1) * rows, :] = _dot_nt(q[:, hs], kbuf[slot, h, j].astype(BF16))
    s = s_scr[...]
    e = jnp.exp(s - jnp.max(s, axis=-1, keepdims=True))
    p_scr[...] = (e / jnp.sum(e, axis=-1, keepdims=True)).astype(BF16)
    seq = _row_iota((rows, MEM_HD)) // seq_rows
    for h in range(MEM_HEADS):
        hs = slice(h * MEM_HD, (h + 1) * MEM_HD)
        acc = None
        for j in range(nbm):
            pair = h * nbm + j
            oh = _dot(p_scr[pair * rows:(pair + 1) * rows, :], vbuf[slot, h, j].astype(BF16))
            acc = oh if acc is None else jnp.where(seq == j, oh, acc)
        o_ref[:, hs] = (acc * _silu(g_ref[:, hs])).astype(o_ref.dtype)


def mem_attention_cache(p_e, cache_k, cache_v, out_prev, *, layer, slab, row_base, n_seq, seq_total, nbm):
    w = BRANCH_W
    rows_blk = nbm * seq_total
    base_blk = row_base // rows_blk
    buf = pltpu.VMEM((2, MEM_HEADS, nbm, N_MEM, MEM_HD), F32)
    return pl.pallas_call(
        functools.partial(_mem_attn_cache_kernel, layer=layer, nbm=nbm, seq_rows=seq_total),
        out_shape=jax.ShapeDtypeStruct(out_prev.shape, BF16),
        grid=(n_seq // nbm,),
        in_specs=[pl.BlockSpec((rows_blk, w), lambda i: (base_blk + i, 0)),
                  pl.BlockSpec((rows_blk, w), lambda i: (base_blk + i, 1)),
                  pl.BlockSpec(memory_space=pl.ANY), pl.BlockSpec(memory_space=pl.ANY),
                  pl.BlockSpec(memory_space=pl.ANY)],
        out_specs=pl.BlockSpec((None, rows_blk, w), lambda i: (slab, base_blk + i, 0)),
        scratch_shapes=[buf, buf, pltpu.SemaphoreType.DMA((2, 2, MEM_HEADS)),
                        pltpu.VMEM((MEM_HEADS * nbm * rows_blk, N_MEM), F32),
                        pltpu.VMEM((MEM_HEADS * nbm * rows_blk, N_MEM), BF16)],
        input_output_aliases={4: 0},
        compiler_params=_params("arbitrary"),
        name="mem_attention_cache",
    )(p_e, p_e, cache_k, cache_v, out_prev)


def _cmul(ar, ai, br, bi):
    return ar * br - ai * bi, ar * bi + ai * br


def _s5_abar(lr, li, log_dt):
    dt = jnp.exp(log_dt)
    mag = jnp.exp(lr * dt)
    return mag * jnp.cos(li * dt), mag * jnp.sin(li * dt)


def _s5_prep_kernel(lra, lia, dta, bra, bia, cfr, cfi, lrc, lic, dtc, ctr, cti,
                    m_ref, wre_ref, wim_ref, vre_ref, vim_ref):
    gi = S5_GB * S5_GROUP
    gp = S5_GB * S5_STATE
    lr, li = lra[...], lia[...]
    ar, ai = _s5_abar(lr, li, dta[...])
    den = lr * lr + li * li
    zr = ((ar - 1.0) * lr + ai * li) / den
    zi = (ai * lr - (ar - 1.0) * li) / den
    bbr, bbi = _cmul(zr, zi, bra[...], bia[...])
    wmask = (_row_iota((gi, gp)) // S5_GROUP) == (_col_iota((gi, gp)) // S5_STATE)
    mmask = (_row_iota((gi, gi)) // S5_GROUP) == (_col_iota((gi, gi)) // S5_GROUP)
    cr, ci = cfr[...], cfi[...]
    pr, pi = jnp.ones_like(ar), jnp.zeros_like(ar)
    for tau in range(S5_CHUNK):
        lrr, lii = _cmul(pr, pi, bbr, bbi)
        wre_ref[S5_CHUNK - 1 - tau] = jnp.where(wmask, lrr, 0.0).astype(BF16)
        wim_ref[S5_CHUNK - 1 - tau] = jnp.where(wmask, lii, 0.0).astype(BF16)
        x = (lax.dot_general(lrr[:, :S5_STATE], cr, (((1,), (1,)), ((), ())), precision=HIGHEST,
                             preferred_element_type=F32)
             - lax.dot_general(lii[:, :S5_STATE], ci, (((1,), (1,)), ((), ())), precision=HIGHEST,
                               preferred_element_type=F32))
        m_ref[tau] = jnp.where(mmask, x, 0.0).astype(BF16)
        pr, pi = _cmul(pr, pi, ar, ai)

    arc, aic = _s5_abar(lrc[...], lic[...], dtc[...])
    vmask = (_row_iota((gp, gi)) // S5_STATE) == (_col_iota((gp, gi)) // S5_GROUP)
    pr, pi = arc, aic
    for t in range(S5_CHUNK):
        vr, vi = _cmul(ctr[...], cti[...], pr, pi)
        vre_ref[t] = jnp.where(vmask, vr, 0.0).astype(BF16)
        vim_ref[t] = jnp.where(vmask, -vi, 0.0).astype(BF16)
        pr, pi = _cmul(pr, pi, arc, aic)


def s5_prepare(lam_re, lam_im, log_dt, b_re, b_im, c_re, c_im):
    dp, g, p = lam_re.shape
    i = S5_GROUP
    gi, gp = S5_GB * i, S5_GB * p
    nb = g // S5_GB

    def a_layout(x_gp):
        return jnp.broadcast_to(x_gp[:, :, None, None, :], (dp, g, i, S5_GB, p)).reshape(dp, g * i, gp)

    def c_layout(x_gp):
        return jnp.broadcast_to(x_gp[:, :, :, None], (dp, g, p, gi)).reshape(dp, g * p, gi)

    ldt = jnp.broadcast_to(log_dt[:, :, None], (dp, g, p))
    bt = lambda b: jnp.broadcast_to(jnp.transpose(b, (0, 1, 3, 2))[:, :, :, None, :],
                                    (dp, g, i, S5_GB, p)).reshape(dp, g * i, gp)
    ct = lambda c: jnp.broadcast_to(jnp.transpose(c, (0, 1, 3, 2))[:, :, :, None, :],
                                    (dp, g, p, S5_GB, i)).reshape(dp, g * p, gi)
    args = [a_layout(lam_re), a_layout(lam_im), a_layout(ldt), bt(b_re), bt(b_im),
            c_re.reshape(dp, g * i, p), c_im.reshape(dp, g * i, p),
            c_layout(lam_re), c_layout(lam_im), c_layout(ldt), ct(c_re), ct(c_im)]
    spec_a = pl.BlockSpec((None, gi, gp), lambda l, b: (l, b, 0))
    spec_f = pl.BlockSpec((None, gi, p), lambda l, b: (l, b, 0))
    spec_c = pl.BlockSpec((None, gp, gi), lambda l, b: (l, b, 0))
    out5 = lambda r, c: pl.BlockSpec((None, S5_CHUNK, None, r, c), lambda l, b: (l, 0, b, 0, 0))
    shp = lambda r, c: jax.ShapeDtypeStruct((dp, S5_CHUNK, nb, r, c), BF16)
    return pl.pallas_call(
        _s5_prep_kernel,
        out_shape=(shp(gi, gi), shp(gi, gp), shp(gi, gp), shp(gp, gi), shp(gp, gi)),
        grid=(dp, nb),
        in_specs=[spec_a] * 5 + [spec_f] * 2 + [spec_c] * 5,
        out_specs=(out5(gi, gi), out5(gi, gp), out5(gi, gp), out5(gp, gi), out5(gp, gi)),
        compiler_params=_params("parallel", "parallel"),
        name="s5_prepare",
    )(*args)


S5_GB_STEP = 1


def _s5_steps(u_ref):
    n = u_ref.shape[0] // S5_CHUNK
    return [u_ref[pl.ds(s, n, stride=S5_CHUNK), :].astype(BF16) for s in range(S5_CHUNK)]


def _s5_e_kernel(u_ref, wre_ref, wim_ref, ere_ref, eim_ref):
    gi, gp = S5_GB * S5_GROUP, S5_GB * S5_STATE
    us = _s5_steps(u_ref)
    for half in range(S5_GB_STEP):
        ucat = jnp.concatenate([u[:, half * gi:(half + 1) * gi] for u in us], axis=1)
        wre = jnp.concatenate([wre_ref[s, half] for s in range(S5_CHUNK)], axis=0)
        wim = jnp.concatenate([wim_ref[s, half] for s in range(S5_CHUNK)], axis=0)
        ere_ref[:, half * gp:(half + 1) * gp] = _dot(ucat, wre)
        eim_ref[:, half * gp:(half + 1) * gp] = _dot(ucat, wim)


def s5_chunk_inputs(p_a, wre, wim, layer, tr):
    rows = p_a.shape[0] // S5_CHUNK
    gi, gp = S5_GB * S5_GROUP, S5_GB * S5_STATE
    n_b = S5_GROUPS // (S5_GB * S5_GB_STEP)
    wspec = pl.BlockSpec((None, S5_CHUNK, S5_GB_STEP, gi, gp), lambda i, b: (layer, 0, b, 0, 0))
    ospec = pl.BlockSpec((tr, S5_GB_STEP * gp), lambda i, b: (i, b))
    oshape = jax.ShapeDtypeStruct((rows, S5_GROUPS * S5_STATE), F32)
    return pl.pallas_call(
        _s5_e_kernel,
        out_shape=(oshape, oshape),
        grid=(rows // tr, n_b),
        in_specs=[pl.BlockSpec((S5_CHUNK * tr, S5_GB_STEP * gi), lambda i, b: (i, b)), wspec, wspec],
        out_specs=(ospec, ospec),
        compiler_params=_params("parallel", "parallel"),
        name="s5_chunk_inputs",
    )(p_a, wre, wim)


def _s5_scan_kernel(ere, eim, h0r, h0i, lr_ref, li_ref, ldt_ref, hre, him, fpr, fpi, fsr, fsi,
                    *, n_seq, rows_per_seq):
    sub = V7X_SUBLANES
    cw = ere.shape[1]
    ar, ai = _s5_abar(lr_ref[...], li_ref[...], ldt_ref[...])
    a2 = _cmul(ar, ai, ar, ai)
    p1 = _cmul(*a2, *a2)
    p2 = _cmul(*p1, *p1)
    p4 = _cmul(*p2, *p2)
    p8 = _cmul(*p4, *p4)
    row8 = _row_iota((sub, cw))
    tr_, ti_ = jnp.ones((sub, cw), F32), jnp.zeros((sub, cw), F32)
    for bit, pw in ((1, p1), (2, p2), (4, p4)):
        nr, ni = _cmul(tr_, ti_, *pw)
        sel = (row8 & bit) != 0
        tr_, ti_ = jnp.where(sel, nr, tr_), jnp.where(sel, ni, ti_)

    base = n_seq * rows_per_seq
    n_s = h0r.shape[0]
    h0r_v, h0i_v = h0r[...], h0i[...]
    hre[base:base + n_s, :] = h0r_v
    him[base:base + n_s, :] = h0i_v
    dr, di = _cmul(p1[0], p1[1], h0r_v, h0i_v)
    fsr[...] = dr + ere[base:base + n_s, :]
    fsi[...] = di + eim[base:base + n_s, :]

    for n in range(n_seq):
        def body(j, carry, n=n):
            cr, ci = carry
            r0 = pl.multiple_of(n * rows_per_seq + j * sub, sub)
            xr = ere[pl.ds(r0, sub), :]
            xi = eim[pl.ds(r0, sub), :]
            for k, pw in ((1, p1), (2, p2), (4, p4)):
                keep = row8 >= k
                sr = jnp.where(keep, _shift_rows(xr, k), 0.0)
                si = jnp.where(keep, _shift_rows(xi, k), 0.0)
                mr, mi = _cmul(pw[0], pw[1], sr, si)
                xr, xi = xr + mr, xi + mi
            er = jnp.where(row8 >= 1, _shift_rows(xr, 1), 0.0)
            ei = jnp.where(row8 >= 1, _shift_rows(xi, 1), 0.0)
            qr, qi = _cmul(tr_, ti_, cr, ci)
            hre[pl.ds(r0, sub), :] = er + qr
            him[pl.ds(r0, sub), :] = ei + qi
            nr, ni = _cmul(p8[0], p8[1], cr, ci)
            return xr[sub - 1:sub] + nr, xi[sub - 1:sub] + ni

        zero = jnp.zeros((1, cw), F32)
        cr, ci = lax.fori_loop(0, rows_per_seq // sub, body, (zero, zero))
        fpr[n:n + 1, :] = cr
        fpi[n:n + 1, :] = ci


def s5_scan(ere, eim, h0r, h0i, lam_re_row, lam_im_row, log_dt_row, layer, *, n_seq, rows_per_seq, cw):
    rows, width = ere.shape
    n_s = h0r.shape[0]
    col = lambda r: pl.BlockSpec((r, cw), lambda j: (0, j))
    prow = pl.BlockSpec((None, 1, cw), lambda j: (layer, 0, j))
    shp = lambda r: jax.ShapeDtypeStruct((r, width), F32)
    return pl.pallas_call(
        functools.partial(_s5_scan_kernel, n_seq=n_seq, rows_per_seq=rows_per_seq),
        out_shape=(shp(rows), shp(rows), shp(n_seq), shp(n_seq), shp(n_s), shp(n_s)),
        grid=(width // cw,),
        in_specs=[col(rows), col(rows), col(n_s), col(n_s), prow, prow, prow],
        out_specs=(col(rows), col(rows), col(n_seq), col(n_seq), col(n_s), col(n_s)),
        compiler_params=_params("parallel"),
        name="s5_scan",
    )(ere, eim, h0r, h0i, lam_re_row, lam_im_row, log_dt_row)


def _s5_y_kernel(u_ref, m_ref, hre_ref, him_ref, vre_ref, vim_ref, y_ref):
    gi, gp = S5_GB * S5_GROUP, S5_GB * S5_STATE
    us = _s5_steps(u_ref)
    n = hre_ref.shape[0]
    for half in range(S5_GB_STEP):
        hs = slice(half * gp, (half + 1) * gp)
        ls = slice(half * gi, (half + 1) * gi)
        hr = hre_ref[:, hs].astype(BF16)
        hi = him_ref[:, hs].astype(BF16)
        zero_m = jnp.zeros((gi, gi), BF16)
        for t0 in range(0, S5_CHUNK, 2):
            t1 = t0 + 1
            lhs = jnp.concatenate([hr, hi] + [us[s][:, ls] for s in range(t1 + 1)], axis=1)
            rows = [jnp.concatenate([vre_ref[t0, half], vre_ref[t1, half]], axis=1),
                    jnp.concatenate([vim_ref[t0, half], vim_ref[t1, half]], axis=1)]
            for s in range(t1 + 1):
                left = m_ref[t0 - s, half] if s <= t0 else zero_m
                rows.append(jnp.concatenate([left, m_ref[t1 - s, half]], axis=1))
            acc = _dot(lhs, jnp.concatenate(rows, axis=0))
            y_ref[pl.ds(t0, n, stride=S5_CHUNK), ls] = acc[:, :gi]
            y_ref[pl.ds(t1, n, stride=S5_CHUNK), ls] = acc[:, gi:]


def s5_outputs(p_a, m, hre, him, vre, vim, layer, tr):
    rows = p_a.shape[0] // S5_CHUNK
    gi, gp = S5_GB * S5_GROUP, S5_GB * S5_STATE
    n_b = S5_GROUPS // (S5_GB * S5_GB_STEP)
    mspec = pl.BlockSpec((None, S5_CHUNK, S5_GB_STEP, gi, gi), lambda i, b: (layer, 0, b, 0, 0))
    vspec = pl.BlockSpec((None, S5_CHUNK, S5_GB_STEP, gp, gi), lambda i, b: (layer, 0, b, 0, 0))
    hspec = pl.BlockSpec((tr, S5_GB_STEP * gp), lambda i, b: (i, b))
    uspec = pl.BlockSpec((S5_CHUNK * tr, S5_GB_STEP * gi), lambda i, b: (i, b))
    return pl.pallas_call(
        _s5_y_kernel,
        out_shape=jax.ShapeDtypeStruct((p_a.shape[0], BRANCH_W), F32),
        grid=(rows // tr, n_b),
        in_specs=[uspec, mspec, hspec, hspec, vspec, vspec],
        out_specs=uspec,
        compiler_params=_params("parallel", "parallel"),
        name="s5_outputs",
    )(p_a, m, hre, him, vre, vim)


def _s5_epilogue_kernel(y_ref, u_ref, g_ref, d_ref, w_ref, o_ref):
    z = _gelu_tanh(y_ref[...] + d_ref[...] * u_ref[...])
    o = z * _sigmoid(_dot(z.astype(BF16), w_ref[...]))
    o_ref[...] = (o * _silu(g_ref[...])).astype(o_ref.dtype)


def s5_epilogue(y, p_a, d_skip, w_glu, tr):
    rows, w = y.shape
    return pl.pallas_call(
        _s5_epilogue_kernel,
        out_shape=jax.ShapeDtypeStruct((N_BRANCH, rows, w), BF16),
        grid=(rows // tr,),
        in_specs=[pl.BlockSpec((tr, w), lambda i: (i, 0)),
                  pl.BlockSpec((tr, w), lambda i: (i, 0)),
                  pl.BlockSpec((tr, w), lambda i: (i, 1)),
                  pl.BlockSpec((1, w), lambda i: (0, 0)),
                  pl.BlockSpec((w, w), lambda i: (0, 0))],
        out_specs=pl.BlockSpec((None, tr, w), lambda i: (0, i, 0)),
        compiler_params=_params("parallel"),
        name="s5_epilogue",
    )(y, p_a, p_a, d_skip, w_glu)


_IN_WIDTHS = (("a_x", 1024), ("a_g", 1024), ("b_q", 1024), ("b_f", 1024), ("b_i", 1024), ("b_g", 1024),
              ("c_q", 512), ("c_k", 512), ("c_v", 1024), ("c_r", 16), ("c_g", 1024),
              ("d_x", 1024), ("d_g", 1024), ("e_q", 1024), ("e_g", 1024))

TM_NORM = 512
TM_MM = 1088
TN_MM = 2048
TN_CQ = 1280
TM_MERGE = 1088
TN_MERGE = 1024
TM_OUT = 544
TR_S5 = 1088
TR_S5_EPI = 544
CW_S5_SCAN = 512
CHUNK_GLA = 128
SUB_GLA = 2
NB_GLA_SAMPLE = 8
CHUNK_LRU = 256
ROWS_MEM_PROMPT = 512
NB_MEM_SAMPLE = 8


def _in_cols(w_in_l):
    cols, off = {}, 0
    for name, width in _IN_WIDTHS:
        cols[name] = w_in_l[:, off:off + width]
        off += width
    return cols


def kernel(x_prompt, x_sample, mem_prompt, state_s5_re, state_s5_im, state_hgrn, state_gla, state_rglru, state_conv, cache_mem_k, cache_mem_v, norm_pre, norm_post, w_in, s5_lam_re, s5_lam_im, s5_log_dt, s5_b_re, s5_b_im, s5_c_re, s5_c_im, s5_d, s5_w_glu, hg_lb_logits, hg_norm, gla_w_up, gla_b_up, gla_norm, conv_w, conv_b, lru_w_r, lru_b_r, lru_w_i, lru_b_i, lru_lam, mem_norm, w_mem_k, w_mem_v, w_branch, w_merge, w_out):
    d, w = D_MODEL, BRANCH_W
    m_p, m_s = BATCH * SEQ, DEC_BATCH * DEC_SEQ
    m_all = m_p + m_s
    bf = lambda a: a.astype(BF16)
    row = lambda v: v[None, :]

    x = jnp.concatenate([x_prompt.reshape(m_p, d), x_sample.reshape(m_s, d)], axis=0)

    mem2 = mem_prompt.reshape(BATCH * N_MEM, d)
    mk_l, mv_l = [], []
    for l in range(DEPTH):
        mn = rmsnorm_rows(mem2, row(mem_norm[l]), TM_NORM)
        mk_l.append(matmul(mn, bf(w_mem_k[l]), BATCH * N_MEM, w // 2))
        mv_l.append(matmul(mn, bf(w_mem_v[l]), BATCH * N_MEM, w // 2))

    lb = hg_lower_bounds(hg_lb_logits)
    s5_m, s5_wre, s5_wim, s5_vre, s5_vim = s5_prepare(s5_lam_re, s5_lam_im, s5_log_dt, s5_b_re, s5_b_im,
                                                      s5_c_re, s5_c_im)
    gs = S5_GROUPS * S5_STATE
    lam_re_row = s5_lam_re.reshape(DEPTH, 1, gs)
    lam_im_row = s5_lam_im.reshape(DEPTH, 1, gs)
    log_dt_row = jnp.broadcast_to(s5_log_dt[:, :, None], (DEPTH, S5_GROUPS, S5_STATE)).reshape(DEPTH, 1, gs)
    s5_h0r = state_s5_re.reshape(DEPTH, DEC_BATCH, gs)
    s5_h0i = state_s5_im.reshape(DEPTH, DEC_BATCH, gs)
    w_up_pad = jnp.zeros((DEPTH, V7X_LANES, GLA_KEY_W), F32).at[:, :GLA_RANK].set(gla_w_up)

    hg_p = hg_s = gl_p = gl_s = None
    outs = {k: [] for k in ("p_s5r", "p_s5i", "p_lru", "p_conv", "s_s5r", "s_s5i", "s_lru", "s_conv")}
    for l in range(DEPTH):
        cols = _in_cols(w_in[l])
        w_a = bf(jnp.concatenate([cols["a_x"], cols["a_g"]], axis=1))
        w_b = bf(jnp.concatenate([cols["b_q"], cols["b_f"], cols["b_i"], cols["b_g"]], axis=1))
        w_cv = bf(jnp.concatenate([cols["c_v"], cols["c_g"]], axis=1))
        w_cq = bf(jnp.concatenate([cols["c_q"], cols["c_k"], cols["c_r"],
                                   jnp.zeros((d, TN_CQ - 2 * GLA_KEY_W - GLA_RANK), F32)], axis=1))
        w_d = bf(jnp.concatenate([cols["d_x"], cols["d_g"]], axis=1))
        w_e = bf(jnp.concatenate([cols["e_q"], cols["e_g"]], axis=1))

        if l == 0:
            h = rmsnorm_rows(x, row(norm_pre[0]), TM_NORM)
        p_a = matmul(h, w_a, TM_MM, TN_MM)
        p_b = matmul(h, w_b, TM_MM, TN_MM)
        p_cv = matmul(h, w_cv, TM_MM, TN_MM)
        p_cq = matmul(h, w_cq, TM_MM, TN_CQ)
        p_d = matmul(h, w_d, TM_MM, TN_MM)
        p_e = matmul(h, w_e, TM_MM, TN_MM)

        ere, eim = s5_chunk_inputs(p_a, s5_wre, s5_wim, l, TR_S5)
        hre, him, fpr, fpi, fsr, fsi = s5_scan(ere, eim, s5_h0r[l], s5_h0i[l], lam_re_row, lam_im_row, log_dt_row,
                                               l, n_seq=BATCH, rows_per_seq=SEQ // S5_CHUNK, cw=CW_S5_SCAN)
        y = s5_outputs(p_a, s5_m, hre, him, s5_vre, s5_vim, l, TR_S5)
        br = s5_epilogue(y, p_a, row(s5_d[l]), bf(s5_w_glu[l]), TR_S5_EPI)
        outs["p_s5r"].append(fpr); outs["p_s5i"].append(fpi)
        outs["s_s5r"].append(fsr); outs["s_s5i"].append(fsi)

        hg_consts = [row(lb[l]), row(hg_norm[l])]
        hg_kw = dict(layer=l, slab=1, heads=HG_HEADS, dk=HG_DK, dv=HG_DV)
        hg_cols = [(p_b, w, 0), (p_b, w, 1), (p_b, w, 2), (p_b, w, 3)]
        br, hg_p = gla_branch("hgrn", hg_cols, hg_consts, None, br, hg_p, row_base=0, n_seq=BATCH,
                              seq_total=SEQ, chunk=CHUNK_GLA, nb=1, sub=SUB_GLA, **hg_kw)
        br, hg_s = gla_branch("hgrn", hg_cols, hg_consts, state_hgrn, br, hg_s, row_base=m_p,
                              n_seq=DEC_BATCH, seq_total=DEC_SEQ, chunk=DEC_SEQ, nb=NB_GLA_SAMPLE, **hg_kw)

        gl_consts = [w_up_pad[l], row(gla_b_up[l]), row(gla_norm[l])]
        gl_kw = dict(layer=l, slab=2, heads=GLA_HEADS, dk=GLA_DK, dv=GLA_DV)
        gl_cols = [(p_cq, GLA_KEY_W, 0), (p_cq, GLA_KEY_W, 1), (p_cv, w, 0), (p_cq, V7X_LANES, w // V7X_LANES),
                   (p_cv, w, 1)]
        br, gl_p = gla_branch("gla", gl_cols, gl_consts, None, br, gl_p, row_base=0, n_seq=BATCH,
                              seq_total=SEQ, chunk=CHUNK_GLA, nb=1, sub=SUB_GLA, **gl_kw)
        br, gl_s = gla_branch("gla", gl_cols, gl_consts, state_gla, br, gl_s, row_base=m_p,
                              n_seq=DEC_BATCH, seq_total=DEC_SEQ, chunk=DEC_SEQ, nb=NB_GLA_SAMPLE, **gl_kw)

        lru_consts = [conv_w[l], row(conv_b[l]), bf(lru_w_r[l]), row(lru_b_r[l]), bf(lru_w_i[l]), row(lru_b_i[l]),
                      row(lru_lam[l])]
        br, hfin_p, cfin_p = lru_prompt(p_d, lru_consts, br, slab=3, n_seq=BATCH, seq_total=SEQ, chunk=CHUNK_LRU)
        br, hfin_s, cfin_s = lru_sample(p_d, jnp.transpose(state_conv[l], (1, 0, 2)), state_rglru[l],
                                        lru_consts, br, slab=3, row_base=m_p)
        outs["p_lru"].append(hfin_p); outs["p_conv"].append(cfin_p)
        outs["s_lru"].append(hfin_s); outs["s_conv"].append(cfin_s)

        br = mem_attention(p_e, mk_l[l].reshape(1, BATCH, N_MEM, w), mv_l[l].reshape(1, BATCH, N_MEM, w), br,
                           layer=0, slab=4, row_base=0, n_seq=BATCH, seq_total=SEQ, rows_blk=ROWS_MEM_PROMPT, nbm=1)
        br = mem_attention_cache(p_e, cache_mem_k, cache_mem_v, br, layer=l, slab=4, row_base=m_p, n_seq=DEC_BATCH,
                                 seq_total=DEC_SEQ, nbm=NB_MEM_SAMPLE)

        merged = merge_branches(h, br, bf(w_merge[l]), bf(w_branch[l]), TM_MERGE, TN_MERGE)
        if l + 1 < DEPTH:
            x, h = out_proj_residual(merged, bf(w_out[l]), row(norm_post[l]), x, row(norm_pre[l + 1]), TM_OUT)
        else:
            x = out_proj_residual(merged, bf(w_out[l]), row(norm_post[l]), x, None, TM_OUT)

    st = lambda k, shape: jnp.stack(outs[k]).reshape(shape)
    s5_p_shape = (DEPTH, BATCH, S5_GROUPS, S5_STATE)
    s5_s_shape = (DEPTH, DEC_BATCH, S5_GROUPS, S5_STATE)
    mem_shape = (DEPTH, BATCH, N_MEM, MEM_HEADS, MEM_HD)
    return (x[:m_p].reshape(BATCH, SEQ, d), x[m_p:].reshape(DEC_BATCH, DEC_SEQ, d),
            st("p_s5r", s5_p_shape), st("p_s5i", s5_p_shape), hg_p, gl_p,
            st("p_lru", (DEPTH, BATCH, w)), st("p_conv", (DEPTH, BATCH, CONV_W - 1, w)),
            jnp.stack(mk_l).reshape(mem_shape), jnp.stack(mv_l).reshape(mem_shape),
            st("s_s5r", s5_s_shape), st("s_s5i", s5_s_shape), hg_s, gl_s,
            st("s_lru", (DEPTH, DEC_BATCH, w)), jnp.transpose(jnp.stack(outs["s_conv"]), (0, 2, 1, 3)))
```

```python
import functools
import math

import jax
import jax.numpy as jnp
from jax import lax
from jax.experimental import pallas as pl
from jax.experimental.pallas import tpu as pltpu

F32 = jnp.float32
BF16 = jnp.bfloat16
HIGHEST = lax.Precision.HIGHEST

V7X_LANES = 128
V7X_SUBLANES = 8
V7X_VMEM_LIMIT_BYTES = 56 * 1024 * 1024
V7X_VMEM_LIMIT_MERGE_BYTES = 58 * 1024 * 1024

EPS = 1e-6
D_MODEL = 2048
DEPTH = 4
BATCH = 4
SEQ = 2048
DEC_BATCH = 128
DEC_SEQ = 4
BRANCH_W = 1024
N_BRANCH = 5
S5_GROUP = 16
S5_GROUPS = 64
S5_STATE = 64
S5_CHUNK = 4
S5_GB = 8
HG_HEADS = 8
HG_DK = 128
HG_DV = 128
GLA_HEADS = 4
GLA_KEY_W = 512
GLA_DK = 128
GLA_DV = 256
GLA_RANK = 16
GLA_GATE_TEMP = 16.0
LRU_BLOCKS = 8
LRU_BW = 128
CONV_W = 4
LRU_C = 8.0
N_MEM = 256
MEM_HEADS = 4
MEM_HD = 256


def _params(*sem):
    return pltpu.CompilerParams(dimension_semantics=sem, vmem_limit_bytes=V7X_VMEM_LIMIT_BYTES)


def _dot(a, b):
    return jnp.dot(a, b, preferred_element_type=F32)


def _dot_nt(a, b):
    return lax.dot_general(a, b, (((1,), (1,)), ((), ())), preferred_element_type=F32)


def _dot_tn(a, b):
    return lax.dot_general(a, b, (((0,), (0,)), ((), ())), preferred_element_type=F32)


def _sigmoid(x):
    return 1.0 / (1.0 + jnp.exp(-x))


def _silu(x):
    return x * _sigmoid(x)


def _log1p_exp_neg_abs(x):
    return jnp.log(1.0 + jnp.exp(-jnp.abs(x)))


def _log_sigmoid(x):
    return jnp.minimum(x, 0.0) - _log1p_exp_neg_abs(x)


def _softplus(x):
    return jnp.maximum(x, 0.0) + _log1p_exp_neg_abs(x)


def _gelu_tanh(x):
    return 0.5 * x * (1.0 + jnp.tanh(math.sqrt(2.0 / math.pi) * (x + 0.044715 * (x * x * x))))


def _row_iota(shape):
    return lax.broadcasted_iota(jnp.int32, shape, 0)


def _col_iota(shape):
    return lax.broadcasted_iota(jnp.int32, shape, 1)


def _shift_rows(x, k):
    return pltpu.roll(x, k, 0)


def _shift_rows_up(x, k):
    return pltpu.roll(x, x.shape[0] - k, 0)


def _rmsnorm_kernel(x_ref, g_ref, o_ref):
    x = x_ref[...]
    y = x * lax.rsqrt(jnp.mean(x * x, axis=-1, keepdims=True) + EPS)
    o_ref[...] = (y * g_ref[...]).astype(o_ref.dtype)


def rmsnorm_rows(x, g, tm):
    m, d = x.shape
    return pl.pallas_call(
        _rmsnorm_kernel,
        out_shape=jax.ShapeDtypeStruct((m, d), BF16),
        grid=(m // tm,),
        in_specs=[pl.BlockSpec((tm, d), lambda i: (i, 0)), pl.BlockSpec((1, d), lambda i: (0, 0))],
        out_specs=pl.BlockSpec((tm, d), lambda i: (i, 0)),
        compiler_params=_params("parallel"),
        name="rmsnorm_rows",
    )(x, g)


def _matmul_kernel(a_ref, w_ref, o_ref):
    o_ref[...] = _dot(a_ref[...], w_ref[...]).astype(o_ref.dtype)


def matmul(a, w, tm, tn, out_dtype=F32):
    m, k = a.shape
    n = w.shape[1]
    return pl.pallas_call(
        _matmul_kernel,
        out_shape=jax.ShapeDtypeStruct((m, n), out_dtype),
        grid=(m // tm, n // tn),
        in_specs=[pl.BlockSpec((tm, k), lambda i, j: (i, 0)), pl.BlockSpec((k, tn), lambda i, j: (0, j))],
        out_specs=pl.BlockSpec((tm, tn), lambda i, j: (i, j)),
        compiler_params=_params("parallel", "parallel"),
        name="matmul",
    )(a, w)


def _merge_kernel(h_ref, b_ref, wm_ref, wb_ref, o_ref, acc_ref):
    c = pl.program_id(1)
    j = pl.program_id(2)
    nj, _, tn = acc_ref.shape
    contrib = _sigmoid(_dot(h_ref[...], wm_ref[...])) * _dot(b_ref[...], wb_ref[...])

    @pl.when(c == 0)
    def _():
        acc_ref[j] = contrib

    @pl.when(jnp.logical_and(c > 0, c < N_BRANCH - 1))
    def _():
        acc_ref[j] += contrib

    for jj in range(nj):
        @pl.when(jnp.logical_and(c == N_BRANCH - 1, j == jj))
        def _(jj=jj):
            o_ref[:, jj * tn:(jj + 1) * tn] = (acc_ref[jj] + contrib).astype(o_ref.dtype)


def merge_branches(h, branches, w_merge, w_branch, tm, tn):
    m, d = h.shape
    w = branches.shape[2]
    nj = d // tn
    return pl.pallas_call(
        _merge_kernel,
        out_shape=jax.ShapeDtypeStruct((m, d), BF16),
        grid=(m // tm, N_BRANCH, nj),
        in_specs=[pl.BlockSpec((tm, d), lambda i, c, j: (i, 0), pipeline_mode=pl.Buffered(1)),
                  pl.BlockSpec((None, tm, w), lambda i, c, j: (c, i, 0)),
                  pl.BlockSpec((d, tn), lambda i, c, j: (0, c * nj + j)),
                  pl.BlockSpec((None, w, tn), lambda i, c, j: (c, 0, j))],
        out_specs=pl.BlockSpec((tm, d), lambda i, c, j: (i, 0), pipeline_mode=pl.Buffered(1)),
        scratch_shapes=[pltpu.VMEM((nj, tm, tn), F32)],
        compiler_params=pltpu.CompilerParams(dimension_semantics=("parallel", "arbitrary", "arbitrary"),
                                             vmem_limit_bytes=V7X_VMEM_LIMIT_MERGE_BYTES),
        name="merge_branches",
    )(h, branches, w_merge, w_branch)


def _out_kernel(m_ref, w_ref, g_ref, x_ref, *rest):
    y = _dot(m_ref[...], w_ref[...])
    y = y * lax.rsqrt(jnp.mean(y * y, axis=-1, keepdims=True) + EPS)
    xn = x_ref[...] + y * g_ref[...]
    if len(rest) == 1:
        (o_ref,) = rest
    else:
        gn_ref, o_ref, h_ref = rest
        hn = xn * lax.rsqrt(jnp.mean(xn * xn, axis=-1, keepdims=True) + EPS)
        h_ref[...] = (hn * gn_ref[...]).astype(h_ref.dtype)
    o_ref[...] = xn


def out_proj_residual(merged, w_out, g, x, g_next, tm):
    m, d = x.shape
    rows = pl.BlockSpec((tm, d), lambda i: (i, 0))
    vec = pl.BlockSpec((1, d), lambda i: (0, 0))
    in_specs = [rows, pl.BlockSpec((d, d), lambda i: (0, 0)), vec, rows]
    args = [merged, w_out, g, x]
    out_shape = jax.ShapeDtypeStruct((m, d), F32)
    out_specs = rows
    if g_next is not None:
        in_specs.append(vec)
        args.append(g_next)
        out_shape = (out_shape, jax.ShapeDtypeStruct((m, d), BF16))
        out_specs = (rows, rows)
    return pl.pallas_call(
        _out_kernel,
        out_shape=out_shape,
        grid=(m // tm,),
        in_specs=in_specs,
        out_specs=out_specs,
        compiler_params=_params("parallel"),
        name="out_proj_residual",
    )(*args)


def _hg_lb_kernel(logit_ref, lb_ref):
    x = logit_ref[...]
    e = jnp.exp(x - jnp.max(x, axis=0, keepdims=True))
    sm = e / jnp.sum(e, axis=0, keepdims=True)
    acc = jnp.zeros_like(sm[0:1])
    lb_ref[0:1, :] = acc
    for l in range(1, DEPTH):
        acc = acc + sm[l:l + 1]
        lb_ref[l:l + 1, :] = acc


def hg_lower_bounds(logits):
    return pl.pallas_call(
        _hg_lb_kernel,
        out_shape=jax.ShapeDtypeStruct(logits.shape, F32),
        name="hg_lower_bounds",
    )(logits)


def _gla_kernel(*refs, mode, seq_len, nb, heads, dk, dv, has_init, sub):
    n_row = 4 if mode == "hgrn" else 5
    n_const = 2 if mode == "hgrn" else 3
    row_refs, consts, rest = refs[:n_row], refs[n_row:n_row + n_const], refs[n_row + n_const:]
    if has_init:
        s0_ref, rest = rest[0], rest[1:]
    o_ref, sout_ref, s_scr = rest[:3]
    scratch = rest[3:]

    @pl.when(pl.program_id(1) == 0)
    def _():
        if has_init:
            s_scr[...] = s0_ref[...]
        else:
            s_scr[...] = jnp.zeros_like(s_scr)

    rows = nb * seq_len
    for sc in range(sub):
        window = pl.ds(sc * rows, rows)
        _gla_chunk(*[r.at[window] for r in row_refs], *consts, o_ref.at[window], s_scr, *scratch,
                   mode=mode, seq_len=seq_len, nb=nb, heads=heads, dk=dk, dv=dv)

    @pl.when(pl.program_id(1) == pl.num_programs(1) - 1)
    def _():
        sout_ref[...] = s_scr[...]


def _gla_chunk(*refs, mode, seq_len, nb, heads, dk, dv):
    L = seq_len
    R = nb * L
    C = heads * dk
    if mode == "hgrn":
        q_ref, f_ref, v_ref, g_ref, lb_ref, gn_ref = refs[:6]
        rest = refs[6:]
    else:
        q_ref, k_ref, v_ref, r_ref, g_ref, wup_ref, bup_ref, gn_ref = refs[:8]
        rest = refs[8:]
    o_ref, s_scr, att_scr, qe_scr, kd_scr, b_scr = rest

    t = _row_iota((R, C)) % L
    if mode == "hgrn":
        lb = lb_ref[...]
        fz = f_ref[...]
        la = jnp.log(lb)
        lc = jnp.log1p(-lb) + _log_sigmoid(fz)
        g = jnp.maximum(la, lc) + _log1p_exp_neg_abs(la - lc)
        k = (1.0 - lb) * _sigmoid(-fz)
        q = q_ref[...]
    else:
        pre = jnp.dot(r_ref[...], wup_ref[...], preferred_element_type=F32, precision=HIGHEST)
        g = _log_sigmoid(pre + bup_ref[...]) * (1.0 / GLA_GATE_TEMP)
        k = k_ref[...]
        q = q_ref[...] * (dk ** -0.5)

    rr = _row_iota((R, R))
    cc = _col_iota((R, R))
    same_seq = (rr // L) == (cc // L)
    tri_incl = jnp.where(same_seq, jnp.where(cc <= rr, 1.0, 0.0), 0.0)
    tri_after = jnp.where(same_seq, jnp.where(cc > rr, 1.0, 0.0), 0.0)
    tri = jnp.concatenate([tri_incl, tri_after], axis=0).astype(BF16)
    g_hi = g.astype(BF16)
    g_r1 = g - g_hi.astype(F32)
    g_mid = g_r1.astype(BF16)
    g_lo = (g_r1 - g_mid.astype(F32)).astype(BF16)
    sums = _dot(tri, g_hi) + _dot(tri, g_mid) + _dot(tri, g_lo)
    b = sums[:R]
    rb = sums[R:]
    b_scr[...] = b
    qe_scr[...] = (q * jnp.exp(b)).astype(BF16)
    kd_scr[...] = (k * jnp.exp(rb)).astype(BF16)

    def add_level(level, qt, kt, mask):
        same = kt is qt
        qt = qt.astype(BF16)
        kt = qt if same else kt.astype(BF16)
        for h in range(heads):
            hs = slice(h * dk, (h + 1) * dk)
            p = jnp.where(mask, _dot_nt(qt[:, hs], kt[:, hs]), 0.0)
            if level == 0:
                att_scr[h] = p
            else:
                att_scr[h] += p

    add_level(0, q, k, rr == cc)
    level = 1
    m = 1
    while m < L:
        pos = t % (2 * m)
        upper = pos >= m
        if m == 1:
            d = jnp.where(upper, g, 0.0)
        elif m == 2:
            d = jnp.where(pos == 2, g,
                          jnp.where(pos == 3, g + _shift_rows(g, 1),
                                    jnp.where(pos == 0, _shift_rows_up(g, 1), 0.0)))
        else:
            b3 = b.reshape(R // (2 * m), 2 * m, C)
            d3 = b3 - b3[:, m - 1:m, :]
            d = -jnp.abs(d3.reshape(R, C))
        x = jnp.where(upper, q, k) * jnp.exp(d)
        blk = 2 * m
        pair = jnp.logical_and((rr // blk) == (cc // blk),
                               jnp.logical_and((rr % blk) >= m, (cc % blk) < m))
        add_level(level, x, x, pair)
        level += 1
        m *= 2

    gn = gn_ref[...]
    seq_o = _row_iota((R, dv)) // L
    seq_k = _row_iota((R, dk)) // L
    pad_rows = (-R) % V7X_LANES
    for h in range(heads):
        hs = slice(h * dk, (h + 1) * dk)
        vs = slice(h * dv, (h + 1) * dv)
        vh = v_ref[:, vs].astype(BF16)
        vh_p = jnp.concatenate([vh, jnp.zeros((pad_rows, dv), BF16)], axis=0) if pad_rows else vh
        o = _dot(att_scr[h].astype(BF16), vh)
        qe = qe_scr[:, hs]
        kd = kd_scr[:, hs]
        for j in range(nb):
            st = s_scr[j, h]
            inter = _dot(qe, st.astype(BF16))
            kd_j = kd
            if nb > 1:
                inter = jnp.where(seq_o == j, inter, 0.0)
                kd_j = jnp.where(seq_k == j, kd, jnp.zeros_like(kd))
            o = o + inter
            if pad_rows:
                kd_j = jnp.concatenate([kd_j, jnp.zeros((pad_rows, dk), BF16)], axis=0)
            upd = _dot_tn(kd_j, vh_p)
            dl = jnp.exp(b_scr[j * L + L - 1:j * L + L, hs])
            colb = jnp.broadcast_to(dl, (dk, dk)).T
            if dv != dk:
                colb = jnp.concatenate([colb] * (dv // dk), axis=1)
            s_scr[j, h] = colb * st + upd
        on = o * lax.rsqrt(jnp.mean(o * o, axis=-1, keepdims=True) + EPS) * gn
        o_ref[:, vs] = (on * _silu(g_ref[:, vs])).astype(o_ref.dtype)


def gla_branch(mode, col_blocks, consts, s0, out_prev, sout_prev, *, layer, slab, row_base, n_seq, seq_total,
               chunk, nb, heads, dk, dv, sub=1):
    L = chunk
    R = nb * L
    rows_blk = sub * R
    n_chunks = seq_total // (sub * L)
    base_blk = row_base // rows_blk
    width = heads * dv

    def rows(i, c):
        return base_blk + i * n_chunks + c

    in_specs = [pl.BlockSpec((rows_blk, w), functools.partial(lambda i, c, bi: (rows(i, c), bi), bi=bi))
                for (_, w, bi) in col_blocks]
    args = [a for (a, _, _) in col_blocks]
    for a in consts:
        in_specs.append(pl.BlockSpec(a.shape, lambda i, c: (0, 0)))
        args.append(a)
    has_init = s0 is not None
    state_spec = pl.BlockSpec((None, nb, heads, dk, dv), lambda i, c: (layer, i, 0, 0, 0))
    if has_init:
        in_specs.append(state_spec)
        args.append(s0)
    n_real = len(args)
    aliases = {}
    for out_idx, prev in enumerate((out_prev, sout_prev)):
        if prev is not None:
            aliases[len(args)] = out_idx
            in_specs.append(pl.BlockSpec(memory_space=pl.ANY))
            args.append(prev)
    n_in = len(args)

    def kern(*refs):
        _gla_kernel(*refs[:n_real], *refs[n_in:], mode=mode, seq_len=L, nb=nb, heads=heads, dk=dk, dv=dv,
                    has_init=has_init, sub=sub)

    return pl.pallas_call(
        kern,
        out_shape=(jax.ShapeDtypeStruct(out_prev.shape, BF16),
                   jax.ShapeDtypeStruct((DEPTH, n_seq, heads, dk, dv), F32)),
        grid=(n_seq // nb, n_chunks),
        in_specs=in_specs,
        out_specs=(pl.BlockSpec((None, rows_blk, width), lambda i, c: (slab, rows(i, c), 0)), state_spec),
        scratch_shapes=[pltpu.VMEM((nb, heads, dk, dv), F32), pltpu.VMEM((heads, R, R), F32),
                        pltpu.VMEM((R, heads * dk), BF16), pltpu.VMEM((R, heads * dk), BF16),
                        pltpu.VMEM((R, heads * dk), F32)],
        input_output_aliases=aliases,
        compiler_params=_params("parallel", "arbitrary"),
        name=mode + "_branch",
    )(*args)


def _lru_gates(xc, wr_ref, br_ref, wi_ref, bi_ref, lam_ref):
    xcb = xc.astype(BF16)
    r_parts, i_parts = [], []
    for blk in range(xc.shape[1] // LRU_BW):
        bs = slice(blk * LRU_BW, (blk + 1) * LRU_BW)
        r_parts.append(_dot(xcb[:, bs], wr_ref[blk]))
        i_parts.append(_dot(xcb[:, bs], wi_ref[blk]))
    r = _sigmoid(jnp.concatenate(r_parts, axis=1) + br_ref[...])
    ig = _sigmoid(jnp.concatenate(i_parts, axis=1) + bi_ref[...])
    log_a = (-LRU_C) * r * _softplus(-lam_ref[...])
    a = jnp.exp(log_a)
    u = jnp.sqrt(1.0 - a * a) * (ig * xc)
    return a, u


def _lru_prompt_kernel(x_ref, g_ref, cw_ref, cb_ref, wr_ref, br_ref, wi_ref, bi_ref, lam_ref,
                       o_ref, hfin_ref, cfin_ref, tail_scr, hc_scr, a_scr, u_scr):
    tc, w = x_ref.shape
    sub = V7X_SUBLANES
    c = pl.program_id(1)

    @pl.when(c == 0)
    def _():
        tail_scr[...] = jnp.zeros_like(tail_scr)
        hc_scr[...] = jnp.zeros_like(hc_scr)

    x = x_ref[...]
    tail = tail_scr[...]
    row8 = _row_iota((sub, w))
    cw = cw_ref[...]
    xc = cb_ref[...] + cw[CONV_W - 1:CONV_W] * x
    for k in range(1, CONV_W):
        xs = _shift_rows(x, k)
        head = jnp.where(row8 < k, _shift_rows(tail, k), xs[0:sub])
        xs = jnp.concatenate([head, xs[sub:]], axis=0)
        xc = xc + cw[CONV_W - 1 - k:CONV_W - k] * xs
    last8 = x[tc - sub:tc]
    tail_scr[...] = last8

    a, u = _lru_gates(xc, wr_ref, br_ref, wi_ref, bi_ref, lam_ref)
    a_scr[...] = a
    u_scr[...] = u

    def body(i, carry):
        r0 = pl.multiple_of(i * sub, sub)
        at = a_scr[pl.ds(r0, sub), :]
        ut = u_scr[pl.ds(r0, sub), :]
        for k in (1, 2, 4):
            keep = row8 >= k
            ut = ut + at * jnp.where(keep, _shift_rows(ut, k), 0.0)
            at = at * jnp.where(keep, _shift_rows(at, k), 1.0)
        ht = ut + at * carry
        u_scr[pl.ds(r0, sub), :] = ht
        return jnp.broadcast_to(ht[sub - 1:sub], (sub, w))

    carry = lax.fori_loop(0, tc // sub, body, hc_scr[...])
    hc_scr[...] = carry
    o_ref[...] = (u_scr[...] * _silu(g_ref[...])).astype(o_ref.dtype)

    @pl.when(c == pl.num_programs(1) - 1)
    def _():
        hfin_ref[...] = carry[0:1]
        cfin_ref[...] = _shift_rows(last8, CONV_W - 1)[0:CONV_W - 1]


def lru_prompt(p_d, consts, out_prev, *, slab, n_seq, seq_total, chunk):
    w = BRANCH_W
    n_chunks = seq_total // chunk
    const_specs = [pl.BlockSpec(a.shape, functools.partial(lambda i, c, nd: (0,) * nd, nd=a.ndim)) for a in consts]

    def kern(*refs):
        n_in = 2 + len(consts)
        _lru_prompt_kernel(*refs[:n_in], *refs[n_in + 1:])

    return pl.pallas_call(
        kern,
        out_shape=(jax.ShapeDtypeStruct(out_prev.shape, BF16),
                   jax.ShapeDtypeStruct((n_seq, 1, w), F32),
                   jax.ShapeDtypeStruct((n_seq, CONV_W - 1, w), F32)),
        grid=(n_seq, n_chunks),
        in_specs=[pl.BlockSpec((chunk, w), lambda i, c: (i * n_chunks + c, 0)),
                  pl.BlockSpec((chunk, w), lambda i, c: (i * n_chunks + c, 1))] + const_specs
        + [pl.BlockSpec(memory_space=pl.ANY)],
        out_specs=(pl.BlockSpec((None, chunk, w), lambda i, c: (slab, i * n_chunks + c, 0)),
                   pl.BlockSpec((None, 1, w), lambda i, c: (i, 0, 0)),
                   pl.BlockSpec((None, CONV_W - 1, w), lambda i, c: (i, 0, 0))),
        scratch_shapes=[pltpu.VMEM((V7X_SUBLANES, w), F32), pltpu.VMEM((V7X_SUBLANES, w), F32),
                        pltpu.VMEM((chunk, w), F32), pltpu.VMEM((chunk, w), F32)],
        input_output_aliases={2 + len(consts): 0},
        compiler_params=_params("parallel", "arbitrary"),
        name="lru_prompt",
    )(p_d, p_d, *consts, out_prev)


def _lru_sample_kernel(x_ref, g_ref, buf_ref, h0_ref, cw_ref, cb_ref, wr_ref, br_ref, wi_ref, bi_ref, lam_ref,
                       prev_ref, o_ref, hfin_ref, cfin_ref, o_scr):
    del prev_ref
    steps = DEC_SEQ
    n = x_ref.shape[0] // steps
    step_rows = lambda t: pl.ds(t, n, stride=steps)
    xcat = [buf_ref[j] for j in range(CONV_W - 1)] + [x_ref[step_rows(t), :] for t in range(steps)]
    cw = cw_ref[...]
    xcs = []
    for t in range(steps):
        xc = cb_ref[...]
        for j in range(CONV_W):
            xc = xc + cw[j:j + 1] * xcat[t + j]
        xcs.append(xc)
    a, u = _lru_gates(jnp.concatenate(xcs, axis=0), wr_ref, br_ref, wi_ref, bi_ref, lam_ref)
    h = h0_ref[...]
    for t in range(steps):
        h = a[t * n:(t + 1) * n] * h + u[t * n:(t + 1) * n]
        o_scr[step_rows(t), :] = h * _silu(g_ref[step_rows(t), :])
    o_ref[...] = o_scr[...].astype(o_ref.dtype)
    hfin_ref[...] = h
    for j in range(CONV_W - 1):
        cfin_ref[j] = xcat[steps + j]


def lru_sample(p_d, conv_buf, h0, consts, out_prev, *, slab, row_base):
    w = BRANCH_W
    bw = LRU_BW
    n = h0.shape[0]
    rows = n * DEC_SEQ
    blk = row_base // rows
    nbk = w // bw
    conv_w_, conv_b_, w_r, b_r, w_i, b_i, lam = consts
    lane = lambda r: pl.BlockSpec((r, bw), lambda i: (0, i))
    wblk = pl.BlockSpec((1, bw, bw), lambda i: (i, 0, 0))
    tail = pl.BlockSpec((CONV_W - 1, n, bw), lambda i: (0, 0, i))
    return pl.pallas_call(
        _lru_sample_kernel,
        out_shape=(jax.ShapeDtypeStruct(out_prev.shape, BF16),
                   jax.ShapeDtypeStruct((n, w), F32),
                   jax.ShapeDtypeStruct((CONV_W - 1, n, w), F32)),
        grid=(nbk,),
        in_specs=[pl.BlockSpec((rows, bw), lambda i: (blk, i)), pl.BlockSpec((rows, bw), lambda i: (blk, nbk + i)),
                  tail, lane(n), lane(CONV_W), lane(1), wblk, lane(1), wblk, lane(1), lane(1),
                  pl.BlockSpec(memory_space=pl.ANY)],
        out_specs=(pl.BlockSpec((None, rows, bw), lambda i: (slab, blk, i)), lane(n), tail),
        scratch_shapes=[pltpu.VMEM((rows, bw), F32)],
        input_output_aliases={11: 0},
        compiler_params=_params("parallel"),
        name="lru_sample",
    )(p_d, p_d, conv_buf, h0, conv_w_, conv_b_, w_r, b_r, w_i, b_i, lam, out_prev)


def _mem_attn_kernel(q_ref, g_ref, k_ref, v_ref, prev_ref, o_ref, s_scr, p_scr, *, nbm, seq_rows):
    del prev_ref
    rows = q_ref.shape[0]
    q = (q_ref[...] * (MEM_HD ** -0.5)).astype(BF16)
    for h in range(MEM_HEADS):
        hs = slice(h * MEM_HD, (h + 1) * MEM_HD)
        for j in range(nbm):
            pair = h * nbm + j
            s_scr[pair * rows:(pair + 1) * rows, :] = _dot_nt(q[:, hs], k_ref[j, :, hs].astype(BF16))
    s = s_scr[...]
    e = jnp.exp(s - jnp.max(s, axis=-1, keepdims=True))
    p_scr[...] = (e / jnp.sum(e, axis=-1, keepdims=True)).astype(BF16)
    seq = _row_iota((rows, MEM_HD)) // seq_rows
    for h in range(MEM_HEADS):
        hs = slice(h * MEM_HD, (h + 1) * MEM_HD)
        acc = None
        for j in range(nbm):
            pair = h * nbm + j
            oh = _dot(p_scr[pair * rows:(pair + 1) * rows, :], v_ref[j, :, hs].astype(BF16))
            acc = oh if acc is None else jnp.where(seq == j, oh, acc)
        o_ref[:, hs] = (acc * _silu(g_ref[:, hs])).astype(o_ref.dtype)


def mem_attention(p_e, mem_k, mem_v, out_prev, *, layer, slab, row_base, n_seq, seq_total, rows_blk, nbm):
    w = BRANCH_W
    base_blk = row_base // rows_blk
    if nbm == 1:
        t_blocks = seq_total // rows_blk
        grid = (n_seq, t_blocks)
        rmap = lambda i, t: base_blk + i * t_blocks + t
    else:
        grid = (n_seq // nbm, 1)
        rmap = lambda i, t: base_blk + i
    mem_spec = pl.BlockSpec((None, nbm, N_MEM, w), lambda i, t: (layer, i, 0, 0))
    in_specs = [pl.BlockSpec((rows_blk, w), lambda i, t: (rmap(i, t), 0)),
                pl.BlockSpec((rows_blk, w), lambda i, t: (rmap(i, t), 1)),
                mem_spec, mem_spec, pl.BlockSpec(memory_space=pl.ANY)]
    args = [p_e, p_e, mem_k, mem_v, out_prev]
    aliases = {4: 0}
    return pl.pallas_call(
        functools.partial(_mem_attn_kernel, nbm=nbm, seq_rows=seq_total),
        out_shape=jax.ShapeDtypeStruct(out_prev.shape, BF16),
        grid=grid,
        in_specs=in_specs,
        out_specs=pl.BlockSpec((None, rows_blk, w), lambda i, t: (slab, rmap(i, t), 0)),
        scratch_shapes=[pltpu.VMEM((MEM_HEADS * nbm * rows_blk, N_MEM), F32),
                        pltpu.VMEM((MEM_HEADS * nbm * rows_blk, N_MEM), BF16)],
        input_output_aliases=aliases,
        compiler_params=_params("parallel", "arbitrary"),
        name="mem_attention",
    )(*args)


def _mem_attn_cache_kernel(q_ref, g_ref, k_hbm, v_hbm, prev_ref, o_ref, kbuf, vbuf, sems, s_scr, p_scr,
                           *, layer, nbm, seq_rows):
    del prev_ref
    i = pl.program_id(0)
    n_steps = pl.num_programs(0)

    def slab_copies(step, slot):
        out = []
        for h in range(MEM_HEADS):
            for which, (src, dst) in enumerate(((k_hbm, kbuf), (v_hbm, vbuf))):
                out.append(pltpu.make_async_copy(src.at[layer, pl.ds(step * nbm, nbm), :, h, :],
                                                 dst.at[slot, h], sems.at[slot, which, h]))
        return out

    slot = i % 2

    @pl.when(i == 0)
    def _():
        for c in slab_copies(0, 0):
            c.start()

    @pl.when(i + 1 < n_steps)
    def _():
        for c in slab_copies(i + 1, 1 - slot):
            c.start()

    for c in slab_copies(i, slot):
        c.wait()

    rows = q_ref.shape[0]
    q = (q_ref[...] * (MEM_HD ** -0.5)).astype(BF16)
    for h in range(MEM_HEADS):
        hs = slice(h * MEM_HD, (h + 1) * MEM_HD)
        for j in range(nbm):
            pair = h * nbm + j
            s_scr[pair * rows:(pair + 1) * rows, :] = _dot_nt(q[:, hs], kbuf[slot, h, j].astype(BF16))
    s = s_scr[...]
    e = jnp.exp(s - jnp.max(s, axis=-1, keepdims=True))
    p_scr[...] = (e / jnp.sum(e, axis=-1, keepdims=True)).astype(BF16)
    seq = _row_iota((rows, MEM_HD)) // seq_rows
    for h in range(MEM_HEADS):
        hs = slice(h * MEM_HD, (h + 1) * MEM_HD)
        acc = None
        for j in range(nbm):
            pair = h * nbm + j
            oh = _dot(p_scr[pair * rows:(pair + 1) * rows, :], vbuf[slot, h, j].astype(BF16))
            acc = oh if acc is None else jnp.where(seq == j, oh, acc)
        o_ref[:, hs] = (acc * _silu(g_ref[:, hs])).astype(o_ref.dtype)


def mem_attention_cache(p_e, cache_k, cache_v, out_prev, *, layer, slab, row_base, n_seq, seq_total, nbm):
    w = BRANCH_W
    rows_blk = nbm * seq_total
    base_blk = row_base // rows_blk
    buf = pltpu.VMEM((2, MEM_HEADS, nbm, N_MEM, MEM_HD), F32)
    return pl.pallas_call(
        functools.partial(_mem_attn_cache_kernel, layer=layer, nbm=nbm, seq_rows=seq_total),
        out_shape=jax.ShapeDtypeStruct(out_prev.shape, BF16),
        grid=(n_seq // nbm,),
        in_specs=[pl.BlockSpec((rows_blk, w), lambda i: (base_blk + i, 0)),
                  pl.BlockSpec((rows_blk, w), lambda i: (base_blk + i, 1)),
                  pl.BlockSpec(memory_space=pl.ANY), pl.BlockSpec(memory_space=pl.ANY),
                  pl.BlockSpec(memory_space=pl.ANY)],
        out_specs=pl.BlockSpec((None, rows_blk, w), lambda i: (slab, base_blk + i, 0)),
        scratch_shapes=[buf, buf, pltpu.SemaphoreType.DMA((2, 2, MEM_HEADS)),
                        pltpu.VMEM((MEM_HEADS * nbm * rows_blk, N_MEM), F32),
                        pltpu.VMEM((MEM_HEADS * nbm * rows_blk, N_MEM), BF16)],
        input_output_aliases={4: 0},
        compiler_params=_params("arbitrary"),
        name="mem_attention_cache",
    )(p_e, p_e, cache_k, cache_v, out_prev)


def _cmul(ar, ai, br, bi):
    return ar * br - ai * bi, ar * bi + ai * br


def _s5_abar(lr, li, log_dt):
    dt = jnp.exp(log_dt)
    mag = jnp.exp(lr * dt)
    return mag * jnp.cos(li * dt), mag * jnp.sin(li * dt)


def _s5_prep_kernel(lra, lia, dta, bra, bia, cfr, cfi, lrc, lic, dtc, ctr, cti,
                    m_ref, wre_ref, wim_ref, vre_ref, vim_ref):
    gi = S5_GB * S5_GROUP
    gp = S5_GB * S5_STATE
    lr, li = lra[...], lia[...]
    ar, ai = _s5_abar(lr, li, dta[...])
    den = lr * lr + li * li
    zr = ((ar - 1.0) * lr + ai * li) / den
    zi = (ai * lr - (ar - 1.0) * li) / den
    bbr, bbi = _cmul(zr, zi, bra[...], bia[...])
    wmask = (_row_iota((gi, gp)) // S5_GROUP) == (_col_iota((gi, gp)) // S5_STATE)
    mmask = (_row_iota((gi, gi)) // S5_GROUP) == (_col_iota((gi, gi)) // S5_GROUP)
    cr, ci = cfr[...], cfi[...]
    pr, pi = jnp.ones_like(ar), jnp.zeros_like(ar)
    for tau in range(S5_CHUNK):
        lrr, lii = _cmul(pr, pi, bbr, bbi)
        wre_ref[S5_CHUNK - 1 - tau] = jnp.where(wmask, lrr, 0.0).astype(BF16)
        wim_ref[S5_CHUNK - 1 - tau] = jnp.where(wmask, lii, 0.0).astype(BF16)
        x = (lax.dot_general(lrr[:, :S5_STATE], cr, (((1,), (1,)), ((), ())), precision=HIGHEST,
                             preferred_element_type=F32)
             - lax.dot_general(lii[:, :S5_STATE], ci, (((1,), (1,)), ((), ())), precision=HIGHEST,
                               preferred_element_type=F32))
        m_ref[tau] = jnp.where(mmask, x, 0.0).astype(BF16)
        pr, pi = _cmul(pr, pi, ar, ai)

    arc, aic = _s5_abar(lrc[...], lic[...], dtc[...])
    vmask = (_row_iota((gp, gi)) // S5_STATE) == (_col_iota((gp, gi)) // S5_GROUP)
    pr, pi = arc, aic
    for t in range(S5_CHUNK):
        vr, vi = _cmul(ctr[...], cti[...], pr, pi)
        vre_ref[t] = jnp.where(vmask, vr, 0.0).astype(BF16)
        vim_ref[t] = jnp.where(vmask, -vi, 0.0).astype(BF16)
        pr, pi = _cmul(pr, pi, arc, aic)


def s5_prepare(lam_re, lam_im, log_dt, b_re, b_im, c_re, c_im):
    dp, g, p = lam_re.shape
    i = S5_GROUP
    gi, gp = S5_GB * i, S5_GB * p
    nb = g // S5_GB

    def a_layout(x_gp):
        return jnp.broadcast_to(x_gp[:, :, None, None, :], (dp, g, i, S5_GB, p)).reshape(dp, g * i, gp)

    def c_layout(x_gp):
        return jnp.broadcast_to(x_gp[:, :, :, None], (dp, g, p, gi)).reshape(dp, g * p, gi)

    ldt = jnp.broadcast_to(log_dt[:, :, None], (dp, g, p))
    bt = lambda b: jnp.broadcast_to(jnp.transpose(b, (0, 1, 3, 2))[:, :, :, None, :],
                                    (dp, g, i, S5_GB, p)).reshape(dp, g * i, gp)
    ct = lambda c: jnp.broadcast_to(jnp.transpose(c, (0, 1, 3, 2))[:, :, :, None, :],
                                    (dp, g, p, S5_GB, i)).reshape(dp, g * p, gi)
    args = [a_layout(lam_re), a_layout(lam_im), a_layout(ldt), bt(b_re), bt(b_im),
            c_re.reshape(dp, g * i, p), c_im.reshape(dp, g * i, p),
            c_layout(lam_re), c_layout(lam_im), c_layout(ldt), ct(c_re), ct(c_im)]
    spec_a = pl.BlockSpec((None, gi, gp), lambda l, b: (l, b, 0))
    spec_f = pl.BlockSpec((None, gi, p), lambda l, b: (l, b, 0))
    spec_c = pl.BlockSpec((None, gp, gi), lambda l, b: (l, b, 0))
    out5 = lambda r, c: pl.BlockSpec((None, S5_CHUNK, None, r, c), lambda l, b: (l, 0, b, 0, 0))
    shp = lambda r, c: jax.ShapeDtypeStruct((dp, S5_CHUNK, nb, r, c), BF16)
    return pl.pallas_call(
        _s5_prep_kernel,
        out_shape=(shp(gi, gi), shp(gi, gp), shp(gi, gp), shp(gp, gi), shp(gp, gi)),
        grid=(dp, nb),
        in_specs=[spec_a] * 5 + [spec_f] * 2 + [spec_c] * 5,
        out_specs=(out5(gi, gi), out5(gi, gp), out5(gi, gp), out5(gp, gi), out5(gp, gi)),
        compiler_params=_params("parallel", "parallel"),
        name="s5_prepare",
    )(*args)


S5_GB_STEP = 1


def _s5_steps(u_ref):
    n = u_ref.shape[0] // S5_CHUNK
    return [u_ref[pl.ds(s, n, stride=S5_CHUNK), :].astype(BF16) for s in range(S5_CHUNK)]


def _s5_e_kernel(u_ref, wre_ref, wim_ref, ere_ref, eim_ref):
    gi, gp = S5_GB * S5_GROUP, S5_GB * S5_STATE
    us = _s5_steps(u_ref)
    for half in range(S5_GB_STEP):
        ucat = jnp.concatenate([u[:, half * gi:(half + 1) * gi] for u in us], axis=1)
        wre = jnp.concatenate([wre_ref[s, half] for s in range(S5_CHUNK)], axis=0)
        wim = jnp.concatenate([wim_ref[s, half] for s in range(S5_CHUNK)], axis=0)
        ere_ref[:, half * gp:(half + 1) * gp] = _dot(ucat, wre)
        eim_ref[:, half * gp:(half + 1) * gp] = _dot(ucat, wim)


def s5_chunk_inputs(p_a, wre, wim, layer, tr):
    rows = p_a.shape[0] // S5_CHUNK
    gi, gp = S5_GB * S5_GROUP, S5_GB * S5_STATE
    n_b = S5_GROUPS // (S5_GB * S5_GB_STEP)
    wspec = pl.BlockSpec((None, S5_CHUNK, S5_GB_STEP, gi, gp), lambda i, b: (layer, 0, b, 0, 0))
    ospec = pl.BlockSpec((tr, S5_GB_STEP * gp), lambda i, b: (i, b))
    oshape = jax.ShapeDtypeStruct((rows, S5_GROUPS * S5_STATE), F32)
    return pl.pallas_call(
        _s5_e_kernel,
        out_shape=(oshape, oshape),
        grid=(rows // tr, n_b),
        in_specs=[pl.BlockSpec((S5_CHUNK * tr, S5_GB_STEP * gi), lambda i, b: (i, b)), wspec, wspec],
        out_specs=(ospec, ospec),
        compiler_params=_params("parallel", "parallel"),
        name="s5_chunk_inputs",
    )(p_a, wre, wim)


def _s5_scan_kernel(ere, eim, h0r, h0i, lr_ref, li_ref, ldt_ref, hre, him, fpr, fpi, fsr, fsi,
                    *, n_seq, rows_per_seq):
    sub = V7X_SUBLANES
    cw = ere.shape[1]
    ar, ai = _s5_abar(lr_ref[...], li_ref[...], ldt_ref[...])
    a2 = _cmul(ar, ai, ar, ai)
    p1 = _cmul(*a2, *a2)
    p2 = _cmul(*p1, *p1)
    p4 = _cmul(*p2, *p2)
    p8 = _cmul(*p4, *p4)
    row8 = _row_iota((sub, cw))
    tr_, ti_ = jnp.ones((sub, cw), F32), jnp.zeros((sub, cw), F32)
    for bit, pw in ((1, p1), (2, p2), (4, p4)):
        nr, ni = _cmul(tr_, ti_, *pw)
        sel = (row8 & bit) != 0
        tr_, ti_ = jnp.where(sel, nr, tr_), jnp.where(sel, ni, ti_)

    base = n_seq * rows_per_seq
    n_s = h0r.shape[0]
    h0r_v, h0i_v = h0r[...], h0i[...]
    hre[base:base + n_s, :] = h0r_v
    him[base:base + n_s, :] = h0i_v
    dr, di = _cmul(p1[0], p1[1], h0r_v, h0i_v)
    fsr[...] = dr + ere[base:base + n_s, :]
    fsi[...] = di + eim[base:base + n_s, :]

    def tile_step(r0, cr, ci):
        xr = ere[pl.ds(r0, sub), :]
        xi = eim[pl.ds(r0, sub), :]
        for k, pw in ((1, p1), (2, p2), (4, p4)):
            keep = row8 >= k
            sr = jnp.where(keep, _shift_rows(xr, k), 0.0)
            si = jnp.where(keep, _shift_rows(xi, k), 0.0)
            mr, mi = _cmul(pw[0], pw[1], sr, si)
            xr, xi = xr + mr, xi + mi
        er = jnp.where(row8 >= 1, _shift_rows(xr, 1), 0.0)
        ei = jnp.where(row8 >= 1, _shift_rows(xi, 1), 0.0)
        qr, qi = _cmul(tr_, ti_, cr, ci)
        hre[pl.ds(r0, sub), :] = er + qr
        him[pl.ds(r0, sub), :] = ei + qi
        nr, ni = _cmul(p8[0], p8[1], cr, ci)
        return xr[sub - 1:sub] + nr, xi[sub - 1:sub] + ni

    def body(j, carry):
        out = []
        for n in range(n_seq):
            r0 = pl.multiple_of(n * rows_per_seq + j * sub, sub)
            out.extend(tile_step(r0, carry[2 * n], carry[2 * n + 1]))
        return tuple(out)

    zero = jnp.zeros((1, cw), F32)
    final = lax.fori_loop(0, rows_per_seq // sub, body, (zero,) * (2 * n_seq))
    for n in range(n_seq):
        fpr[n:n + 1, :] = final[2 * n]
        fpi[n:n + 1, :] = final[2 * n + 1]


def s5_scan(ere, eim, h0r, h0i, lam_re_row, lam_im_row, log_dt_row, layer, *, n_seq, rows_per_seq, cw):
    rows, width = ere.shape
    n_s = h0r.shape[0]
    col = lambda r: pl.BlockSpec((r, cw), lambda j: (0, j))
    prow = pl.BlockSpec((None, 1, cw), lambda j: (layer, 0, j))
    shp = lambda r: jax.ShapeDtypeStruct((r, width), F32)
    return pl.pallas_call(
        functools.partial(_s5_scan_kernel, n_seq=n_seq, rows_per_seq=rows_per_seq),
        out_shape=(shp(rows), shp(rows), shp(n_seq), shp(n_seq), shp(n_s), shp(n_s)),
        grid=(width // cw,),
        in_specs=[col(rows), col(rows), col(n_s), col(n_s), prow, prow, prow],
        out_specs=(col(rows), col(rows), col(n_seq), col(n_seq), col(n_s), col(n_s)),
        compiler_params=_params("parallel"),
        name="s5_scan",
    )(ere, eim, h0r, h0i, lam_re_row, lam_im_row, log_dt_row)


def _s5_y_kernel(u_ref, m_ref, hre_ref, him_ref, vre_ref, vim_ref, y_ref):
    gi, gp = S5_GB * S5_GROUP, S5_GB * S5_STATE
    us = _s5_steps(u_ref)
    n = hre_ref.shape[0]
    for half in range(S5_GB_STEP):
        hs = slice(half * gp, (half + 1) * gp)
        ls = slice(half * gi, (half + 1) * gi)
        hr = hre_ref[:, hs].astype(BF16)
        hi = him_ref[:, hs].astype(BF16)
        zero_m = jnp.zeros((gi, gi), BF16)
        for t0 in range(0, S5_CHUNK, 2):
            t1 = t0 + 1
            lhs = jnp.concatenate([hr, hi] + [us[s][:, ls] for s in range(t1 + 1)], axis=1)
            rows = [jnp.concatenate([vre_ref[t0, half], vre_ref[t1, half]], axis=1),
                    jnp.concatenate([vim_ref[t0, half], vim_ref[t1, half]], axis=1)]
            for s in range(t1 + 1):
                left = m_ref[t0 - s, half] if s <= t0 else zero_m
                rows.append(jnp.concatenate([left, m_ref[t1 - s, half]], axis=1))
            acc = _dot(lhs, jnp.concatenate(rows, axis=0))
            y_ref[pl.ds(t0, n, stride=S5_CHUNK), ls] = acc[:, :gi]
            y_ref[pl.ds(t1, n, stride=S5_CHUNK), ls] = acc[:, gi:]


def s5_outputs(p_a, m, hre, him, vre, vim, layer, tr):
    rows = p_a.shape[0] // S5_CHUNK
    gi, gp = S5_GB * S5_GROUP, S5_GB * S5_STATE
    n_b = S5_GROUPS // (S5_GB * S5_GB_STEP)
    mspec = pl.BlockSpec((None, S5_CHUNK, S5_GB_STEP, gi, gi), lambda i, b: (layer, 0, b, 0, 0))
    vspec = pl.BlockSpec((None, S5_CHUNK, S5_GB_STEP, gp, gi), lambda i, b: (layer, 0, b, 0, 0))
    hspec = pl.BlockSpec((tr, S5_GB_STEP * gp), lambda i, b: (i, b))
    uspec = pl.BlockSpec((S5_CHUNK * tr, S5_GB_STEP * gi), lambda i, b: (i, b))
    return pl.pallas_call(
        _s5_y_kernel,
        out_shape=jax.ShapeDtypeStruct((p_a.shape[0], BRANCH_W), F32),
        grid=(rows // tr, n_b),
        in_specs=[uspec, mspec, hspec, hspec, vspec, vspec],
        out_specs=uspec,
        compiler_params=_params("parallel", "parallel"),
        name="s5_outputs",
    )(p_a, m, hre, him, vre, vim)


def _s5_epilogue_kernel(y_ref, u_ref, g_ref, d_ref, w_ref, o_ref):
    z = _gelu_tanh(y_ref[...] + d_ref[...] * u_ref[...])
    o = z * _sigmoid(_dot(z.astype(BF16), w_ref[...]))
    o_ref[...] = (o * _silu(g_ref[...])).astype(o_ref.dtype)


def s5_epilogue(y, p_a, d_skip, w_glu, tr):
    rows, w = y.shape
    return pl.pallas_call(
        _s5_epilogue_kernel,
        out_shape=jax.ShapeDtypeStruct((N_BRANCH, rows, w), BF16),
        grid=(rows // tr,),
        in_specs=[pl.BlockSpec((tr, w), lambda i: (i, 0)),
                  pl.BlockSpec((tr, w), lambda i: (i, 0)),
                  pl.BlockSpec((tr, w), lambda i: (i, 1)),
                  pl.BlockSpec((1, w), lambda i: (0, 0)),
                  pl.BlockSpec((w, w), lambda i: (0, 0))],
        out_specs=pl.BlockSpec((None, tr, w), lambda i: (0, i, 0)),
        compiler_params=_params("parallel"),
        name="s5_epilogue",
    )(y, p_a, p_a, d_skip, w_glu)


_IN_WIDTHS = (("a_x", 1024), ("a_g", 1024), ("b_q", 1024), ("b_f", 1024), ("b_i", 1024), ("b_g", 1024),
              ("c_q", 512), ("c_k", 512), ("c_v", 1024), ("c_r", 16), ("c_g", 1024),
              ("d_x", 1024), ("d_g", 1024), ("e_q", 1024), ("e_g", 1024))

TM_NORM = 512
TM_MM = 1088
TN_MM = 2048
TN_CQ = 1280
TM_MERGE = 1088
TN_MERGE = 2048
TM_OUT = 544
TR_S5 = 1088
TR_S5_EPI = 544
CW_S5_SCAN = 512
CHUNK_GLA = 128
SUB_GLA = 2
NB_GLA_SAMPLE = 8
CHUNK_LRU = 256
ROWS_MEM_PROMPT = 512
NB_MEM_SAMPLE = 8


def _in_cols(w_in_l):
    cols, off = {}, 0
    for name, width in _IN_WIDTHS:
        cols[name] = w_in_l[:, off:off + width]
        off += width
    return cols


def kernel(x_prompt, x_sample, mem_prompt, state_s5_re, state_s5_im, state_hgrn, state_gla, state_rglru, state_conv, cache_mem_k, cache_mem_v, norm_pre, norm_post, w_in, s5_lam_re, s5_lam_im, s5_log_dt, s5_b_re, s5_b_im, s5_c_re, s5_c_im, s5_d, s5_w_glu, hg_lb_logits, hg_norm, gla_w_up, gla_b_up, gla_norm, conv_w, conv_b, lru_w_r, lru_b_r, lru_w_i, lru_b_i, lru_lam, mem_norm, w_mem_k, w_mem_v, w_branch, w_merge, w_out):
    d, w = D_MODEL, BRANCH_W
    m_p, m_s = BATCH * SEQ, DEC_BATCH * DEC_SEQ
    m_all = m_p + m_s
    bf = lambda a: a.astype(BF16)
    row = lambda v: v[None, :]

    x = jnp.concatenate([x_prompt.reshape(m_p, d), x_sample.reshape(m_s, d)], axis=0)

    mem2 = mem_prompt.reshape(BATCH * N_MEM, d)
    mk_l, mv_l = [], []
    for l in range(DEPTH):
        mn = rmsnorm_rows(mem2, row(mem_norm[l]), TM_NORM)
        mk_l.append(matmul(mn, bf(w_mem_k[l]), BATCH * N_MEM, w // 2))
        mv_l.append(matmul(mn, bf(w_mem_v[l]), BATCH * N_MEM, w // 2))

    lb = hg_lower_bounds(hg_lb_logits)
    s5_m, s5_wre, s5_wim, s5_vre, s5_vim = s5_prepare(s5_lam_re, s5_lam_im, s5_log_dt, s5_b_re, s5_b_im,
                                                      s5_c_re, s5_c_im)
    gs = S5_GROUPS * S5_STATE
    lam_re_row = s5_lam_re.reshape(DEPTH, 1, gs)
    lam_im_row = s5_lam_im.reshape(DEPTH, 1, gs)
    log_dt_row = jnp.broadcast_to(s5_log_dt[:, :, None], (DEPTH, S5_GROUPS, S5_STATE)).reshape(DEPTH, 1, gs)
    s5_h0r = state_s5_re.reshape(DEPTH, DEC_BATCH, gs)
    s5_h0i = state_s5_im.reshape(DEPTH, DEC_BATCH, gs)
    w_up_pad = jnp.zeros((DEPTH, V7X_LANES, GLA_KEY_W), F32).at[:, :GLA_RANK].set(gla_w_up)

    hg_p = hg_s = gl_p = gl_s = None
    outs = {k: [] for k in ("p_s5r", "p_s5i", "p_lru", "p_conv", "s_s5r", "s_s5i", "s_lru", "s_conv")}
    for l in range(DEPTH):
        cols = _in_cols(w_in[l])
        w_a = bf(jnp.concatenate([cols["a_x"], cols["a_g"]], axis=1))
        w_b = bf(jnp.concatenate([cols["b_q"], cols["b_f"], cols["b_i"], cols["b_g"]], axis=1))
        w_cv = bf(jnp.concatenate([cols["c_v"], cols["c_g"]], axis=1))
        w_cq = bf(jnp.concatenate([cols["c_q"], cols["c_k"], cols["c_r"],
                                   jnp.zeros((d, TN_CQ - 2 * GLA_KEY_W - GLA_RANK), F32)], axis=1))
        w_d = bf(jnp.concatenate([cols["d_x"], cols["d_g"]], axis=1))
        w_e = bf(jnp.concatenate([cols["e_q"], cols["e_g"]], axis=1))

        if l == 0:
            h = rmsnorm_rows(x, row(norm_pre[0]), TM_NORM)
        p_a = matmul(h, w_a, TM_MM, TN_MM)
        p_b = matmul(h, w_b, TM_MM, TN_MM)
        p_cv = matmul(h, w_cv, TM_MM, TN_MM)
        p_cq = matmul(h, w_cq, TM_MM, TN_CQ)
        p_d = matmul(h, w_d, TM_MM, TN_MM)
        p_e = matmul(h, w_e, TM_MM, TN_MM)

        ere, eim = s5_chunk_inputs(p_a, s5_wre, s5_wim, l, TR_S5)
        hre, him, fpr, fpi, fsr, fsi = s5_scan(ere, eim, s5_h0r[l], s5_h0i[l], lam_re_row, lam_im_row, log_dt_row,
                                               l, n_seq=BATCH, rows_per_seq=SEQ // S5_CHUNK, cw=CW_S5_SCAN)
        y = s5_outputs(p_a, s5_m, hre, him, s5_vre, s5_vim, l, TR_S5)
        br = s5_epilogue(y, p_a, row(s5_d[l]), bf(s5_w_glu[l]), TR_S5_EPI)
        outs["p_s5r"].append(fpr); outs["p_s5i"].append(fpi)
        outs["s_s5r"].append(fsr); outs["s_s5i"].append(fsi)

        hg_consts = [row(lb[l]), row(hg_norm[l])]
        hg_kw = dict(layer=l, slab=1, heads=HG_HEADS, dk=HG_DK, dv=HG_DV)
        hg_cols = [(p_b, w, 0), (p_b, w, 1), (p_b, w, 2), (p_b, w, 3)]
        br, hg_p = gla_branch("hgrn", hg_cols, hg_consts, None, br, hg_p, row_base=0, n_seq=BATCH,
                              seq_total=SEQ, chunk=CHUNK_GLA, nb=1, sub=SUB_GLA, **hg_kw)
        br, hg_s = gla_branch("hgrn", hg_cols, hg_consts, state_hgrn, br, hg_s, row_base=m_p,
                              n_seq=DEC_BATCH, seq_total=DEC_SEQ, chunk=DEC_SEQ, nb=NB_GLA_SAMPLE, **hg_kw)

        gl_consts = [w_up_pad[l], row(gla_b_up[l]), row(gla_norm[l])]
        gl_kw = dict(layer=l, slab=2, heads=GLA_HEADS, dk=GLA_DK, dv=GLA_DV)
        gl_cols = [(p_cq, GLA_KEY_W, 0), (p_cq, GLA_KEY_W, 1), (p_cv, w, 0), (p_cq, V7X_LANES, w // V7X_LANES),
                   (p_cv, w, 1)]
        br, gl_p = gla_branch("gla", gl_cols, gl_consts, None, br, gl_p, row_base=0, n_seq=BATCH,
                              seq_total=SEQ, chunk=CHUNK_GLA, nb=1, sub=SUB_GLA, **gl_kw)
        br, gl_s = gla_branch("gla", gl_cols, gl_consts, state_gla, br, gl_s, row_base=m_p,
                              n_seq=DEC_BATCH, seq_total=DEC_SEQ, chunk=DEC_SEQ, nb=NB_GLA_SAMPLE, **gl_kw)

        lru_consts = [conv_w[l], row(conv_b[l]), bf(lru_w_r[l]), row(lru_b_r[l]), bf(lru_w_i[l]), row(lru_b_i[l]),
                      row(lru_lam[l])]
        br, hfin_p, cfin_p = lru_prompt(p_d, lru_consts, br, slab=3, n_seq=BATCH, seq_total=SEQ, chunk=CHUNK_LRU)
        br, hfin_s, cfin_s = lru_sample(p_d, jnp.transpose(state_conv[l], (1, 0, 2)), state_rglru[l],
                                        lru_consts, br, slab=3, row_base=m_p)
        outs["p_lru"].append(hfin_p); outs["p_conv"].append(cfin_p)
        outs["s_lru"].append(hfin_s); outs["s_conv"].append(cfin_s)

        br = mem_attention(p_e, mk_l[l].reshape(1, BATCH, N_MEM, w), mv_l[l].reshape(1, BATCH, N_MEM, w), br,
                           layer=0, slab=4, row_base=0, n_seq=BATCH, seq_total=SEQ, rows_blk=ROWS_MEM_PROMPT, nbm=1)
        br = mem_attention_cache(p_e, cache_mem_k, cache_mem_v, br, layer=l, slab=4, row_base=m_p, n_seq=DEC_BATCH,
                                 seq_total=DEC_SEQ, nbm=NB_MEM_SAMPLE)

        merged = merge_branches(h, br, bf(w_merge[l]), bf(w_branch[l]), TM_MERGE, TN_MERGE)
        if l + 1 < DEPTH:
            x, h = out_proj_residual(merged, bf(w_out[l]), row(norm_post[l]), x, row(norm_pre[l + 1]), TM_OUT)
        else:
            x = out_proj_residual(merged, bf(w_out[l]), row(norm_post[l]), x, None, TM_OUT)

    st = lambda k, shape: jnp.stack(outs[k]).reshape(shape)
    s5_p_shape = (DEPTH, BATCH, S5_GROUPS, S5_STATE)
    s5_s_shape = (DEPTH, DEC_BATCH, S5_GROUPS, S5_STATE)
    mem_shape = (DEPTH, BATCH, N_MEM, MEM_HEADS, MEM_HD)
    return (x[:m_p].reshape(BATCH, SEQ, d), x[m_p:].reshape(DEC_BATCH, DEC_SEQ, d),
            st("p_s5r", s5_p_shape), st("p_s5i", s5_p_shape), hg_p, gl_p,
            st("p_lru", (DEPTH, BATCH, w)), st("p_conv", (DEPTH, BATCH, CONV_W - 1, w)),
            jnp.stack(mk_l).reshape(mem_shape), jnp.stack(mv_l).reshape(mem_shape),
            st("s_s5r", s5_s_shape), st("s_s5i", s5_s_shape), hg_s, gl_s,
            st("s_lru", (DEPTH, DEC_BATCH, w)), jnp.transpose(jnp.stack(outs["s_conv"]), (0, 2, 1, 3)))
```

```python
import functools
import math

import jax
import jax.numpy as jnp
from jax import lax
from jax.experimental import pallas as pl
from jax.experimental.pallas import tpu as pltpu

F32 = jnp.float32
BF16 = jnp.bfloat16
HIGHEST = lax.Precision.HIGHEST

V7X_LANES = 128
V7X_SUBLANES = 8
V7X_VMEM_LIMIT_BYTES = 56 * 1024 * 1024

EPS = 1e-6
D_MODEL = 2048
DEPTH = 4
BATCH = 4
SEQ = 2048
DEC_BATCH = 128
DEC_SEQ = 4
BRANCH_W = 1024
N_BRANCH = 5
S5_GROUP = 16
S5_GROUPS = 64
S5_STATE = 64
S5_CHUNK = 4
S5_GB = 8
HG_HEADS = 8
HG_DK = 128
HG_DV = 128
GLA_HEADS = 4
GLA_KEY_W = 512
GLA_DK = 128
GLA_DV = 256
GLA_RANK = 16
GLA_GATE_TEMP = 16.0
LRU_BLOCKS = 8
LRU_BW = 128
CONV_W = 4
LRU_C = 8.0
N_MEM = 256
MEM_HEADS = 4
MEM_HD = 256


def _params(*sem):
    return pltpu.CompilerParams(dimension_semantics=sem, vmem_limit_bytes=V7X_VMEM_LIMIT_BYTES)


def _dot(a, b):
    return jnp.dot(a, b, preferred_element_type=F32)


def _dot_nt(a, b):
    return lax.dot_general(a, b, (((1,), (1,)), ((), ())), preferred_element_type=F32)


def _dot_tn(a, b):
    return lax.dot_general(a, b, (((0,), (0,)), ((), ())), preferred_element_type=F32)


def _sigmoid(x):
    return 1.0 / (1.0 + jnp.exp(-x))


def _silu(x):
    return x * _sigmoid(x)


def _log1p_exp_neg_abs(x):
    return jnp.log(1.0 + jnp.exp(-jnp.abs(x)))


def _log_sigmoid(x):
    return jnp.minimum(x, 0.0) - _log1p_exp_neg_abs(x)


def _softplus(x):
    return jnp.maximum(x, 0.0) + _log1p_exp_neg_abs(x)


def _gelu_tanh(x):
    return 0.5 * x * (1.0 + jnp.tanh(math.sqrt(2.0 / math.pi) * (x + 0.044715 * (x * x * x))))


def _row_iota(shape):
    return lax.broadcasted_iota(jnp.int32, shape, 0)


def _col_iota(shape):
    return lax.broadcasted_iota(jnp.int32, shape, 1)


def _shift_rows(x, k):
    return pltpu.roll(x, k, 0)


def _shift_rows_up(x, k):
    return pltpu.roll(x, x.shape[0] - k, 0)


def _rmsnorm_kernel(x_ref, g_ref, o_ref):
    x = x_ref[...]
    y = x * lax.rsqrt(jnp.mean(x * x, axis=-1, keepdims=True) + EPS)
    o_ref[...] = (y * g_ref[...]).astype(o_ref.dtype)


def rmsnorm_rows(x, g, tm):
    m, d = x.shape
    return pl.pallas_call(
        _rmsnorm_kernel,
        out_shape=jax.ShapeDtypeStruct((m, d), BF16),
        grid=(m // tm,),
        in_specs=[pl.BlockSpec((tm, d), lambda i: (i, 0)), pl.BlockSpec((1, d), lambda i: (0, 0))],
        out_specs=pl.BlockSpec((tm, d), lambda i: (i, 0)),
        compiler_params=_params("parallel"),
        name="rmsnorm_rows",
    )(x, g)


def _matmul_kernel(a_ref, w_ref, o_ref):
    o_ref[...] = _dot(a_ref[...], w_ref[...]).astype(o_ref.dtype)


def matmul(a, w, tm, tn, out_dtype=F32):
    m, k = a.shape
    n = w.shape[1]
    return pl.pallas_call(
        _matmul_kernel,
        out_shape=jax.ShapeDtypeStruct((m, n), out_dtype),
        grid=(m // tm, n // tn),
        in_specs=[pl.BlockSpec((tm, k), lambda i, j: (i, 0)), pl.BlockSpec((k, tn), lambda i, j: (0, j))],
        out_specs=pl.BlockSpec((tm, tn), lambda i, j: (i, j)),
        compiler_params=_params("parallel", "parallel"),
        name="matmul",
    )(a, w)


def _merge_kernel(h_ref, b_ref, wm_ref, wb_ref, o_ref, acc_ref):
    c = pl.program_id(1)
    j = pl.program_id(2)
    nj, _, tn = acc_ref.shape
    contrib = _sigmoid(_dot(h_ref[...], wm_ref[...])) * _dot(b_ref[...], wb_ref[...])

    @pl.when(c == 0)
    def _():
        acc_ref[j] = contrib

    @pl.when(jnp.logical_and(c > 0, c < N_BRANCH - 1))
    def _():
        acc_ref[j] += contrib

    for jj in range(nj):
        @pl.when(jnp.logical_and(c == N_BRANCH - 1, j == jj))
        def _(jj=jj):
            o_ref[:, jj * tn:(jj + 1) * tn] = (acc_ref[jj] + contrib).astype(o_ref.dtype)


def merge_branches(h, branches, w_merge, w_branch, tm, tn):
    m, d = h.shape
    w = branches.shape[2]
    nj = d // tn
    return pl.pallas_call(
        _merge_kernel,
        out_shape=jax.ShapeDtypeStruct((m, d), BF16),
        grid=(m // tm, N_BRANCH, nj),
        in_specs=[pl.BlockSpec((tm, d), lambda i, c, j: (i, 0)),
                  pl.BlockSpec((None, tm, w), lambda i, c, j: (c, i, 0)),
                  pl.BlockSpec((d, tn), lambda i, c, j: (0, c * nj + j)),
                  pl.BlockSpec((None, w, tn), lambda i, c, j: (c, 0, j))],
        out_specs=pl.BlockSpec((tm, d), lambda i, c, j: (i, 0)),
        scratch_shapes=[pltpu.VMEM((nj, tm, tn), F32)],
        compiler_params=_params("parallel", "arbitrary", "arbitrary"),
        name="merge_branches",
    )(h, branches, w_merge, w_branch)


def _out_kernel(m_ref, w_ref, g_ref, x_ref, *rest):
    y = _dot(m_ref[...], w_ref[...])
    y = y * lax.rsqrt(jnp.mean(y * y, axis=-1, keepdims=True) + EPS)
    xn = x_ref[...] + y * g_ref[...]
    if len(rest) == 1:
        (o_ref,) = rest
    else:
        gn_ref, o_ref, h_ref = rest
        hn = xn * lax.rsqrt(jnp.mean(xn * xn, axis=-1, keepdims=True) + EPS)
        h_ref[...] = (hn * gn_ref[...]).astype(h_ref.dtype)
    o_ref[...] = xn


def out_proj_residual(merged, w_out, g, x, g_next, tm):
    m, d = x.shape
    rows = pl.BlockSpec((tm, d), lambda i: (i, 0))
    vec = pl.BlockSpec((1, d), lambda i: (0, 0))
    in_specs = [rows, pl.BlockSpec((d, d), lambda i: (0, 0)), vec, rows]
    args = [merged, w_out, g, x]
    out_shape = jax.ShapeDtypeStruct((m, d), F32)
    out_specs = rows
    if g_next is not None:
        in_specs.append(vec)
        args.append(g_next)
        out_shape = (out_shape, jax.ShapeDtypeStruct((m, d), BF16))
        out_specs = (rows, rows)
    return pl.pallas_call(
        _out_kernel,
        out_shape=out_shape,
        grid=(m // tm,),
        in_specs=in_specs,
        out_specs=out_specs,
        compiler_params=_params("parallel"),
        name="out_proj_residual",
    )(*args)


def _hg_lb_kernel(logit_ref, lb_ref):
    x = logit_ref[...]
    e = jnp.exp(x - jnp.max(x, axis=0, keepdims=True))
    sm = e / jnp.sum(e, axis=0, keepdims=True)
    acc = jnp.zeros_like(sm[0:1])
    lb_ref[0:1, :] = acc
    for l in range(1, DEPTH):
        acc = acc + sm[l:l + 1]
        lb_ref[l:l + 1, :] = acc


def hg_lower_bounds(logits):
    return pl.pallas_call(
        _hg_lb_kernel,
        out_shape=jax.ShapeDtypeStruct(logits.shape, F32),
        name="hg_lower_bounds",
    )(logits)


def _gla_kernel(*refs, mode, seq_len, nb, heads, dk, dv, has_init, sub):
    n_row = 4 if mode == "hgrn" else 5
    n_const = 2 if mode == "hgrn" else 3
    row_refs, consts, rest = refs[:n_row], refs[n_row:n_row + n_const], refs[n_row + n_const:]
    if has_init:
        s0_ref, rest = rest[0], rest[1:]
    o_ref, sout_ref, s_scr = rest[:3]
    scratch = rest[3:]

    @pl.when(pl.program_id(1) == 0)
    def _():
        if has_init:
            s_scr[...] = s0_ref[...]
        else:
            s_scr[...] = jnp.zeros_like(s_scr)

    rows = nb * seq_len
    for sc in range(sub):
        window = pl.ds(sc * rows, rows)
        _gla_chunk(*[r.at[window] for r in row_refs], *consts, o_ref.at[window], s_scr, *scratch,
                   mode=mode, seq_len=seq_len, nb=nb, heads=heads, dk=dk, dv=dv)

    @pl.when(pl.program_id(1) == pl.num_programs(1) - 1)
    def _():
        sout_ref[...] = s_scr[...]


def _gla_chunk(*refs, mode, seq_len, nb, heads, dk, dv):
    L = seq_len
    R = nb * L
    C = heads * dk
    if mode == "hgrn":
        q_ref, f_ref, v_ref, g_ref, lb_ref, gn_ref = refs[:6]
        rest = refs[6:]
    else:
        q_ref, k_ref, v_ref, r_ref, g_ref, wup_ref, bup_ref, gn_ref = refs[:8]
        rest = refs[8:]
    o_ref, s_scr, att_scr, qe_scr, kd_scr, b_scr = rest

    t = _row_iota((R, C)) % L
    if mode == "hgrn":
        lb = lb_ref[...]
        fz = f_ref[...]
        la = jnp.log(lb)
        lc = jnp.log1p(-lb) + _log_sigmoid(fz)
        g = jnp.maximum(la, lc) + _log1p_exp_neg_abs(la - lc)
        k = (1.0 - lb) * _sigmoid(-fz)
        q = q_ref[...]
    else:
        pre = jnp.dot(r_ref[...], wup_ref[...], preferred_element_type=F32, precision=HIGHEST)
        g = _log_sigmoid(pre + bup_ref[...]) * (1.0 / GLA_GATE_TEMP)
        k = k_ref[...]
        q = q_ref[...] * (dk ** -0.5)

    rr = _row_iota((R, R))
    cc = _col_iota((R, R))
    same_seq = (rr // L) == (cc // L)
    tri_incl = jnp.where(same_seq, jnp.where(cc <= rr, 1.0, 0.0), 0.0)
    tri_after = jnp.where(same_seq, jnp.where(cc > rr, 1.0, 0.0), 0.0)
    tri = jnp.concatenate([tri_incl, tri_after], axis=0).astype(BF16)
    g_hi = g.astype(BF16)
    g_r1 = g - g_hi.astype(F32)
    g_mid = g_r1.astype(BF16)
    g_lo = (g_r1 - g_mid.astype(F32)).astype(BF16)
    sums = _dot(tri, g_hi) + _dot(tri, g_mid) + _dot(tri, g_lo)
    b = sums[:R]
    rb = sums[R:]
    b_scr[...] = b
    qe_scr[...] = (q * jnp.exp(b)).astype(BF16)
    kd_scr[...] = (k * jnp.exp(rb)).astype(BF16)

    def add_level(level, qt, kt, mask):
        same = kt is qt
        qt = qt.astype(BF16)
        kt = qt if same else kt.astype(BF16)
        for h in range(heads):
            hs = slice(h * dk, (h + 1) * dk)
            p = jnp.where(mask, _dot_nt(qt[:, hs], kt[:, hs]), 0.0)
            if level == 0:
                att_scr[h] = p
            else:
                att_scr[h] += p

    add_level(0, q, k, rr == cc)
    level = 1
    m = 1
    while m < L:
        pos = t % (2 * m)
        upper = pos >= m
        if m == 1:
            d = jnp.where(upper, g, 0.0)
        elif m == 2:
            d = jnp.where(pos == 2, g,
                          jnp.where(pos == 3, g + _shift_rows(g, 1),
                                    jnp.where(pos == 0, _shift_rows_up(g, 1), 0.0)))
        else:
            b3 = b.reshape(R // (2 * m), 2 * m, C)
            d3 = b3 - b3[:, m - 1:m, :]
            d = -jnp.abs(d3.reshape(R, C))
        x = jnp.where(upper, q, k) * jnp.exp(d)
        blk = 2 * m
        pair = jnp.logical_and((rr // blk) == (cc // blk),
                               jnp.logical_and((rr % blk) >= m, (cc % blk) < m))
        add_level(level, x, x, pair)
        level += 1
        m *= 2

    gn = gn_ref[...]
    seq_o = _row_iota((R, dv)) // L
    seq_k = _row_iota((R, dk)) // L
    pad_rows = (-R) % V7X_LANES
    for h in range(heads):
        hs = slice(h * dk, (h + 1) * dk)
        vs = slice(h * dv, (h + 1) * dv)
        vh = v_ref[:, vs].astype(BF16)
        vh_p = jnp.concatenate([vh, jnp.zeros((pad_rows, dv), BF16)], axis=0) if pad_rows else vh
        o = _dot(att_scr[h].astype(BF16), vh)
        qe = qe_scr[:, hs]
        kd = kd_scr[:, hs]
        for j in range(nb):
            st = s_scr[j, h]
            inter = _dot(qe, st.astype(BF16))
            kd_j = kd
            if nb > 1:
                inter = jnp.where(seq_o == j, inter, 0.0)
                kd_j = jnp.where(seq_k == j, kd, jnp.zeros_like(kd))
            o = o + inter
            if pad_rows:
                kd_j = jnp.concatenate([kd_j, jnp.zeros((pad_rows, dk), BF16)], axis=0)
            upd = _dot_tn(kd_j, vh_p)
            dl = jnp.exp(b_scr[j * L + L - 1:j * L + L, hs])
            colb = jnp.broadcast_to(dl, (dk, dk)).T
            if dv != dk:
                colb = jnp.concatenate([colb] * (dv // dk), axis=1)
            s_scr[j, h] = colb * st + upd
        on = o * lax.rsqrt(jnp.mean(o * o, axis=-1, keepdims=True) + EPS) * gn
        o_ref[:, vs] = (on * _silu(g_ref[:, vs])).astype(o_ref.dtype)


def gla_branch(mode, col_blocks, consts, s0, out_prev, sout_prev, *, layer, slab, row_base, n_seq, seq_total,
               chunk, nb, heads, dk, dv, sub=1):
    L = chunk
    R = nb * L
    rows_blk = sub * R
    n_chunks = seq_total // (sub * L)
    base_blk = row_base // rows_blk
    width = heads * dv

    def rows(i, c):
        return base_blk + i * n_chunks + c

    in_specs = [pl.BlockSpec((rows_blk, w), functools.partial(lambda i, c, bi: (rows(i, c), bi), bi=bi))
                for (_, w, bi) in col_blocks]
    args = [a for (a, _, _) in col_blocks]
    for a in consts:
        in_specs.append(pl.BlockSpec(a.shape, lambda i, c: (0, 0)))
        args.append(a)
    has_init = s0 is not None
    state_spec = pl.BlockSpec((None, nb, heads, dk, dv), lambda i, c: (layer, i, 0, 0, 0))
    if has_init:
        in_specs.append(state_spec)
        args.append(s0)
    n_real = len(args)
    aliases = {}
    for out_idx, prev in enumerate((out_prev, sout_prev)):
        if prev is not None:
            aliases[len(args)] = out_idx
            in_specs.append(pl.BlockSpec(memory_space=pl.ANY))
            args.append(prev)
    n_in = len(args)

    def kern(*refs):
        _gla_kernel(*refs[:n_real], *refs[n_in:], mode=mode, seq_len=L, nb=nb, heads=heads, dk=dk, dv=dv,
                    has_init=has_init, sub=sub)

    return pl.pallas_call(
        kern,
        out_shape=(jax.ShapeDtypeStruct(out_prev.shape, BF16),
                   jax.ShapeDtypeStruct((DEPTH, n_seq, heads, dk, dv), F32)),
        grid=(n_seq // nb, n_chunks),
        in_specs=in_specs,
        out_specs=(pl.BlockSpec((None, rows_blk, width), lambda i, c: (slab, rows(i, c), 0)), state_spec),
        scratch_shapes=[pltpu.VMEM((nb, heads, dk, dv), F32), pltpu.VMEM((heads, R, R), F32),
                        pltpu.VMEM((R, heads * dk), BF16), pltpu.VMEM((R, heads * dk), BF16),
                        pltpu.VMEM((R, heads * dk), F32)],
        input_output_aliases=aliases,
        compiler_params=_params("parallel", "arbitrary"),
        name=mode + "_branch",
    )(*args)


def _lru_gates(xc, wr_ref, br_ref, wi_ref, bi_ref, lam_ref):
    xcb = xc.astype(BF16)
    r_parts, i_parts = [], []
    for blk in range(xc.shape[1] // LRU_BW):
        bs = slice(blk * LRU_BW, (blk + 1) * LRU_BW)
        r_parts.append(_dot(xcb[:, bs], wr_ref[blk]))
        i_parts.append(_dot(xcb[:, bs], wi_ref[blk]))
    r = _sigmoid(jnp.concatenate(r_parts, axis=1) + br_ref[...])
    ig = _sigmoid(jnp.concatenate(i_parts, axis=1) + bi_ref[...])
    log_a = (-LRU_C) * r * _softplus(-lam_ref[...])
    a = jnp.exp(log_a)
    u = jnp.sqrt(1.0 - a * a) * (ig * xc)
    return a, u


def _lru_prompt_kernel(x_ref, g_ref, cw_ref, cb_ref, wr_ref, br_ref, wi_ref, bi_ref, lam_ref,
                       o_ref, hfin_ref, cfin_ref, tail_scr, hc_scr, a_scr, u_scr):
    tc, w = x_ref.shape
    sub = V7X_SUBLANES
    c = pl.program_id(1)

    @pl.when(c == 0)
    def _():
        tail_scr[...] = jnp.zeros_like(tail_scr)
        hc_scr[...] = jnp.zeros_like(hc_scr)

    x = x_ref[...]
    tail = tail_scr[...]
    row8 = _row_iota((sub, w))
    cw = cw_ref[...]
    xc = cb_ref[...] + cw[CONV_W - 1:CONV_W] * x
    for k in range(1, CONV_W):
        xs = _shift_rows(x, k)
        head = jnp.where(row8 < k, _shift_rows(tail, k), xs[0:sub])
        xs = jnp.concatenate([head, xs[sub:]], axis=0)
        xc = xc + cw[CONV_W - 1 - k:CONV_W - k] * xs
    last8 = x[tc - sub:tc]
    tail_scr[...] = last8

    a, u = _lru_gates(xc, wr_ref, br_ref, wi_ref, bi_ref, lam_ref)
    a_scr[...] = a
    u_scr[...] = u

    def body(i, carry):
        r0 = pl.multiple_of(i * sub, sub)
        at = a_scr[pl.ds(r0, sub), :]
        ut = u_scr[pl.ds(r0, sub), :]
        for k in (1, 2, 4):
            keep = row8 >= k
            ut = ut + at * jnp.where(keep, _shift_rows(ut, k), 0.0)
            at = at * jnp.where(keep, _shift_rows(at, k), 1.0)
        ht = ut + at * carry
        u_scr[pl.ds(r0, sub), :] = ht
        return jnp.broadcast_to(ht[sub - 1:sub], (sub, w))

    carry = lax.fori_loop(0, tc // sub, body, hc_scr[...])
    hc_scr[...] = carry
    o_ref[...] = (u_scr[...] * _silu(g_ref[...])).astype(o_ref.dtype)

    @pl.when(c == pl.num_programs(1) - 1)
    def _():
        hfin_ref[...] = carry[0:1]
        cfin_ref[...] = _shift_rows(last8, CONV_W - 1)[0:CONV_W - 1]


def lru_prompt(p_d, consts, out_prev, *, slab, n_seq, seq_total, chunk):
    w = BRANCH_W
    n_chunks = seq_total // chunk
    const_specs = [pl.BlockSpec(a.shape, functools.partial(lambda i, c, nd: (0,) * nd, nd=a.ndim)) for a in consts]

    def kern(*refs):
        n_in = 2 + len(consts)
        _lru_prompt_kernel(*refs[:n_in], *refs[n_in + 1:])

    return pl.pallas_call(
        kern,
        out_shape=(jax.ShapeDtypeStruct(out_prev.shape, BF16),
                   jax.ShapeDtypeStruct((n_seq, 1, w), F32),
                   jax.ShapeDtypeStruct((n_seq, CONV_W - 1, w), F32)),
        grid=(n_seq, n_chunks),
        in_specs=[pl.BlockSpec((chunk, w), lambda i, c: (i * n_chunks + c, 0)),
                  pl.BlockSpec((chunk, w), lambda i, c: (i * n_chunks + c, 1))] + const_specs
        + [pl.BlockSpec(memory_space=pl.ANY)],
        out_specs=(pl.BlockSpec((None, chunk, w), lambda i, c: (slab, i * n_chunks + c, 0)),
                   pl.BlockSpec((None, 1, w), lambda i, c: (i, 0, 0)),
                   pl.BlockSpec((None, CONV_W - 1, w), lambda i, c: (i, 0, 0))),
        scratch_shapes=[pltpu.VMEM((V7X_SUBLANES, w), F32), pltpu.VMEM((V7X_SUBLANES, w), F32),
                        pltpu.VMEM((chunk, w), F32), pltpu.VMEM((chunk, w), F32)],
        input_output_aliases={2 + len(consts): 0},
        compiler_params=_params("parallel", "arbitrary"),
        name="lru_prompt",
    )(p_d, p_d, *consts, out_prev)


def _lru_sample_kernel(x_ref, g_ref, buf_ref, h0_ref, cw_ref, cb_ref, wr_ref, br_ref, wi_ref, bi_ref, lam_ref,
                       prev_ref, o_ref, hfin_ref, cfin_ref, o_scr):
    del prev_ref
    steps = DEC_SEQ
    n = x_ref.shape[0] // steps
    step_rows = lambda t: pl.ds(t, n, stride=steps)
    xcat = [buf_ref[j] for j in range(CONV_W - 1)] + [x_ref[step_rows(t), :] for t in range(steps)]
    cw = cw_ref[...]
    xcs = []
    for t in range(steps):
        xc = cb_ref[...]
        for j in range(CONV_W):
            xc = xc + cw[j:j + 1] * xcat[t + j]
        xcs.append(xc)
    a, u = _lru_gates(jnp.concatenate(xcs, axis=0), wr_ref, br_ref, wi_ref, bi_ref, lam_ref)
    h = h0_ref[...]
    for t in range(steps):
        h = a[t * n:(t + 1) * n] * h + u[t * n:(t + 1) * n]
        o_scr[step_rows(t), :] = h * _silu(g_ref[step_rows(t), :])
    o_ref[...] = o_scr[...].astype(o_ref.dtype)
    hfin_ref[...] = h
    for j in range(CONV_W - 1):
        cfin_ref[j] = xcat[steps + j]


def lru_sample(p_d, conv_buf, h0, consts, out_prev, *, slab, row_base):
    w = BRANCH_W
    bw = LRU_BW
    n = h0.shape[0]
    rows = n * DEC_SEQ
    blk = row_base // rows
    nbk = w // bw
    conv_w_, conv_b_, w_r, b_r, w_i, b_i, lam = consts
    lane = lambda r: pl.BlockSpec((r, bw), lambda i: (0, i))
    wblk = pl.BlockSpec((1, bw, bw), lambda i: (i, 0, 0))
    tail = pl.BlockSpec((CONV_W - 1, n, bw), lambda i: (0, 0, i))
    return pl.pallas_call(
        _lru_sample_kernel,
        out_shape=(jax.ShapeDtypeStruct(out_prev.shape, BF16),
                   jax.ShapeDtypeStruct((n, w), F32),
                   jax.ShapeDtypeStruct((CONV_W - 1, n, w), F32)),
        grid=(nbk,),
        in_specs=[pl.BlockSpec((rows, bw), lambda i: (blk, i)), pl.BlockSpec((rows, bw), lambda i: (blk, nbk + i)),
                  tail, lane(n), lane(CONV_W), lane(1), wblk, lane(1), wblk, lane(1), lane(1),
                  pl.BlockSpec(memory_space=pl.ANY)],
        out_specs=(pl.BlockSpec((None, rows, bw), lambda i: (slab, blk, i)), lane(n), tail),
        scratch_shapes=[pltpu.VMEM((rows, bw), F32)],
        input_output_aliases={11: 0},
        compiler_params=_params("parallel"),
        name="lru_sample",
    )(p_d, p_d, conv_buf, h0, conv_w_, conv_b_, w_r, b_r, w_i, b_i, lam, out_prev)


def _mem_attn_kernel(q_ref, g_ref, k_ref, v_ref, prev_ref, o_ref, s_scr, p_scr, *, nbm, seq_rows):
    del prev_ref
    rows = q_ref.shape[0]
    q = (q_ref[...] * (MEM_HD ** -0.5)).astype(BF16)
    for h in range(MEM_HEADS):
        hs = slice(h * MEM_HD, (h + 1) * MEM_HD)
        for j in range(nbm):
            pair = h * nbm + j
            s_scr[pair * rows:(pair + 1) * rows, :] = _dot_nt(q[:, hs], k_ref[j, :, hs].astype(BF16))
    s = s_scr[...]
    e = jnp.exp(s - jnp.max(s, axis=-1, keepdims=True))
    p_scr[...] = (e / jnp.sum(e, axis=-1, keepdims=True)).astype(BF16)
    seq = _row_iota((rows, MEM_HD)) // seq_rows
    for h in range(MEM_HEADS):
        hs = slice(h * MEM_HD, (h + 1) * MEM_HD)
        acc = None
        for j in range(nbm):
            pair = h * nbm + j
            oh = _dot(p_scr[pair * rows:(pair + 1) * rows, :], v_ref[j, :, hs].astype(BF16))
            acc = oh if acc is None else jnp.where(seq == j, oh, acc)
        o_ref[:, hs] = (acc * _silu(g_ref[:, hs])).astype(o_ref.dtype)


def mem_attention(p_e, mem_k, mem_v, out_prev, *, layer, slab, row_base, n_seq, seq_total, rows_blk, nbm):
    w = BRANCH_W
    base_blk = row_base // rows_blk
    if nbm == 1:
        t_blocks = seq_total // rows_blk
        grid = (n_seq, t_blocks)
        rmap = lambda i, t: base_blk + i * t_blocks + t
    else:
        grid = (n_seq // nbm, 1)
        rmap = lambda i, t: base_blk + i
    mem_spec = pl.BlockSpec((None, nbm, N_MEM, w), lambda i, t: (layer, i, 0, 0))
    in_specs = [pl.BlockSpec((rows_blk, w), lambda i, t: (rmap(i, t), 0)),
                pl.BlockSpec((rows_blk, w), lambda i, t: (rmap(i, t), 1)),
                mem_spec, mem_spec, pl.BlockSpec(memory_space=pl.ANY)]
    args = [p_e, p_e, mem_k, mem_v, out_prev]
    aliases = {4: 0}
    return pl.pallas_call(
        functools.partial(_mem_attn_kernel, nbm=nbm, seq_rows=seq_total),
        out_shape=jax.ShapeDtypeStruct(out_prev.shape, BF16),
        grid=grid,
        in_specs=in_specs,
        out_specs=pl.BlockSpec((None, rows_blk, w), lambda i, t: (slab, rmap(i, t), 0)),
        scratch_shapes=[pltpu.VMEM((MEM_HEADS * nbm * rows_blk, N_MEM), F32),
                        pltpu.VMEM((MEM_HEADS * nbm * rows_blk, N_MEM), BF16)],
        input_output_aliases=aliases,
        compiler_params=_params("parallel", "arbitrary"),
        name="mem_attention",
    )(*args)


def _mem_attn_cache_kernel(q_ref, g_ref, k_hbm, v_hbm, prev_ref, o_ref, kbuf, vbuf, sems, s_scr, p_scr,
                           *, layer, nbm, seq_rows):
    del prev_ref
    i = pl.program_id(0)
    n_steps = pl.num_programs(0)

    def slab_copies(step, slot):
        out = []
        for h in range(MEM_HEADS):
            for which, (src, dst) in enumerate(((k_hbm, kbuf), (v_hbm, vbuf))):
                out.append(pltpu.make_async_copy(src.at[layer, pl.ds(step * nbm, nbm), :, h, :],
                                                 dst.at[slot, h], sems.at[slot, which, h]))
        return out

    slot = i % 2

    @pl.when(i == 0)
    def _():
        for c in slab_copies(0, 0):
            c.start()

    @pl.when(i + 1 < n_steps)
    def _():
        for c in slab_copies(i + 1, 1 - slot):
            c.start()

    for c in slab_copies(i, slot):
        c.wait()

    rows = q_ref.shape[0]
    q = (q_ref[...] * (MEM_HD ** -0.5)).astype(BF16)
    for h in range(MEM_HEADS):
        hs = slice(h * MEM_HD, (h + 1) * MEM_HD)
        for j in range(nbm):
            pair = h * nbm + j
            s_scr[pair * rows:(pair + 1) * rows, :] = _dot_nt(q[:, hs], kbuf[slot, h, j].astype(BF16))
    s = s_scr[...]
    e = jnp.exp(s - jnp.max(s, axis=-1, keepdims=True))
    p_scr[...] = (e / jnp.sum(e, axis=-1, keepdims=True)).astype(BF16)
    seq = _row_iota((rows, MEM_HD)) // seq_rows
    for h in range(MEM_HEADS):
        hs = slice(h * MEM_HD, (h + 1) * MEM_HD)
        acc = None
        for j in range(nbm):
            pair = h * nbm + j
            oh = _dot(p_scr[pair * rows:(pair + 1) * rows, :], vbuf[slot, h, j].astype(BF16))
            acc = oh if acc is None else jnp.where(seq == j, oh, acc)
        o_ref[:, hs] = (acc * _silu(g_ref[:, hs])).astype(o_ref.dtype)


def mem_attention_cache(p_e, cache_k, cache_v, out_prev, *, layer, slab, row_base, n_seq, seq_total, nbm):
    w = BRANCH_W
    rows_blk = nbm * seq_total
    base_blk = row_base // rows_blk
    buf = pltpu.VMEM((2, MEM_HEADS, nbm, N_MEM, MEM_HD), F32)
    return pl.pallas_call(
        functools.partial(_mem_attn_cache_kernel, layer=layer, nbm=nbm, seq_rows=seq_total),
        out_shape=jax.ShapeDtypeStruct(out_prev.shape, BF16),
        grid=(n_seq // nbm,),
        in_specs=[pl.BlockSpec((rows_blk, w), lambda i: (base_blk + i, 0)),
                  pl.BlockSpec((rows_blk, w), lambda i: (base_blk + i, 1)),
                  pl.BlockSpec(memory_space=pl.ANY), pl.BlockSpec(memory_space=pl.ANY),
                  pl.BlockSpec(memory_space=pl.ANY)],
        out_specs=pl.BlockSpec((None, rows_blk, w), lambda i: (slab, base_blk + i, 0)),
        scratch_shapes=[buf, buf, pltpu.SemaphoreType.DMA((2, 2, MEM_HEADS)),
                        pltpu.VMEM((MEM_HEADS * nbm * rows_blk, N_MEM), F32),
                        pltpu.VMEM((MEM_HEADS * nbm * rows_blk, N_MEM), BF16)],
        input_output_aliases={4: 0},
        compiler_params=_params("arbitrary"),
        name="mem_attention_cache",
    )(p_e, p_e, cache_k, cache_v, out_prev)


def _cmul(ar, ai, br, bi):
    return ar * br - ai * bi, ar * bi + ai * br


def _s5_abar(lr, li, log_dt):
    dt = jnp.exp(log_dt)
    mag = jnp.exp(lr * dt)
    return mag * jnp.cos(li * dt), mag * jnp.sin(li * dt)


def _s5_prep_kernel(lra, lia, dta, bra, bia, cfr, cfi, lrc, lic, dtc, ctr, cti,
                    m_ref, wre_ref, wim_ref, vre_ref, vim_ref):
    gi = S5_GB * S5_GROUP
    gp = S5_GB * S5_STATE
    lr, li = lra[...], lia[...]
    ar, ai = _s5_abar(lr, li, dta[...])
    den = lr * lr + li * li
    zr = ((ar - 1.0) * lr + ai * li) / den
    zi = (ai * lr - (ar - 1.0) * li) / den
    bbr, bbi = _cmul(zr, zi, bra[...], bia[...])
    wmask = (_row_iota((gi, gp)) // S5_GROUP) == (_col_iota((gi, gp)) // S5_STATE)
    mmask = (_row_iota((gi, gi)) // S5_GROUP) == (_col_iota((gi, gi)) // S5_GROUP)
    cr, ci = cfr[...], cfi[...]
    pr, pi = jnp.ones_like(ar), jnp.zeros_like(ar)
    for tau in range(S5_CHUNK):
        lrr, lii = _cmul(pr, pi, bbr, bbi)
        wre_ref[S5_CHUNK - 1 - tau] = jnp.where(wmask, lrr, 0.0).astype(BF16)
        wim_ref[S5_CHUNK - 1 - tau] = jnp.where(wmask, lii, 0.0).astype(BF16)
        x = (lax.dot_general(lrr[:, :S5_STATE], cr, (((1,), (1,)), ((), ())), precision=HIGHEST,
                             preferred_element_type=F32)
             - lax.dot_general(lii[:, :S5_STATE], ci, (((1,), (1,)), ((), ())), precision=HIGHEST,
                               preferred_element_type=F32))
        m_ref[tau] = jnp.where(mmask, x, 0.0).astype(BF16)
        pr, pi = _cmul(pr, pi, ar, ai)

    arc, aic = _s5_abar(lrc[...], lic[...], dtc[...])
    vmask = (_row_iota((gp, gi)) // S5_STATE) == (_col_iota((gp, gi)) // S5_GROUP)
    pr, pi = arc, aic
    for t in range(S5_CHUNK):
        vr, vi = _cmul(ctr[...], cti[...], pr, pi)
        vre_ref[t] = jnp.where(vmask, vr, 0.0).astype(BF16)
        vim_ref[t] = jnp.where(vmask, -vi, 0.0).astype(BF16)
        pr, pi = _cmul(pr, pi, arc, aic)


def s5_prepare(lam_re, lam_im, log_dt, b_re, b_im, c_re, c_im):
    dp, g, p = lam_re.shape
    i = S5_GROUP
    gi, gp = S5_GB * i, S5_GB * p
    nb = g // S5_GB

    def a_layout(x_gp):
        return jnp.broadcast_to(x_gp[:, :, None, None, :], (dp, g, i, S5_GB, p)).reshape(dp, g * i, gp)

    def c_layout(x_gp):
        return jnp.broadcast_to(x_gp[:, :, :, None], (dp, g, p, gi)).reshape(dp, g * p, gi)

    ldt = jnp.broadcast_to(log_dt[:, :, None], (dp, g, p))
    bt = lambda b: jnp.broadcast_to(jnp.transpose(b, (0, 1, 3, 2))[:, :, :, None, :],
                                    (dp, g, i, S5_GB, p)).reshape(dp, g * i, gp)
    ct = lambda c: jnp.broadcast_to(jnp.transpose(c, (0, 1, 3, 2))[:, :, :, None, :],
                                    (dp, g, p, S5_GB, i)).reshape(dp, g * p, gi)
    args = [a_layout(lam_re), a_layout(lam_im), a_layout(ldt), bt(b_re), bt(b_im),
            c_re.reshape(dp, g * i, p), c_im.reshape(dp, g * i, p),
            c_layout(lam_re), c_layout(lam_im), c_layout(ldt), ct(c_re), ct(c_im)]
    spec_a = pl.BlockSpec((None, gi, gp), lambda l, b: (l, b, 0))
    spec_f = pl.BlockSpec((None, gi, p), lambda l, b: (l, b, 0))
    spec_c = pl.BlockSpec((None, gp, gi), lambda l, b: (l, b, 0))
    out5 = lambda r, c: pl.BlockSpec((None, S5_CHUNK, None, r, c), lambda l, b: (l, 0, b, 0, 0))
    shp = lambda r, c: jax.ShapeDtypeStruct((dp, S5_CHUNK, nb, r, c), BF16)
    return pl.pallas_call(
        _s5_prep_kernel,
        out_shape=(shp(gi, gi), shp(gi, gp), shp(gi, gp), shp(gp, gi), shp(gp, gi)),
        grid=(dp, nb),
        in_specs=[spec_a] * 5 + [spec_f] * 2 + [spec_c] * 5,
        out_specs=(out5(gi, gi), out5(gi, gp), out5(gi, gp), out5(gp, gi), out5(gp, gi)),
        compiler_params=_params("parallel", "parallel"),
        name="s5_prepare",
    )(*args)


S5_GB_STEP = 1


def _s5_steps(u_ref):
    n = u_ref.shape[0] // S5_CHUNK
    return [u_ref[pl.ds(s, n, stride=S5_CHUNK), :].astype(BF16) for s in range(S5_CHUNK)]


def _s5_e_kernel(u_ref, wre_ref, wim_ref, ere_ref, eim_ref):
    gi, gp = S5_GB * S5_GROUP, S5_GB * S5_STATE
    us = _s5_steps(u_ref)
    for half in range(S5_GB_STEP):
        ucat = jnp.concatenate([u[:, half * gi:(half + 1) * gi] for u in us], axis=1)
        wre = jnp.concatenate([wre_ref[s, half] for s in range(S5_CHUNK)], axis=0)
        wim = jnp.concatenate([wim_ref[s, half] for s in range(S5_CHUNK)], axis=0)
        ere_ref[:, half * gp:(half + 1) * gp] = _dot(ucat, wre)
        eim_ref[:, half * gp:(half + 1) * gp] = _dot(ucat, wim)


def s5_chunk_inputs(p_a, wre, wim, layer, tr):
    rows = p_a.shape[0] // S5_CHUNK
    gi, gp = S5_GB * S5_GROUP, S5_GB * S5_STATE
    n_b = S5_GROUPS // (S5_GB * S5_GB_STEP)
    wspec = pl.BlockSpec((None, S5_CHUNK, S5_GB_STEP, gi, gp), lambda i, b: (layer, 0, b, 0, 0))
    ospec = pl.BlockSpec((tr, S5_GB_STEP * gp), lambda i, b: (i, b))
    oshape = jax.ShapeDtypeStruct((rows, S5_GROUPS * S5_STATE), F32)
    return pl.pallas_call(
        _s5_e_kernel,
        out_shape=(oshape, oshape),
        grid=(rows // tr, n_b),
        in_specs=[pl.BlockSpec((S5_CHUNK * tr, S5_GB_STEP * gi), lambda i, b: (i, b)), wspec, wspec],
        out_specs=(ospec, ospec),
        compiler_params=_params("parallel", "parallel"),
        name="s5_chunk_inputs",
    )(p_a, wre, wim)


def _s5_scan_kernel(ere, eim, h0r, h0i, lr_ref, li_ref, ldt_ref, hre, him, fpr, fpi, fsr, fsi,
                    *, n_seq, rows_per_seq):
    sub = V7X_SUBLANES
    cw = ere.shape[1]
    ar, ai = _s5_abar(lr_ref[...], li_ref[...], ldt_ref[...])
    a2 = _cmul(ar, ai, ar, ai)
    p1 = _cmul(*a2, *a2)
    p2 = _cmul(*p1, *p1)
    p4 = _cmul(*p2, *p2)
    p8 = _cmul(*p4, *p4)
    row8 = _row_iota((sub, cw))
    tr_, ti_ = jnp.ones((sub, cw), F32), jnp.zeros((sub, cw), F32)
    for bit, pw in ((1, p1), (2, p2), (4, p4)):
        nr, ni = _cmul(tr_, ti_, *pw)
        sel = (row8 & bit) != 0
        tr_, ti_ = jnp.where(sel, nr, tr_), jnp.where(sel, ni, ti_)

    base = n_seq * rows_per_seq
    n_s = h0r.shape[0]
    h0r_v, h0i_v = h0r[...], h0i[...]
    hre[base:base + n_s, :] = h0r_v
    him[base:base + n_s, :] = h0i_v
    dr, di = _cmul(p1[0], p1[1], h0r_v, h0i_v)
    fsr[...] = dr + ere[base:base + n_s, :]
    fsi[...] = di + eim[base:base + n_s, :]

    def tile_step(r0, cr, ci):
        xr = ere[pl.ds(r0, sub), :]
        xi = eim[pl.ds(r0, sub), :]
        for k, pw in ((1, p1), (2, p2), (4, p4)):
            keep = row8 >= k
            sr = jnp.where(keep, _shift_rows(xr, k), 0.0)
            si = jnp.where(keep, _shift_rows(xi, k), 0.0)
            mr, mi = _cmul(pw[0], pw[1], sr, si)
            xr, xi = xr + mr, xi + mi
        er = jnp.where(row8 >= 1, _shift_rows(xr, 1), 0.0)
        ei = jnp.where(row8 >= 1, _shift_rows(xi, 1), 0.0)
        qr, qi = _cmul(tr_, ti_, cr, ci)
        hre[pl.ds(r0, sub), :] = er + qr
        him[pl.ds(r0, sub), :] = ei + qi
        nr, ni = _cmul(p8[0], p8[1], cr, ci)
        return xr[sub - 1:sub] + nr, xi[sub - 1:sub] + ni

    def body(j, carry):
        out = []
        for n in range(n_seq):
            r0 = pl.multiple_of(n * rows_per_seq + j * sub, sub)
            out.extend(tile_step(r0, carry[2 * n], carry[2 * n + 1]))
        return tuple(out)

    zero = jnp.zeros((1, cw), F32)
    final = lax.fori_loop(0, rows_per_seq // sub, body, (zero,) * (2 * n_seq))
    for n in range(n_seq):
        fpr[n:n + 1, :] = final[2 * n]
        fpi[n:n + 1, :] = final[2 * n + 1]


def s5_scan(ere, eim, h0r, h0i, lam_re_row, lam_im_row, log_dt_row, layer, *, n_seq, rows_per_seq, cw):
    rows, width = ere.shape
    n_s = h0r.shape[0]
    col = lambda r: pl.BlockSpec((r, cw), lambda j: (0, j))
    prow = pl.BlockSpec((None, 1, cw), lambda j: (layer, 0, j))
    shp = lambda r: jax.ShapeDtypeStruct((r, width), F32)
    return pl.pallas_call(
        functools.partial(_s5_scan_kernel, n_seq=n_seq, rows_per_seq=rows_per_seq),
        out_shape=(shp(rows), shp(rows), shp(n_seq), shp(n_seq), shp(n_s), shp(n_s)),
        grid=(width // cw,),
        in_specs=[col(rows), col(rows), col(n_s), col(n_s), prow, prow, prow],
        out_specs=(col(rows), col(rows), col(n_seq), col(n_seq), col(n_s), col(n_s)),
        compiler_params=_params("parallel"),
        name="s5_scan",
    )(ere, eim, h0r, h0i, lam_re_row, lam_im_row, log_dt_row)


def _s5_y_kernel(u_ref, m_ref, hre_ref, him_ref, vre_ref, vim_ref, y_ref):
    gi, gp = S5_GB * S5_GROUP, S5_GB * S5_STATE
    us = _s5_steps(u_ref)
    n = hre_ref.shape[0]
    for half in range(S5_GB_STEP):
        hs = slice(half * gp, (half + 1) * gp)
        ls = slice(half * gi, (half + 1) * gi)
        hr = hre_ref[:, hs].astype(BF16)
        hi = him_ref[:, hs].astype(BF16)
        zero_m = jnp.zeros((gi, gi), BF16)
        for t0 in range(0, S5_CHUNK, 2):
            t1 = t0 + 1
            lhs = jnp.concatenate([hr, hi] + [us[s][:, ls] for s in range(t1 + 1)], axis=1)
            rows = [jnp.concatenate([vre_ref[t0, half], vre_ref[t1, half]], axis=1),
                    jnp.concatenate([vim_ref[t0, half], vim_ref[t1, half]], axis=1)]
            for s in range(t1 + 1):
                left = m_ref[t0 - s, half] if s <= t0 else zero_m
                rows.append(jnp.concatenate([left, m_ref[t1 - s, half]], axis=1))
            acc = _dot(lhs, jnp.concatenate(rows, axis=0))
            y_ref[pl.ds(t0, n, stride=S5_CHUNK), ls] = acc[:, :gi]
            y_ref[pl.ds(t1, n, stride=S5_CHUNK), ls] = acc[:, gi:]


def s5_outputs(p_a, m, hre, him, vre, vim, layer, tr):
    rows = p_a.shape[0] // S5_CHUNK
    gi, gp = S5_GB * S5_GROUP, S5_GB * S5_STATE
    n_b = S5_GROUPS // (S5_GB * S5_GB_STEP)
    mspec = pl.BlockSpec((None, S5_CHUNK, S5_GB_STEP, gi, gi), lambda i, b: (layer, 0, b, 0, 0))
    vspec = pl.BlockSpec((None, S5_CHUNK, S5_GB_STEP, gp, gi), lambda i, b: (layer, 0, b, 0, 0))
    hspec = pl.BlockSpec((tr, S5_GB_STEP * gp), lambda i, b: (i, b))
    uspec = pl.BlockSpec((S5_CHUNK * tr, S5_GB_STEP * gi), lambda i, b: (i, b))
    return pl.pallas_call(
        _s5_y_kernel,
        out_shape=jax.ShapeDtypeStruct((p_a.shape[0], BRANCH_W), F32),
        grid=(rows // tr, n_b),
        in_specs=[uspec, mspec, hspec, hspec, vspec, vspec],
        out_specs=uspec,
        compiler_params=_params("parallel", "parallel"),
        name="s5_outputs",
    )(p_a, m, hre, him, vre, vim)


def _s5_epilogue_kernel(y_ref, u_ref, g_ref, d_ref, w_ref, o_ref):
    z = _gelu_tanh(y_ref[...] + d_ref[...] * u_ref[...])
    o = z * _sigmoid(_dot(z.astype(BF16), w_ref[...]))
    o_ref[...] = (o * _silu(g_ref[...])).astype(o_ref.dtype)


def s5_epilogue(y, p_a, d_skip, w_glu, tr):
    rows, w = y.shape
    return pl.pallas_call(
        _s5_epilogue_kernel,
        out_shape=jax.ShapeDtypeStruct((N_BRANCH, rows, w), BF16),
        grid=(rows // tr,),
        in_specs=[pl.BlockSpec((tr, w), lambda i: (i, 0)),
                  pl.BlockSpec((tr, w), lambda i: (i, 0)),
                  pl.BlockSpec((tr, w), lambda i: (i, 1)),
                  pl.BlockSpec((1, w), lambda i: (0, 0)),
                  pl.BlockSpec((w, w), lambda i: (0, 0))],
        out_specs=pl.BlockSpec((None, tr, w), lambda i: (0, i, 0)),
        compiler_params=_params("parallel"),
        name="s5_epilogue",
    )(y, p_a, p_a, d_skip, w_glu)


_IN_WIDTHS = (("a_x", 1024), ("a_g", 1024), ("b_q", 1024), ("b_f", 1024), ("b_i", 1024), ("b_g", 1024),
              ("c_q", 512), ("c_k", 512), ("c_v", 1024), ("c_r", 16), ("c_g", 1024),
              ("d_x", 1024), ("d_g", 1024), ("e_q", 1024), ("e_g", 1024))

TM_NORM = 512
TM_MM = 1088
TN_MM = 2048
TN_CQ = 1280
TM_MERGE = 1088
TN_MERGE = 1024
TM_OUT = 544
TR_S5 = 1088
TR_S5_EPI = 1088
CW_S5_SCAN = 512
CHUNK_GLA = 128
SUB_GLA = 2
NB_GLA_SAMPLE = 8
CHUNK_LRU = 256
ROWS_MEM_PROMPT = 1024
NB_MEM_SAMPLE = 8


def _in_cols(w_in_l):
    cols, off = {}, 0
    for name, width in _IN_WIDTHS:
        cols[name] = w_in_l[:, off:off + width]
        off += width
    return cols


def kernel(x_prompt, x_sample, mem_prompt, state_s5_re, state_s5_im, state_hgrn, state_gla, state_rglru, state_conv, cache_mem_k, cache_mem_v, norm_pre, norm_post, w_in, s5_lam_re, s5_lam_im, s5_log_dt, s5_b_re, s5_b_im, s5_c_re, s5_c_im, s5_d, s5_w_glu, hg_lb_logits, hg_norm, gla_w_up, gla_b_up, gla_norm, conv_w, conv_b, lru_w_r, lru_b_r, lru_w_i, lru_b_i, lru_lam, mem_norm, w_mem_k, w_mem_v, w_branch, w_merge, w_out):
    d, w = D_MODEL, BRANCH_W
    m_p, m_s = BATCH * SEQ, DEC_BATCH * DEC_SEQ
    m_all = m_p + m_s
    bf = lambda a: a.astype(BF16)
    row = lambda v: v[None, :]

    x = jnp.concatenate([x_prompt.reshape(m_p, d), x_sample.reshape(m_s, d)], axis=0)

    mem2 = mem_prompt.reshape(BATCH * N_MEM, d)
    mk_l, mv_l = [], []
    for l in range(DEPTH):
        mn = rmsnorm_rows(mem2, row(mem_norm[l]), TM_NORM)
        mk_l.append(matmul(mn, bf(w_mem_k[l]), BATCH * N_MEM, w // 2))
        mv_l.append(matmul(mn, bf(w_mem_v[l]), BATCH * N_MEM, w // 2))

    lb = hg_lower_bounds(hg_lb_logits)
    s5_m, s5_wre, s5_wim, s5_vre, s5_vim = s5_prepare(s5_lam_re, s5_lam_im, s5_log_dt, s5_b_re, s5_b_im,
                                                      s5_c_re, s5_c_im)
    gs = S5_GROUPS * S5_STATE
    lam_re_row = s5_lam_re.reshape(DEPTH, 1, gs)
    lam_im_row = s5_lam_im.reshape(DEPTH, 1, gs)
    log_dt_row = jnp.broadcast_to(s5_log_dt[:, :, None], (DEPTH, S5_GROUPS, S5_STATE)).reshape(DEPTH, 1, gs)
    s5_h0r = state_s5_re.reshape(DEPTH, DEC_BATCH, gs)
    s5_h0i = state_s5_im.reshape(DEPTH, DEC_BATCH, gs)
    w_up_pad = jnp.zeros((DEPTH, V7X_LANES, GLA_KEY_W), F32).at[:, :GLA_RANK].set(gla_w_up)

    hg_p = hg_s = gl_p = gl_s = None
    outs = {k: [] for k in ("p_s5r", "p_s5i", "p_lru", "p_conv", "s_s5r", "s_s5i", "s_lru", "s_conv")}
    for l in range(DEPTH):
        cols = _in_cols(w_in[l])
        w_a = bf(jnp.concatenate([cols["a_x"], cols["a_g"]], axis=1))
        w_b = bf(jnp.concatenate([cols["b_q"], cols["b_f"], cols["b_i"], cols["b_g"]], axis=1))
        w_cv = bf(jnp.concatenate([cols["c_v"], cols["c_g"]], axis=1))
        w_cq = bf(jnp.concatenate([cols["c_q"], cols["c_k"], cols["c_r"],
                                   jnp.zeros((d, TN_CQ - 2 * GLA_KEY_W - GLA_RANK), F32)], axis=1))
        w_d = bf(jnp.concatenate([cols["d_x"], cols["d_g"]], axis=1))
        w_e = bf(jnp.concatenate([cols["e_q"], cols["e_g"]], axis=1))

        if l == 0:
            h = rmsnorm_rows(x, row(norm_pre[0]), TM_NORM)
        p_a = matmul(h, w_a, TM_MM, TN_MM)
        p_b = matmul(h, w_b, TM_MM, TN_MM)
        p_cv = matmul(h, w_cv, TM_MM, TN_MM)
        p_cq = matmul(h, w_cq, TM_MM, TN_CQ)
        p_d = matmul(h, w_d, TM_MM, TN_MM)
        p_e = matmul(h, w_e, TM_MM, TN_MM)

        ere, eim = s5_chunk_inputs(p_a, s5_wre, s5_wim, l, TR_S5)
        hre, him, fpr, fpi, fsr, fsi = s5_scan(ere, eim, s5_h0r[l], s5_h0i[l], lam_re_row, lam_im_row, log_dt_row,
                                               l, n_seq=BATCH, rows_per_seq=SEQ // S5_CHUNK, cw=CW_S5_SCAN)
        y = s5_outputs(p_a, s5_m, hre, him, s5_vre, s5_vim, l, TR_S5)
        br = s5_epilogue(y, p_a, row(s5_d[l]), bf(s5_w_glu[l]), TR_S5_EPI)
        outs["p_s5r"].append(fpr); outs["p_s5i"].append(fpi)
        outs["s_s5r"].append(fsr); outs["s_s5i"].append(fsi)

        hg_consts = [row(lb[l]), row(hg_norm[l])]
        hg_kw = dict(layer=l, slab=1, heads=HG_HEADS, dk=HG_DK, dv=HG_DV)
        hg_cols = [(p_b, w, 0), (p_b, w, 1), (p_b, w, 2), (p_b, w, 3)]
        br, hg_p = gla_branch("hgrn", hg_cols, hg_consts, None, br, hg_p, row_base=0, n_seq=BATCH,
                              seq_total=SEQ, chunk=CHUNK_GLA, nb=1, sub=SUB_GLA, **hg_kw)
        br, hg_s = gla_branch("hgrn", hg_cols, hg_consts, state_hgrn, br, hg_s, row_base=m_p,
                              n_seq=DEC_BATCH, seq_total=DEC_SEQ, chunk=DEC_SEQ, nb=NB_GLA_SAMPLE, **hg_kw)

        gl_consts = [w_up_pad[l], row(gla_b_up[l]), row(gla_norm[l])]
        gl_kw = dict(layer=l, slab=2, heads=GLA_HEADS, dk=GLA_DK, dv=GLA_DV)
        gl_cols = [(p_cq, GLA_KEY_W, 0), (p_cq, GLA_KEY_W, 1), (p_cv, w, 0), (p_cq, V7X_LANES, w // V7X_LANES),
                   (p_cv, w, 1)]
        br, gl_p = gla_branch("gla", gl_cols, gl_consts, None, br, gl_p, row_base=0, n_seq=BATCH,
                              seq_total=SEQ, chunk=CHUNK_GLA, nb=1, sub=SUB_GLA, **gl_kw)
        br, gl_s = gla_branch("gla", gl_cols, gl_consts, state_gla, br, gl_s, row_base=m_p,
                              n_seq=DEC_BATCH, seq_total=DEC_SEQ, chunk=DEC_SEQ, nb=NB_GLA_SAMPLE, **gl_kw)

        lru_consts = [conv_w[l], row(conv_b[l]), bf(lru_w_r[l]), row(lru_b_r[l]), bf(lru_w_i[l]), row(lru_b_i[l]),
                      row(lru_lam[l])]
        br, hfin_p, cfin_p = lru_prompt(p_d, lru_consts, br, slab=3, n_seq=BATCH, seq_total=SEQ, chunk=CHUNK_LRU)
        br, hfin_s, cfin_s = lru_sample(p_d, jnp.transpose(state_conv[l], (1, 0, 2)), state_rglru[l],
                                        lru_consts, br, slab=3, row_base=m_p)
        outs["p_lru"].append(hfin_p); outs["p_conv"].append(cfin_p)
        outs["s_lru"].append(hfin_s); outs["s_conv"].append(cfin_s)

        br = mem_attention(p_e, mk_l[l].reshape(1, BATCH, N_MEM, w), mv_l[l].reshape(1, BATCH, N_MEM, w), br,
                           layer=0, slab=4, row_base=0, n_seq=BATCH, seq_total=SEQ, rows_blk=ROWS_MEM_PROMPT, nbm=1)
        br = mem_attention_cache(p_e, cache_mem_k, cache_mem_v, br, layer=l, slab=4, row_base=m_p, n_seq=DEC_BATCH,
                                 seq_total=DEC_SEQ, nbm=NB_MEM_SAMPLE)

        merged = merge_branches(h, br, bf(w_merge[l]), bf(w_branch[l]), TM_MERGE, TN_MERGE)
        if l + 1 < DEPTH:
            x, h = out_proj_residual(merged, bf(w_out[l]), row(norm_post[l]), x, row(norm_pre[l + 1]), TM_OUT)
        else:
            x = out_proj_residual(merged, bf(w_out[l]), row(norm_post[l]), x, None, TM_OUT)

    st = lambda k, shape: jnp.stack(outs[k]).reshape(shape)
    s5_p_shape = (DEPTH, BATCH, S5_GROUPS, S5_STATE)
    s5_s_shape = (DEPTH, DEC_BATCH, S5_GROUPS, S5_STATE)
    mem_shape = (DEPTH, BATCH, N_MEM, MEM_HEADS, MEM_HD)
    return (x[:m_p].reshape(BATCH, SEQ, d), x[m_p:].reshape(DEC_BATCH, DEC_SEQ, d),
            st("p_s5r", s5_p_shape), st("p_s5i", s5_p_shape), hg_p, gl_p,
            st("p_lru", (DEPTH, BATCH, w)), st("p_conv", (DEPTH, BATCH, CONV_W - 1, w)),
            jnp.stack(mk_l).reshape(mem_shape), jnp.stack(mv_l).reshape(mem_shape),
            st("s_s5r", s5_s_shape), st("s_s5i", s5_s_shape), hg_s, gl_s,
            st("s_lru", (DEPTH, DEC_BATCH, w)), jnp.transpose(jnp.stack(outs["s_conv"]), (0, 2, 1, 3)))
```

```python
import functools
import math

import jax
import jax.numpy as jnp
from jax import lax
from jax.experimental import pallas as pl
from jax.experimental.pallas import tpu as pltpu

F32 = jnp.float32
BF16 = jnp.bfloat16
HIGHEST = lax.Precision.HIGHEST

V7X_LANES = 128
V7X_SUBLANES = 8
V7X_VMEM_LIMIT_BYTES = 56 * 1024 * 1024

EPS = 1e-6
D_MODEL = 2048
DEPTH = 4
BATCH = 4
SEQ = 2048
DEC_BATCH = 128
DEC_SEQ = 4
BRANCH_W = 1024
N_BRANCH = 5
S5_GROUP = 16
S5_GROUPS = 64
S5_STATE = 64
S5_CHUNK = 4
S5_GB = 8
HG_HEADS = 8
HG_DK = 128
HG_DV = 128
GLA_HEADS = 4
GLA_KEY_W = 512
GLA_DK = 128
GLA_DV = 256
GLA_RANK = 16
GLA_GATE_TEMP = 16.0
LRU_BLOCKS = 8
LRU_BW = 128
CONV_W = 4
LRU_C = 8.0
N_MEM = 256
MEM_HEADS = 4
MEM_HD = 256


def _params(*sem):
    return pltpu.CompilerParams(dimension_semantics=sem, vmem_limit_bytes=V7X_VMEM_LIMIT_BYTES)


def _dot(a, b):
    return jnp.dot(a, b, preferred_element_type=F32)


def _dot_nt(a, b):
    return lax.dot_general(a, b, (((1,), (1,)), ((), ())), preferred_element_type=F32)


def _dot_tn(a, b):
    return lax.dot_general(a, b, (((0,), (0,)), ((), ())), preferred_element_type=F32)


def _sigmoid(x):
    return 1.0 / (1.0 + jnp.exp(-x))


def _silu(x):
    return x * _sigmoid(x)


def _log1p_exp_neg_abs(x):
    return jnp.log(1.0 + jnp.exp(-jnp.abs(x)))


def _log_sigmoid(x):
    return jnp.minimum(x, 0.0) - _log1p_exp_neg_abs(x)


def _softplus(x):
    return jnp.maximum(x, 0.0) + _log1p_exp_neg_abs(x)


def _gelu_tanh(x):
    return 0.5 * x * (1.0 + jnp.tanh(math.sqrt(2.0 / math.pi) * (x + 0.044715 * (x * x * x))))


def _row_iota(shape):
    return lax.broadcasted_iota(jnp.int32, shape, 0)


def _col_iota(shape):
    return lax.broadcasted_iota(jnp.int32, shape, 1)


def _shift_rows(x, k):
    return pltpu.roll(x, k, 0)


def _shift_rows_up(x, k):
    return pltpu.roll(x, x.shape[0] - k, 0)


def _rmsnorm_kernel(x_ref, g_ref, o_ref):
    x = x_ref[...]
    y = x * lax.rsqrt(jnp.mean(x * x, axis=-1, keepdims=True) + EPS)
    o_ref[...] = (y * g_ref[...]).astype(o_ref.dtype)


def rmsnorm_rows(x, g, tm):
    m, d = x.shape
    return pl.pallas_call(
        _rmsnorm_kernel,
        out_shape=jax.ShapeDtypeStruct((m, d), BF16),
        grid=(m // tm,),
        in_specs=[pl.BlockSpec((tm, d), lambda i: (i, 0)), pl.BlockSpec((1, d), lambda i: (0, 0))],
        out_specs=pl.BlockSpec((tm, d), lambda i: (i, 0)),
        compiler_params=_params("parallel"),
        name="rmsnorm_rows",
    )(x, g)


def _matmul_kernel(a_ref, w_ref, o_ref):
    o_ref[...] = _dot(a_ref[...], w_ref[...]).astype(o_ref.dtype)


def matmul(a, w, tm, tn, out_dtype=F32):
    m, k = a.shape
    n = w.shape[1]
    return pl.pallas_call(
        _matmul_kernel,
        out_shape=jax.ShapeDtypeStruct((m, n), out_dtype),
        grid=(m // tm, n // tn),
        in_specs=[pl.BlockSpec((tm, k), lambda i, j: (i, 0)), pl.BlockSpec((k, tn), lambda i, j: (0, j))],
        out_specs=pl.BlockSpec((tm, tn), lambda i, j: (i, j)),
        compiler_params=_params("parallel", "parallel"),
        name="matmul",
    )(a, w)


def _merge_kernel(h_ref, b_ref, wm_ref, wb_ref, o_ref, acc_ref):
    c = pl.program_id(1)
    j = pl.program_id(2)
    nj, _, tn = acc_ref.shape
    contrib = _sigmoid(_dot(h_ref[...], wm_ref[...])) * _dot(b_ref[...], wb_ref[...])

    @pl.when(c == 0)
    def _():
        acc_ref[j] = contrib

    @pl.when(jnp.logical_and(c > 0, c < N_BRANCH - 1))
    def _():
        acc_ref[j] += contrib

    for jj in range(nj):
        @pl.when(jnp.logical_and(c == N_BRANCH - 1, j == jj))
        def _(jj=jj):
            o_ref[:, jj * tn:(jj + 1) * tn] = (acc_ref[jj] + contrib).astype(o_ref.dtype)


def merge_branches(h, branches, w_merge, w_branch, layer, tm, tn):
    m, d = h.shape
    w = branches.shape[2]
    nj = d // tn
    return pl.pallas_call(
        _merge_kernel,
        out_shape=jax.ShapeDtypeStruct((m, d), BF16),
        grid=(m // tm, N_BRANCH, nj),
        in_specs=[pl.BlockSpec((tm, d), lambda i, c, j: (i, 0)),
                  pl.BlockSpec((None, tm, w), lambda i, c, j: (c, i, 0)),
                  pl.BlockSpec((None, d, tn), lambda i, c, j: (layer, 0, c * nj + j)),
                  pl.BlockSpec((None, None, w, tn), lambda i, c, j: (layer, c, 0, j))],
        out_specs=pl.BlockSpec((tm, d), lambda i, c, j: (i, 0)),
        scratch_shapes=[pltpu.VMEM((nj, tm, tn), F32)],
        compiler_params=_params("parallel", "arbitrary", "arbitrary"),
        name="merge_branches",
    )(h, branches, w_merge, w_branch)


def _out_kernel(m_ref, w_ref, g_ref, x_ref, *rest):
    y = _dot(m_ref[...], w_ref[...])
    y = y * lax.rsqrt(jnp.mean(y * y, axis=-1, keepdims=True) + EPS)
    xn = x_ref[...] + y * g_ref[...]
    if len(rest) == 1:
        (o_ref,) = rest
    else:
        gn_ref, o_ref, h_ref = rest
        hn = xn * lax.rsqrt(jnp.mean(xn * xn, axis=-1, keepdims=True) + EPS)
        h_ref[...] = (hn * gn_ref[...]).astype(h_ref.dtype)
    o_ref[...] = xn


def out_proj_residual(merged, w_out, g, x, g_next, tm):
    m, d = x.shape
    rows = pl.BlockSpec((tm, d), lambda i: (i, 0))
    vec = pl.BlockSpec((1, d), lambda i: (0, 0))
    in_specs = [rows, pl.BlockSpec((d, d), lambda i: (0, 0)), vec, rows]
    args = [merged, w_out, g, x]
    out_shape = jax.ShapeDtypeStruct((m, d), F32)
    out_specs = rows
    if g_next is not None:
        in_specs.append(vec)
        args.append(g_next)
        out_shape = (out_shape, jax.ShapeDtypeStruct((m, d), BF16))
        out_specs = (rows, rows)
    return pl.pallas_call(
        _out_kernel,
        out_shape=out_shape,
        grid=(m // tm,),
        in_specs=in_specs,
        out_specs=out_specs,
        compiler_params=_params("parallel"),
        name="out_proj_residual",
    )(*args)


def _hg_lb_kernel(logit_ref, lb_ref):
    x = logit_ref[...]
    e = jnp.exp(x - jnp.max(x, axis=0, keepdims=True))
    sm = e / jnp.sum(e, axis=0, keepdims=True)
    acc = jnp.zeros_like(sm[0:1])
    lb_ref[0:1, :] = acc
    for l in range(1, DEPTH):
        acc = acc + sm[l:l + 1]
        lb_ref[l:l + 1, :] = acc


def hg_lower_bounds(logits):
    return pl.pallas_call(
        _hg_lb_kernel,
        out_shape=jax.ShapeDtypeStruct(logits.shape, F32),
        name="hg_lower_bounds",
    )(logits)


def _gla_kernel(*refs, mode, seq_len, nb, heads, dk, dv, has_init, sub):
    n_row = 4 if mode == "hgrn" else 5
    n_const = 2 if mode == "hgrn" else 3
    row_refs, consts, rest = refs[:n_row], refs[n_row:n_row + n_const], refs[n_row + n_const:]
    if has_init:
        s0_ref, rest = rest[0], rest[1:]
    o_ref, sout_ref, s_scr = rest[:3]
    scratch = rest[3:]

    @pl.when(pl.program_id(1) == 0)
    def _():
        if has_init:
            s_scr[...] = s0_ref[...]
        else:
            s_scr[...] = jnp.zeros_like(s_scr)

    rows = nb * seq_len
    for sc in range(sub):
        window = pl.ds(sc * rows, rows)
        _gla_chunk(*[r.at[window] for r in row_refs], *consts, o_ref.at[window], s_scr, *scratch,
                   mode=mode, seq_len=seq_len, nb=nb, heads=heads, dk=dk, dv=dv)

    @pl.when(pl.program_id(1) == pl.num_programs(1) - 1)
    def _():
        sout_ref[...] = s_scr[...]


def _gla_chunk(*refs, mode, seq_len, nb, heads, dk, dv):
    L = seq_len
    R = nb * L
    C = heads * dk
    if mode == "hgrn":
        q_ref, f_ref, v_ref, g_ref, lb_ref, gn_ref = refs[:6]
        rest = refs[6:]
    else:
        q_ref, k_ref, v_ref, r_ref, g_ref, wup_ref, bup_ref, gn_ref = refs[:8]
        rest = refs[8:]
    o_ref, s_scr, att_scr, qe_scr, kd_scr, b_scr = rest

    t = _row_iota((R, C)) % L
    if mode == "hgrn":
        lb = lb_ref[...]
        fz = f_ref[...]
        la = jnp.log(lb)
        lc = jnp.log1p(-lb) + _log_sigmoid(fz)
        g = jnp.maximum(la, lc) + _log1p_exp_neg_abs(la - lc)
        k = (1.0 - lb) * _sigmoid(-fz)
        q = q_ref[...]
    else:
        pre = jnp.dot(r_ref[...], wup_ref[...], preferred_element_type=F32, precision=HIGHEST)
        g = _log_sigmoid(pre + bup_ref[...]) * (1.0 / GLA_GATE_TEMP)
        k = k_ref[...]
        q = q_ref[...] * (dk ** -0.5)

    rr = _row_iota((R, R))
    cc = _col_iota((R, R))
    same_seq = (rr // L) == (cc // L)
    tri_incl = jnp.where(same_seq, jnp.where(cc <= rr, 1.0, 0.0), 0.0)
    tri_after = jnp.where(same_seq, jnp.where(cc > rr, 1.0, 0.0), 0.0)
    tri = jnp.concatenate([tri_incl, tri_after], axis=0).astype(BF16)
    g_hi = g.astype(BF16)
    g_r1 = g - g_hi.astype(F32)
    g_mid = g_r1.astype(BF16)
    g_lo = (g_r1 - g_mid.astype(F32)).astype(BF16)
    sums = _dot(tri, g_hi) + _dot(tri, g_mid) + _dot(tri, g_lo)
    b = sums[:R]
    rb = sums[R:]
    b_scr[...] = b
    qe_scr[...] = (q * jnp.exp(b)).astype(BF16)
    kd_scr[...] = (k * jnp.exp(rb)).astype(BF16)

    def add_level(level, qt, kt, mask):
        same = kt is qt
        qt = qt.astype(BF16)
        kt = qt if same else kt.astype(BF16)
        for h in range(heads):
            hs = slice(h * dk, (h + 1) * dk)
            p = jnp.where(mask, _dot_nt(qt[:, hs], kt[:, hs]), 0.0)
            if level == 0:
                att_scr[h] = p
            else:
                att_scr[h] += p

    add_level(0, q, k, rr == cc)
    level = 1
    m = 1
    while m < L:
        pos = t % (2 * m)
        upper = pos >= m
        if m == 1:
            d = jnp.where(upper, g, 0.0)
        elif m == 2:
            d = jnp.where(pos == 2, g,
                          jnp.where(pos == 3, g + _shift_rows(g, 1),
                                    jnp.where(pos == 0, _shift_rows_up(g, 1), 0.0)))
        else:
            b3 = b.reshape(R // (2 * m), 2 * m, C)
            d3 = b3 - b3[:, m - 1:m, :]
            d = -jnp.abs(d3.reshape(R, C))
        x = jnp.where(upper, q, k) * jnp.exp(d)
        blk = 2 * m
        pair = jnp.logical_and((rr // blk) == (cc // blk),
                               jnp.logical_and((rr % blk) >= m, (cc % blk) < m))
        add_level(level, x, x, pair)
        level += 1
        m *= 2

    gn = gn_ref[...]
    seq_o = _row_iota((R, dv)) // L
    seq_k = _row_iota((R, dk)) // L
    pad_rows = (-R) % V7X_LANES
    for h in range(heads):
        hs = slice(h * dk, (h + 1) * dk)
        vs = slice(h * dv, (h + 1) * dv)
        vh = v_ref[:, vs].astype(BF16)
        vh_p = jnp.concatenate([vh, jnp.zeros((pad_rows, dv), BF16)], axis=0) if pad_rows else vh
        o = _dot(att_scr[h].astype(BF16), vh)
        qe = qe_scr[:, hs]
        kd = kd_scr[:, hs]
        for j in range(nb):
            st = s_scr[j, h]
            inter = _dot(qe, st.astype(BF16))
            kd_j = kd
            if nb > 1:
                inter = jnp.where(seq_o == j, inter, 0.0)
                kd_j = jnp.where(seq_k == j, kd, jnp.zeros_like(kd))
            o = o + inter
            if pad_rows:
                kd_j = jnp.concatenate([kd_j, jnp.zeros((pad_rows, dk), BF16)], axis=0)
            upd = _dot_tn(kd_j, vh_p)
            dl = jnp.exp(b_scr[j * L + L - 1:j * L + L, hs])
            colb = jnp.broadcast_to(dl, (dk, dk)).T
            if dv != dk:
                colb = jnp.concatenate([colb] * (dv // dk), axis=1)
            s_scr[j, h] = colb * st + upd
        on = o * lax.rsqrt(jnp.mean(o * o, axis=-1, keepdims=True) + EPS) * gn
        o_ref[:, vs] = (on * _silu(g_ref[:, vs])).astype(o_ref.dtype)


def gla_branch(mode, col_blocks, consts, s0, out_prev, sout_prev, *, layer, slab, row_base, n_seq, seq_total,
               chunk, nb, heads, dk, dv, sub=1):
    L = chunk
    R = nb * L
    rows_blk = sub * R
    n_chunks = seq_total // (sub * L)
    base_blk = row_base // rows_blk
    width = heads * dv

    def rows(i, c):
        return base_blk + i * n_chunks + c

    in_specs = [pl.BlockSpec((rows_blk, w), functools.partial(lambda i, c, bi: (rows(i, c), bi), bi=bi))
                for (_, w, bi) in col_blocks]
    args = [a for (a, _, _) in col_blocks]
    for a in consts:
        in_specs.append(pl.BlockSpec(a.shape, lambda i, c: (0, 0)))
        args.append(a)
    has_init = s0 is not None
    state_spec = pl.BlockSpec((None, nb, heads, dk, dv), lambda i, c: (layer, i, 0, 0, 0))
    if has_init:
        in_specs.append(state_spec)
        args.append(s0)
    n_real = len(args)
    aliases = {}
    for out_idx, prev in enumerate((out_prev, sout_prev)):
        if prev is not None:
            aliases[len(args)] = out_idx
            in_specs.append(pl.BlockSpec(memory_space=pl.ANY))
            args.append(prev)
    n_in = len(args)

    def kern(*refs):
        _gla_kernel(*refs[:n_real], *refs[n_in:], mode=mode, seq_len=L, nb=nb, heads=heads, dk=dk, dv=dv,
                    has_init=has_init, sub=sub)

    return pl.pallas_call(
        kern,
        out_shape=(jax.ShapeDtypeStruct(out_prev.shape, BF16),
                   jax.ShapeDtypeStruct((DEPTH, n_seq, heads, dk, dv), F32)),
        grid=(n_seq // nb, n_chunks),
        in_specs=in_specs,
        out_specs=(pl.BlockSpec((None, rows_blk, width), lambda i, c: (slab, rows(i, c), 0)), state_spec),
        scratch_shapes=[pltpu.VMEM((nb, heads, dk, dv), F32), pltpu.VMEM((heads, R, R), F32),
                        pltpu.VMEM((R, heads * dk), BF16), pltpu.VMEM((R, heads * dk), BF16),
                        pltpu.VMEM((R, heads * dk), F32)],
        input_output_aliases=aliases,
        compiler_params=_params("parallel", "arbitrary"),
        name=mode + "_branch",
    )(*args)


def _lru_gates(xc, wr_ref, br_ref, wi_ref, bi_ref, lam_ref):
    xcb = xc.astype(BF16)
    r_parts, i_parts = [], []
    for blk in range(xc.shape[1] // LRU_BW):
        bs = slice(blk * LRU_BW, (blk + 1) * LRU_BW)
        r_parts.append(_dot(xcb[:, bs], wr_ref[blk]))
        i_parts.append(_dot(xcb[:, bs], wi_ref[blk]))
    r = _sigmoid(jnp.concatenate(r_parts, axis=1) + br_ref[...])
    ig = _sigmoid(jnp.concatenate(i_parts, axis=1) + bi_ref[...])
    log_a = (-LRU_C) * r * _softplus(-lam_ref[...])
    a = jnp.exp(log_a)
    u = jnp.sqrt(1.0 - jnp.exp(2.0 * log_a)) * (ig * xc)
    return a, u


def _lru_prompt_kernel(x_ref, g_ref, cw_ref, cb_ref, wr_ref, br_ref, wi_ref, bi_ref, lam_ref,
                       o_ref, hfin_ref, cfin_ref, tail_scr, hc_scr, a_scr, u_scr):
    tc, w = x_ref.shape
    sub = V7X_SUBLANES
    c = pl.program_id(1)

    @pl.when(c == 0)
    def _():
        tail_scr[...] = jnp.zeros_like(tail_scr)
        hc_scr[...] = jnp.zeros_like(hc_scr)

    x = x_ref[...]
    tail = tail_scr[...]
    row8 = _row_iota((sub, w))
    cw = cw_ref[...]
    xc = cb_ref[...] + cw[CONV_W - 1:CONV_W] * x
    for k in range(1, CONV_W):
        xs = _shift_rows(x, k)
        head = jnp.where(row8 < k, _shift_rows(tail, k), xs[0:sub])
        xs = jnp.concatenate([head, xs[sub:]], axis=0)
        xc = xc + cw[CONV_W - 1 - k:CONV_W - k] * xs
    last8 = x[tc - sub:tc]
    tail_scr[...] = last8

    a, u = _lru_gates(xc, wr_ref, br_ref, wi_ref, bi_ref, lam_ref)
    a_scr[...] = a
    u_scr[...] = u

    def body(i, carry):
        r0 = pl.multiple_of(i * sub, sub)
        at = a_scr[pl.ds(r0, sub), :]
        ut = u_scr[pl.ds(r0, sub), :]
        for k in (1, 2, 4):
            keep = row8 >= k
            ut = ut + at * jnp.where(keep, _shift_rows(ut, k), 0.0)
            at = at * jnp.where(keep, _shift_rows(at, k), 1.0)
        ht = ut + at * carry
        u_scr[pl.ds(r0, sub), :] = ht
        return jnp.broadcast_to(ht[sub - 1:sub], (sub, w))

    carry = lax.fori_loop(0, tc // sub, body, hc_scr[...])
    hc_scr[...] = carry
    o_ref[...] = (u_scr[...] * _silu(g_ref[...])).astype(o_ref.dtype)

    @pl.when(c == pl.num_programs(1) - 1)
    def _():
        hfin_ref[...] = carry[0:1]
        cfin_ref[...] = _shift_rows(last8, CONV_W - 1)[0:CONV_W - 1]


def lru_prompt(p_d, consts, out_prev, *, slab, n_seq, seq_total, chunk):
    w = BRANCH_W
    n_chunks = seq_total // chunk
    const_specs = [pl.BlockSpec(a.shape, functools.partial(lambda i, c, nd: (0,) * nd, nd=a.ndim)) for a in consts]

    def kern(*refs):
        n_in = 2 + len(consts)
        _lru_prompt_kernel(*refs[:n_in], *refs[n_in + 1:])

    return pl.pallas_call(
        kern,
        out_shape=(jax.ShapeDtypeStruct(out_prev.shape, BF16),
                   jax.ShapeDtypeStruct((n_seq, 1, w), F32),
                   jax.ShapeDtypeStruct((n_seq, CONV_W - 1, w), F32)),
        grid=(n_seq, n_chunks),
        in_specs=[pl.BlockSpec((chunk, w), lambda i, c: (i * n_chunks + c, 0)),
                  pl.BlockSpec((chunk, w), lambda i, c: (i * n_chunks + c, 1))] + const_specs
        + [pl.BlockSpec(memory_space=pl.ANY)],
        out_specs=(pl.BlockSpec((None, chunk, w), lambda i, c: (slab, i * n_chunks + c, 0)),
                   pl.BlockSpec((None, 1, w), lambda i, c: (i, 0, 0)),
                   pl.BlockSpec((None, CONV_W - 1, w), lambda i, c: (i, 0, 0))),
        scratch_shapes=[pltpu.VMEM((V7X_SUBLANES, w), F32), pltpu.VMEM((V7X_SUBLANES, w), F32),
                        pltpu.VMEM((chunk, w), F32), pltpu.VMEM((chunk, w), F32)],
        input_output_aliases={2 + len(consts): 0},
        compiler_params=_params("parallel", "arbitrary"),
        name="lru_prompt",
    )(p_d, p_d, *consts, out_prev)


def _lru_sample_kernel(x_ref, g_ref, buf_ref, h0_ref, cw_ref, cb_ref, wr_ref, br_ref, wi_ref, bi_ref, lam_ref,
                       prev_ref, o_ref, hfin_ref, cfin_ref, o_scr):
    del prev_ref
    steps = DEC_SEQ
    n = x_ref.shape[0] // steps
    step_rows = lambda t: pl.ds(t, n, stride=steps)
    xcat = [buf_ref[j] for j in range(CONV_W - 1)] + [x_ref[step_rows(t), :] for t in range(steps)]
    cw = cw_ref[...]
    xcs = []
    for t in range(steps):
        xc = cb_ref[...]
        for j in range(CONV_W):
            xc = xc + cw[j:j + 1] * xcat[t + j]
        xcs.append(xc)
    a, u = _lru_gates(jnp.concatenate(xcs, axis=0), wr_ref, br_ref, wi_ref, bi_ref, lam_ref)
    h = h0_ref[...]
    for t in range(steps):
        h = a[t * n:(t + 1) * n] * h + u[t * n:(t + 1) * n]
        o_scr[step_rows(t), :] = h * _silu(g_ref[step_rows(t), :])
    o_ref[...] = o_scr[...].astype(o_ref.dtype)
    hfin_ref[...] = h
    for j in range(CONV_W - 1):
        cfin_ref[j] = xcat[steps + j]


def lru_sample(p_d, conv_buf, h0, consts, out_prev, *, slab, row_base):
    w = BRANCH_W
    bw = LRU_BW
    n = h0.shape[0]
    rows = n * DEC_SEQ
    blk = row_base // rows
    nbk = w // bw
    conv_w_, conv_b_, w_r, b_r, w_i, b_i, lam = consts
    lane = lambda r: pl.BlockSpec((r, bw), lambda i: (0, i))
    wblk = pl.BlockSpec((1, bw, bw), lambda i: (i, 0, 0))
    tail = pl.BlockSpec((CONV_W - 1, n, bw), lambda i: (0, 0, i))
    return pl.pallas_call(
        _lru_sample_kernel,
        out_shape=(jax.ShapeDtypeStruct(out_prev.shape, BF16),
                   jax.ShapeDtypeStruct((n, w), F32),
                   jax.ShapeDtypeStruct((CONV_W - 1, n, w), F32)),
        grid=(nbk,),
        in_specs=[pl.BlockSpec((rows, bw), lambda i: (blk, i)), pl.BlockSpec((rows, bw), lambda i: (blk, nbk + i)),
                  tail, lane(n), lane(CONV_W), lane(1), wblk, lane(1), wblk, lane(1), lane(1),
                  pl.BlockSpec(memory_space=pl.ANY)],
        out_specs=(pl.BlockSpec((None, rows, bw), lambda i: (slab, blk, i)), lane(n), tail),
        scratch_shapes=[pltpu.VMEM((rows, bw), F32)],
        input_output_aliases={11: 0},
        compiler_params=_params("parallel"),
        name="lru_sample",
    )(p_d, p_d, conv_buf, h0, conv_w_, conv_b_, w_r, b_r, w_i, b_i, lam, out_prev)


def _mem_attn_kernel(q_ref, g_ref, k_ref, v_ref, prev_ref, o_ref, s_scr, p_scr, *, nbm, seq_rows):
    del prev_ref
    rows = q_ref.shape[0]
    q = (q_ref[...] * (MEM_HD ** -0.5)).astype(BF16)
    for h in range(MEM_HEADS):
        hs = slice(h * MEM_HD, (h + 1) * MEM_HD)
        for j in range(nbm):
            pair = h * nbm + j
            s_scr[pair * rows:(pair + 1) * rows, :] = _dot_nt(q[:, hs], k_ref[j, :, hs].astype(BF16))
    s = s_scr[...]
    e = jnp.exp(s - jnp.max(s, axis=-1, keepdims=True))
    p_scr[...] = (e / jnp.sum(e, axis=-1, keepdims=True)).astype(BF16)
    seq = _row_iota((rows, MEM_HD)) // seq_rows
    for h in range(MEM_HEADS):
        hs = slice(h * MEM_HD, (h + 1) * MEM_HD)
        acc = None
        for j in range(nbm):
            pair = h * nbm + j
            oh = _dot(p_scr[pair * rows:(pair + 1) * rows, :], v_ref[j, :, hs].astype(BF16))
            acc = oh if acc is None else jnp.where(seq == j, oh, acc)
        o_ref[:, hs] = (acc * _silu(g_ref[:, hs])).astype(o_ref.dtype)


def mem_attention(p_e, mem_k, mem_v, out_prev, *, layer, slab, row_base, n_seq, seq_total, rows_blk, nbm):
    w = BRANCH_W
    base_blk = row_base // rows_blk
    if nbm == 1:
        t_blocks = seq_total // rows_blk
        grid = (n_seq, t_blocks)
        rmap = lambda i, t: base_blk + i * t_blocks + t
    else:
        grid = (n_seq // nbm, 1)
        rmap = lambda i, t: base_blk + i
    mem_spec = pl.BlockSpec((None, nbm, N_MEM, w), lambda i, t: (layer, i, 0, 0))
    in_specs = [pl.BlockSpec((rows_blk, w), lambda i, t: (rmap(i, t), 0)),
                pl.BlockSpec((rows_blk, w), lambda i, t: (rmap(i, t), 1)),
                mem_spec, mem_spec, pl.BlockSpec(memory_space=pl.ANY)]
    args = [p_e, p_e, mem_k, mem_v, out_prev]
    aliases = {4: 0}
    return pl.pallas_call(
        functools.partial(_mem_attn_kernel, nbm=nbm, seq_rows=seq_total),
        out_shape=jax.ShapeDtypeStruct(out_prev.shape, BF16),
        grid=grid,
        in_specs=in_specs,
        out_specs=pl.BlockSpec((None, rows_blk, w), lambda i, t: (slab, rmap(i, t), 0)),
        scratch_shapes=[pltpu.VMEM((MEM_HEADS * nbm * rows_blk, N_MEM), F32),
                        pltpu.VMEM((MEM_HEADS * nbm * rows_blk, N_MEM), BF16)],
        input_output_aliases=aliases,
        compiler_params=_params("parallel", "arbitrary"),
        name="mem_attention",
    )(*args)


def _mem_attn_cache_kernel(q_ref, g_ref, k_hbm, v_hbm, prev_ref, o_ref, kbuf, vbuf, sems, s_scr, p_scr,
                           *, layer, nbm, seq_rows):
    del prev_ref
    i = pl.program_id(0)
    n_steps = pl.num_programs(0)

    def slab_copies(step, slot):
        out = []
        for h in range(MEM_HEADS):
            for which, (src, dst) in enumerate(((k_hbm, kbuf), (v_hbm, vbuf))):
                out.append(pltpu.make_async_copy(src.at[layer, pl.ds(step * nbm, nbm), :, h, :],
                                                 dst.at[slot, h], sems.at[slot, which, h]))
        return out

    slot = i % 2

    @pl.when(i == 0)
    def _():
        for c in slab_copies(0, 0):
            c.start()

    @pl.when(i + 1 < n_steps)
    def _():
        for c in slab_copies(i + 1, 1 - slot):
            c.start()

    for c in slab_copies(i, slot):
        c.wait()

    rows = q_ref.shape[0]
    q = (q_ref[...] * (MEM_HD ** -0.5)).astype(BF16)
    for h in range(MEM_HEADS):
        hs = slice(h * MEM_HD, (h + 1) * MEM_HD)
        for j in range(nbm):
            pair = h * nbm + j
            s_scr[pair * rows:(pair + 1) * rows, :] = _dot_nt(q[:, hs], kbuf[slot, h, j].astype(BF16))
    s = s_scr[...]
    e = jnp.exp(s - jnp.max(s, axis=-1, keepdims=True))
    p_scr[...] = (e / jnp.sum(e, axis=-1, keepdims=True)).astype(BF16)
    seq = _row_iota((rows, MEM_HD)) // seq_rows
    for h in range(MEM_HEADS):
        hs = slice(h * MEM_HD, (h + 1) * MEM_HD)
        acc = None
        for j in range(nbm):
            pair = h * nbm + j
            oh = _dot(p_scr[pair * rows:(pair + 1) * rows, :], vbuf[slot, h, j].astype(BF16))
            acc = oh if acc is None else jnp.where(seq == j, oh, acc)
        o_ref[:, hs] = (acc * _silu(g_ref[:, hs])).astype(o_ref.dtype)


def mem_attention_cache(p_e, cache_k, cache_v, out_prev, *, layer, slab, row_base, n_seq, seq_total, nbm):
    w = BRANCH_W
    rows_blk = nbm * seq_total
    base_blk = row_base // rows_blk
    buf = pltpu.VMEM((2, MEM_HEADS, nbm, N_MEM, MEM_HD), F32)
    return pl.pallas_call(
        functools.partial(_mem_attn_cache_kernel, layer=layer, nbm=nbm, seq_rows=seq_total),
        out_shape=jax.ShapeDtypeStruct(out_prev.shape, BF16),
        grid=(n_seq // nbm,),
        in_specs=[pl.BlockSpec((rows_blk, w), lambda i: (base_blk + i, 0)),
                  pl.BlockSpec((rows_blk, w), lambda i: (base_blk + i, 1)),
                  pl.BlockSpec(memory_space=pl.ANY), pl.BlockSpec(memory_space=pl.ANY),
                  pl.BlockSpec(memory_space=pl.ANY)],
        out_specs=pl.BlockSpec((None, rows_blk, w), lambda i: (slab, base_blk + i, 0)),
        scratch_shapes=[buf, buf, pltpu.SemaphoreType.DMA((2, 2, MEM_HEADS)),
                        pltpu.VMEM((MEM_HEADS * nbm * rows_blk, N_MEM), F32),
                        pltpu.VMEM((MEM_HEADS * nbm * rows_blk, N_MEM), BF16)],
        input_output_aliases={4: 0},
        compiler_params=_params("arbitrary"),
        name="mem_attention_cache",
    )(p_e, p_e, cache_k, cache_v, out_prev)


def _cmul(ar, ai, br, bi):
    return ar * br - ai * bi, ar * bi + ai * br


def _s5_abar(lr, li, log_dt):
    dt = jnp.exp(log_dt)
    mag = jnp.exp(lr * dt)
    return mag * jnp.cos(li * dt), mag * jnp.sin(li * dt)


def _s5_prep_kernel(lra, lia, dta, bra, bia, cfr, cfi, lrc, lic, dtc, ctr, cti,
                    m_ref, wre_ref, wim_ref, vre_ref, vim_ref):
    gi = S5_GB * S5_GROUP
    gp = S5_GB * S5_STATE
    lr, li = lra[...], lia[...]
    ar, ai = _s5_abar(lr, li, dta[...])
    den = lr * lr + li * li
    zr = ((ar - 1.0) * lr + ai * li) / den
    zi = (ai * lr - (ar - 1.0) * li) / den
    bbr, bbi = _cmul(zr, zi, bra[...], bia[...])
    wmask = (_row_iota((gi, gp)) // S5_GROUP) == (_col_iota((gi, gp)) // S5_STATE)
    mmask = (_row_iota((gi, gi)) // S5_GROUP) == (_col_iota((gi, gi)) // S5_GROUP)
    cr, ci = cfr[...], cfi[...]
    pr, pi = jnp.ones_like(ar), jnp.zeros_like(ar)
    for tau in range(S5_CHUNK):
        lrr, lii = _cmul(pr, pi, bbr, bbi)
        wre_ref[S5_CHUNK - 1 - tau] = jnp.where(wmask, lrr, 0.0).astype(BF16)
        wim_ref[S5_CHUNK - 1 - tau] = jnp.where(wmask, lii, 0.0).astype(BF16)
        x = (lax.dot_general(lrr[:, :S5_STATE], cr, (((1,), (1,)), ((), ())), precision=HIGHEST,
                             preferred_element_type=F32)
             - lax.dot_general(lii[:, :S5_STATE], ci, (((1,), (1,)), ((), ())), precision=HIGHEST,
                               preferred_element_type=F32))
        m_ref[tau] = jnp.where(mmask, x, 0.0).astype(BF16)
        pr, pi = _cmul(pr, pi, ar, ai)

    arc, aic = _s5_abar(lrc[...], lic[...], dtc[...])
    vmask = (_row_iota((gp, gi)) // S5_STATE) == (_col_iota((gp, gi)) // S5_GROUP)
    pr, pi = arc, aic
    for t in range(S5_CHUNK):
        vr, vi = _cmul(ctr[...], cti[...], pr, pi)
        vre_ref[t] = jnp.where(vmask, vr, 0.0).astype(BF16)
        vim_ref[t] = jnp.where(vmask, -vi, 0.0).astype(BF16)
        pr, pi = _cmul(pr, pi, arc, aic)


def s5_prepare(lam_re, lam_im, log_dt, b_re, b_im, c_re, c_im):
    dp, g, p = lam_re.shape
    i = S5_GROUP
    gi, gp = S5_GB * i, S5_GB * p
    nb = g // S5_GB

    def a_layout(x_gp):
        return jnp.broadcast_to(x_gp[:, :, None, None, :], (dp, g, i, S5_GB, p)).reshape(dp, g * i, gp)

    def c_layout(x_gp):
        return jnp.broadcast_to(x_gp[:, :, :, None], (dp, g, p, gi)).reshape(dp, g * p, gi)

    ldt = jnp.broadcast_to(log_dt[:, :, None], (dp, g, p))
    bt = lambda b: jnp.broadcast_to(jnp.transpose(b, (0, 1, 3, 2))[:, :, :, None, :],
                                    (dp, g, i, S5_GB, p)).reshape(dp, g * i, gp)
    ct = lambda c: jnp.broadcast_to(jnp.transpose(c, (0, 1, 3, 2))[:, :, :, None, :],
                                    (dp, g, p, S5_GB, i)).reshape(dp, g * p, gi)
    args = [a_layout(lam_re), a_layout(lam_im), a_layout(ldt), bt(b_re), bt(b_im),
            c_re.reshape(dp, g * i, p), c_im.reshape(dp, g * i, p),
            c_layout(lam_re), c_layout(lam_im), c_layout(ldt), ct(c_re), ct(c_im)]
    spec_a = pl.BlockSpec((None, gi, gp), lambda l, b: (l, b, 0))
    spec_f = pl.BlockSpec((None, gi, p), lambda l, b: (l, b, 0))
    spec_c = pl.BlockSpec((None, gp, gi), lambda l, b: (l, b, 0))
    out5 = lambda r, c: pl.BlockSpec((None, S5_CHUNK, None, r, c), lambda l, b: (l, 0, b, 0, 0))
    shp = lambda r, c: jax.ShapeDtypeStruct((dp, S5_CHUNK, nb, r, c), BF16)
    return pl.pallas_call(
        _s5_prep_kernel,
        out_shape=(shp(gi, gi), shp(gi, gp), shp(gi, gp), shp(gp, gi), shp(gp, gi)),
        grid=(dp, nb),
        in_specs=[spec_a] * 5 + [spec_f] * 2 + [spec_c] * 5,
        out_specs=(out5(gi, gi), out5(gi, gp), out5(gi, gp), out5(gp, gi), out5(gp, gi)),
        compiler_params=_params("parallel", "parallel"),
        name="s5_prepare",
    )(*args)


S5_GB_STEP = 1


def _s5_steps(u_ref):
    n = u_ref.shape[0] // S5_CHUNK
    return [u_ref[pl.ds(s, n, stride=S5_CHUNK), :].astype(BF16) for s in range(S5_CHUNK)]


def _s5_e_kernel(u_ref, wre_ref, wim_ref, ere_ref, eim_ref):
    gi, gp = S5_GB * S5_GROUP, S5_GB * S5_STATE
    us = _s5_steps(u_ref)
    for half in range(S5_GB_STEP):
        ucat = jnp.concatenate([u[:, half * gi:(half + 1) * gi] for u in us], axis=1)
        wre = jnp.concatenate([wre_ref[s, half] for s in range(S5_CHUNK)], axis=0)
        wim = jnp.concatenate([wim_ref[s, half] for s in range(S5_CHUNK)], axis=0)
        ere_ref[:, half * gp:(half + 1) * gp] = _dot(ucat, wre)
        eim_ref[:, half * gp:(half + 1) * gp] = _dot(ucat, wim)


def s5_chunk_inputs(p_a, wre, wim, layer, tr):
    rows = p_a.shape[0] // S5_CHUNK
    gi, gp = S5_GB * S5_GROUP, S5_GB * S5_STATE
    n_b = S5_GROUPS // (S5_GB * S5_GB_STEP)
    wspec = pl.BlockSpec((None, S5_CHUNK, S5_GB_STEP, gi, gp), lambda i, b: (layer, 0, b, 0, 0))
    ospec = pl.BlockSpec((tr, S5_GB_STEP * gp), lambda i, b: (i, b))
    oshape = jax.ShapeDtypeStruct((rows, S5_GROUPS * S5_STATE), F32)
    return pl.pallas_call(
        _s5_e_kernel,
        out_shape=(oshape, oshape),
        grid=(rows // tr, n_b),
        in_specs=[pl.BlockSpec((S5_CHUNK * tr, S5_GB_STEP * gi), lambda i, b: (i, b)), wspec, wspec],
        out_specs=(ospec, ospec),
        compiler_params=_params("parallel", "parallel"),
        name="s5_chunk_inputs",
    )(p_a, wre, wim)


def _s5_scan_kernel(ere, eim, h0r, h0i, lr_ref, li_ref, ldt_ref, hre, him, fpr, fpi, fsr, fsi,
                    *, n_seq, rows_per_seq):
    sub = V7X_SUBLANES
    cw = ere.shape[1]
    ar, ai = _s5_abar(lr_ref[...], li_ref[...], ldt_ref[...])
    a2 = _cmul(ar, ai, ar, ai)
    p1 = _cmul(*a2, *a2)
    p2 = _cmul(*p1, *p1)
    p4 = _cmul(*p2, *p2)
    p8 = _cmul(*p4, *p4)
    row8 = _row_iota((sub, cw))
    tr_, ti_ = jnp.ones((sub, cw), F32), jnp.zeros((sub, cw), F32)
    for bit, pw in ((1, p1), (2, p2), (4, p4)):
        nr, ni = _cmul(tr_, ti_, *pw)
        sel = (row8 & bit) != 0
        tr_, ti_ = jnp.where(sel, nr, tr_), jnp.where(sel, ni, ti_)

    base = n_seq * rows_per_seq
    n_s = h0r.shape[0]
    h0r_v, h0i_v = h0r[...], h0i[...]
    hre[base:base + n_s, :] = h0r_v
    him[base:base + n_s, :] = h0i_v
    dr, di = _cmul(p1[0], p1[1], h0r_v, h0i_v)
    fsr[...] = dr + ere[base:base + n_s, :]
    fsi[...] = di + eim[base:base + n_s, :]

    for n in range(n_seq):
        def body(j, carry, n=n):
            cr, ci = carry
            r0 = pl.multiple_of(n * rows_per_seq + j * sub, sub)
            xr = ere[pl.ds(r0, sub), :]
            xi = eim[pl.ds(r0, sub), :]
            for k, pw in ((1, p1), (2, p2), (4, p4)):
                keep = row8 >= k
                sr = jnp.where(keep, _shift_rows(xr, k), 0.0)
                si = jnp.where(keep, _shift_rows(xi, k), 0.0)
                mr, mi = _cmul(pw[0], pw[1], sr, si)
                xr, xi = xr + mr, xi + mi
            er = jnp.where(row8 >= 1, _shift_rows(xr, 1), 0.0)
            ei = jnp.where(row8 >= 1, _shift_rows(xi, 1), 0.0)
            qr, qi = _cmul(tr_, ti_, cr, ci)
            hre[pl.ds(r0, sub), :] = er + qr
            him[pl.ds(r0, sub), :] = ei + qi
            nr, ni = _cmul(p8[0], p8[1], cr, ci)
            return xr[sub - 1:sub] + nr, xi[sub - 1:sub] + ni

        zero = jnp.zeros((1, cw), F32)
        cr, ci = lax.fori_loop(0, rows_per_seq // sub, body, (zero, zero))
        fpr[n:n + 1, :] = cr
        fpi[n:n + 1, :] = ci


def s5_scan(ere, eim, h0r, h0i, lam_re_row, lam_im_row, log_dt_row, layer, *, n_seq, rows_per_seq, cw):
    rows, width = ere.shape
    n_s = h0r.shape[0]
    col = lambda r: pl.BlockSpec((r, cw), lambda j: (0, j))
    prow = pl.BlockSpec((None, 1, cw), lambda j: (layer, 0, j))
    shp = lambda r: jax.ShapeDtypeStruct((r, width), F32)
    return pl.pallas_call(
        functools.partial(_s5_scan_kernel, n_seq=n_seq, rows_per_seq=rows_per_seq),
        out_shape=(shp(rows), shp(rows), shp(n_seq), shp(n_seq), shp(n_s), shp(n_s)),
        grid=(width // cw,),
        in_specs=[col(rows), col(rows), col(n_s), col(n_s), prow, prow, prow],
        out_specs=(col(rows), col(rows), col(n_seq), col(n_seq), col(n_s), col(n_s)),
        compiler_params=_params("parallel"),
        name="s5_scan",
    )(ere, eim, h0r, h0i, lam_re_row, lam_im_row, log_dt_row)


def _s5_y_kernel(u_ref, m_ref, hre_ref, him_ref, vre_ref, vim_ref, y_ref):
    gi, gp = S5_GB * S5_GROUP, S5_GB * S5_STATE
    us = _s5_steps(u_ref)
    n = hre_ref.shape[0]
    for half in range(S5_GB_STEP):
        hs = slice(half * gp, (half + 1) * gp)
        ls = slice(half * gi, (half + 1) * gi)
        hr = hre_ref[:, hs].astype(BF16)
        hi = him_ref[:, hs].astype(BF16)
        zero_m = jnp.zeros((gi, gi), BF16)
        for t0 in range(0, S5_CHUNK, 2):
            t1 = t0 + 1
            lhs = jnp.concatenate([hr, hi] + [us[s][:, ls] for s in range(t1 + 1)], axis=1)
            rows = [jnp.concatenate([vre_ref[t0, half], vre_ref[t1, half]], axis=1),
                    jnp.concatenate([vim_ref[t0, half], vim_ref[t1, half]], axis=1)]
            for s in range(t1 + 1):
                left = m_ref[t0 - s, half] if s <= t0 else zero_m
                rows.append(jnp.concatenate([left, m_ref[t1 - s, half]], axis=1))
            acc = _dot(lhs, jnp.concatenate(rows, axis=0))
            y_ref[pl.ds(t0, n, stride=S5_CHUNK), ls] = acc[:, :gi]
            y_ref[pl.ds(t1, n, stride=S5_CHUNK), ls] = acc[:, gi:]


def s5_outputs(p_a, m, hre, him, vre, vim, layer, tr):
    rows = p_a.shape[0] // S5_CHUNK
    gi, gp = S5_GB * S5_GROUP, S5_GB * S5_STATE
    n_b = S5_GROUPS // (S5_GB * S5_GB_STEP)
    mspec = pl.BlockSpec((None, S5_CHUNK, S5_GB_STEP, gi, gi), lambda i, b: (layer, 0, b, 0, 0))
    vspec = pl.BlockSpec((None, S5_CHUNK, S5_GB_STEP, gp, gi), lambda i, b: (layer, 0, b, 0, 0))
    hspec = pl.BlockSpec((tr, S5_GB_STEP * gp), lambda i, b: (i, b))
    uspec = pl.BlockSpec((S5_CHUNK * tr, S5_GB_STEP * gi), lambda i, b: (i, b))
    return pl.pallas_call(
        _s5_y_kernel,
        out_shape=jax.ShapeDtypeStruct((p_a.shape[0], BRANCH_W), F32),
        grid=(rows // tr, n_b),
        in_specs=[uspec, mspec, hspec, hspec, vspec, vspec],
        out_specs=uspec,
        compiler_params=_params("parallel", "parallel"),
        name="s5_outputs",
    )(p_a, m, hre, him, vre, vim)


def _s5_epilogue_kernel(y_ref, u_ref, g_ref, d_ref, w_ref, o_ref):
    z = _gelu_tanh(y_ref[...] + d_ref[...] * u_ref[...])
    o = z * _sigmoid(_dot(z.astype(BF16), w_ref[...]))
    o_ref[...] = (o * _silu(g_ref[...])).astype(o_ref.dtype)


def s5_epilogue(y, p_a, d_skip, w_glu, tr):
    rows, w = y.shape
    return pl.pallas_call(
        _s5_epilogue_kernel,
        out_shape=jax.ShapeDtypeStruct((N_BRANCH, rows, w), BF16),
        grid=(rows // tr,),
        in_specs=[pl.BlockSpec((tr, w), lambda i: (i, 0)),
                  pl.BlockSpec((tr, w), lambda i: (i, 0)),
                  pl.BlockSpec((tr, w), lambda i: (i, 1)),
                  pl.BlockSpec((1, w), lambda i: (0, 0)),
                  pl.BlockSpec((w, w), lambda i: (0, 0))],
        out_specs=pl.BlockSpec((None, tr, w), lambda i: (0, i, 0)),
        compiler_params=_params("parallel"),
        name="s5_epilogue",
    )(y, p_a, p_a, d_skip, w_glu)


_IN_WIDTHS = (("a_x", 1024), ("a_g", 1024), ("b_q", 1024), ("b_f", 1024), ("b_i", 1024), ("b_g", 1024),
              ("c_q", 512), ("c_k", 512), ("c_v", 1024), ("c_r", 16), ("c_g", 1024),
              ("d_x", 1024), ("d_g", 1024), ("e_q", 1024), ("e_g", 1024))

TM_NORM = 512
TM_MM = 1088
TN_MM = 2048
TN_CQ = 1280
TM_MERGE = 1088
TN_MERGE = 1024
TM_OUT = 544
TR_S5 = 1088
TR_S5_EPI = 544
CW_S5_SCAN = 512
CHUNK_GLA = 128
SUB_GLA = 2
NB_GLA_SAMPLE = 8
CHUNK_LRU = 256
ROWS_MEM_PROMPT = 512
NB_MEM_SAMPLE = 8


def _in_cols(w_in_l):
    cols, off = {}, 0
    for name, width in _IN_WIDTHS:
        cols[name] = w_in_l[:, off:off + width]
        off += width
    return cols


def kernel(x_prompt, x_sample, mem_prompt, state_s5_re, state_s5_im, state_hgrn, state_gla, state_rglru, state_conv, cache_mem_k, cache_mem_v, norm_pre, norm_post, w_in, s5_lam_re, s5_lam_im, s5_log_dt, s5_b_re, s5_b_im, s5_c_re, s5_c_im, s5_d, s5_w_glu, hg_lb_logits, hg_norm, gla_w_up, gla_b_up, gla_norm, conv_w, conv_b, lru_w_r, lru_b_r, lru_w_i, lru_b_i, lru_lam, mem_norm, w_mem_k, w_mem_v, w_branch, w_merge, w_out):
    d, w = D_MODEL, BRANCH_W
    m_p, m_s = BATCH * SEQ, DEC_BATCH * DEC_SEQ
    m_all = m_p + m_s
    bf = lambda a: a.astype(BF16)
    row = lambda v: v[None, :]

    x = jnp.concatenate([x_prompt.reshape(m_p, d), x_sample.reshape(m_s, d)], axis=0)

    mem2 = mem_prompt.reshape(BATCH * N_MEM, d)
    mk_l, mv_l = [], []
    for l in range(DEPTH):
        mn = rmsnorm_rows(mem2, row(mem_norm[l]), TM_NORM)
        mk_l.append(matmul(mn, bf(w_mem_k[l]), BATCH * N_MEM, w // 2))
        mv_l.append(matmul(mn, bf(w_mem_v[l]), BATCH * N_MEM, w // 2))

    lb = hg_lower_bounds(hg_lb_logits)
    s5_m, s5_wre, s5_wim, s5_vre, s5_vim = s5_prepare(s5_lam_re, s5_lam_im, s5_log_dt, s5_b_re, s5_b_im,
                                                      s5_c_re, s5_c_im)
    gs = S5_GROUPS * S5_STATE
    lam_re_row = s5_lam_re.reshape(DEPTH, 1, gs)
    lam_im_row = s5_lam_im.reshape(DEPTH, 1, gs)
    log_dt_row = jnp.broadcast_to(s5_log_dt[:, :, None], (DEPTH, S5_GROUPS, S5_STATE)).reshape(DEPTH, 1, gs)
    s5_h0r = state_s5_re.reshape(DEPTH, DEC_BATCH, gs)
    s5_h0i = state_s5_im.reshape(DEPTH, DEC_BATCH, gs)
    w_up_pad = jnp.zeros((DEPTH, V7X_LANES, GLA_KEY_W), F32).at[:, :GLA_RANK].set(gla_w_up)

    w_merge_bf, w_branch_bf = bf(w_merge), bf(w_branch)
    hg_p = hg_s = gl_p = gl_s = None
    outs = {k: [] for k in ("p_s5r", "p_s5i", "p_lru", "p_conv", "s_s5r", "s_s5i", "s_lru", "s_conv")}
    for l in range(DEPTH):
        cols = _in_cols(w_in[l])
        w_a = bf(jnp.concatenate([cols["a_x"], cols["a_g"]], axis=1))
        w_b = bf(jnp.concatenate([cols["b_q"], cols["b_f"], cols["b_i"], cols["b_g"]], axis=1))
        w_cv = bf(jnp.concatenate([cols["c_v"], cols["c_g"]], axis=1))
        w_cq = bf(jnp.concatenate([cols["c_q"], cols["c_k"], cols["c_r"],
                                   jnp.zeros((d, TN_CQ - 2 * GLA_KEY_W - GLA_RANK), F32)], axis=1))
        w_d = bf(jnp.concatenate([cols["d_x"], cols["d_g"]], axis=1))
        w_e = bf(jnp.concatenate([cols["e_q"], cols["e_g"]], axis=1))

        if l == 0:
            h = rmsnorm_rows(x, row(norm_pre[0]), TM_NORM)
        p_a = matmul(h, w_a, TM_MM, TN_MM)
        p_b = matmul(h, w_b, TM_MM, TN_MM)
        p_cv = matmul(h, w_cv, TM_MM, TN_MM)
        p_cq = matmul(h, w_cq, TM_MM, TN_CQ)
        p_d = matmul(h, w_d, TM_MM, TN_MM)
        p_e = matmul(h, w_e, TM_MM, TN_MM)

        ere, eim = s5_chunk_inputs(p_a, s5_wre, s5_wim, l, TR_S5)
        hre, him, fpr, fpi, fsr, fsi = s5_scan(ere, eim, s5_h0r[l], s5_h0i[l], lam_re_row, lam_im_row, log_dt_row,
                                               l, n_seq=BATCH, rows_per_seq=SEQ // S5_CHUNK, cw=CW_S5_SCAN)
        y = s5_outputs(p_a, s5_m, hre, him, s5_vre, s5_vim, l, TR_S5)
        br = s5_epilogue(y, p_a, row(s5_d[l]), bf(s5_w_glu[l]), TR_S5_EPI)
        outs["p_s5r"].append(fpr); outs["p_s5i"].append(fpi)
        outs["s_s5r"].append(fsr); outs["s_s5i"].append(fsi)

        hg_consts = [row(lb[l]), row(hg_norm[l])]
        hg_kw = dict(layer=l, slab=1, heads=HG_HEADS, dk=HG_DK, dv=HG_DV)
        hg_cols = [(p_b, w, 0), (p_b, w, 1), (p_b, w, 2), (p_b, w, 3)]
        br, hg_p = gla_branch("hgrn", hg_cols, hg_consts, None, br, hg_p, row_base=0, n_seq=BATCH,
                              seq_total=SEQ, chunk=CHUNK_GLA, nb=1, sub=SUB_GLA, **hg_kw)
        br, hg_s = gla_branch("hgrn", hg_cols, hg_consts, state_hgrn, br, hg_s, row_base=m_p,
                              n_seq=DEC_BATCH, seq_total=DEC_SEQ, chunk=DEC_SEQ, nb=NB_GLA_SAMPLE, **hg_kw)

        gl_consts = [w_up_pad[l], row(gla_b_up[l]), row(gla_norm[l])]
        gl_kw = dict(layer=l, slab=2, heads=GLA_HEADS, dk=GLA_DK, dv=GLA_DV)
        gl_cols = [(p_cq, GLA_KEY_W, 0), (p_cq, GLA_KEY_W, 1), (p_cv, w, 0), (p_cq, V7X_LANES, w // V7X_LANES),
                   (p_cv, w, 1)]
        br, gl_p = gla_branch("gla", gl_cols, gl_consts, None, br, gl_p, row_base=0, n_seq=BATCH,
                              seq_total=SEQ, chunk=CHUNK_GLA, nb=1, sub=SUB_GLA, **gl_kw)
        br, gl_s = gla_branch("gla", gl_cols, gl_consts, state_gla, br, gl_s, row_base=m_p,
                              n_seq=DEC_BATCH, seq_total=DEC_SEQ, chunk=DEC_SEQ, nb=NB_GLA_SAMPLE, **gl_kw)

        lru_consts = [conv_w[l], row(conv_b[l]), bf(lru_w_r[l]), row(lru_b_r[l]), bf(lru_w_i[l]), row(lru_b_i[l]),
                      row(lru_lam[l])]
        br, hfin_p, cfin_p = lru_prompt(p_d, lru_consts, br, slab=3, n_seq=BATCH, seq_total=SEQ, chunk=CHUNK_LRU)
        br, hfin_s, cfin_s = lru_sample(p_d, jnp.transpose(state_conv[l], (1, 0, 2)), state_rglru[l],
                                        lru_consts, br, slab=3, row_base=m_p)
        outs["p_lru"].append(hfin_p); outs["p_conv"].append(cfin_p)
        outs["s_lru"].append(hfin_s); outs["s_conv"].append(cfin_s)

        br = mem_attention(p_e, mk_l[l].reshape(1, BATCH, N_MEM, w), mv_l[l].reshape(1, BATCH, N_MEM, w), br,
                           layer=0, slab=4, row_base=0, n_seq=BATCH, seq_total=SEQ, rows_blk=ROWS_MEM_PROMPT, nbm=1)
        br = mem_attention_cache(p_e, cache_mem_k, cache_mem_v, br, layer=l, slab=4, row_base=m_p, n_seq=DEC_BATCH,
                                 seq_total=DEC_SEQ, nbm=NB_MEM_SAMPLE)

        merged = merge_branches(h, br, w_merge_bf, w_branch_bf, l, TM_MERGE, TN_MERGE)
        if l + 1 < DEPTH:
            x, h = out_proj_residual(merged, bf(w_out[l]), row(norm_post[l]), x, row(norm_pre[l + 1]), TM_OUT)
        else:
            x = out_proj_residual(merged, bf(w_out[l]), row(norm_post[l]), x, None, TM_OUT)

    st = lambda k, shape: jnp.stack(outs[k]).reshape(shape)
    s5_p_shape = (DEPTH, BATCH, S5_GROUPS, S5_STATE)
    s5_s_shape = (DEPTH, DEC_BATCH, S5_GROUPS, S5_STATE)
    mem_shape = (DEPTH, BATCH, N_MEM, MEM_HEADS, MEM_HD)
    return (x[:m_p].reshape(BATCH, SEQ, d), x[m_p:].reshape(DEC_BATCH, DEC_SEQ, d),
            st("p_s5r", s5_p_shape), st("p_s5i", s5_p_shape), hg_p, gl_p,
            st("p_lru", (DEPTH, BATCH, w)), st("p_conv", (DEPTH, BATCH, CONV_W - 1, w)),
            jnp.stack(mk_l).reshape(mem_shape), jnp.stack(mv_l).reshape(mem_shape),
            st("s_s5r", s5_s_shape), st("s_s5i", s5_s_shape), hg_s, gl_s,
            st("s_lru", (DEPTH, DEC_BATCH, w)), jnp.transpose(jnp.stack(outs["s_conv"]), (0, 2, 1, 3)))
```
